```python
import math
import jax, jax.numpy as jnp
from jax import lax
import numpy as np

D_MODEL = 1024
BATCH = 8
SEQ = 4096
DEPTH = 1

HEAD_DIM = 64
A_Q_HEADS = 8
A_KV_HEADS = 2
A_GROUP = A_Q_HEADS // A_KV_HEADS
A_HALF_WINDOW = 128
A_BLOCK = 128
B_PATTERNS = ((128, 1), (512, 4), (2048, 16))
B_N_GROUPS = len(B_PATTERNS)
B_HEADS_PER_GROUP = 4
B_BLOCK = 64
ROPE_THETA = 10000.0
D_FF = 3 * D_MODEL
CONV_WIDTH = 3
RMS_EPS = 1e-6
NEG_INF = -1e30

A_Q_COLS = A_Q_HEADS * HEAD_DIM
A_KV_COLS = A_KV_HEADS * HEAD_DIM
A_COLS = A_Q_COLS + 2 * A_KV_COLS
B_PROJ_COLS = B_N_GROUPS * B_HEADS_PER_GROUP * HEAD_DIM
IN_COLS = A_COLS + 3 * B_PROJ_COLS
A_OUT = A_Q_COLS
B_OUT = B_HEADS_PER_GROUP * HEAD_DIM

kernel_name = "hybrid_gated_window_dilated_attention_convffn"


def rms_norm(x, gain):
    xf = x.astype(jnp.float32)
    y = xf * lax.rsqrt(jnp.mean(xf * xf, axis=-1, keepdims=True) + RMS_EPS)
    return (y * gain.astype(jnp.float32)).astype(x.dtype)


def rope(t, seq_len):
    dh = t.shape[-1]
    half = dh // 2
    inv = ROPE_THETA ** (-jnp.arange(half, dtype=jnp.float32) / half)
    ang = jnp.arange(seq_len, dtype=jnp.float32)[:, None] * inv[None, :]
    bshape = (1, seq_len) + (1,) * (t.ndim - 3) + (half,)
    cos = jnp.cos(ang).reshape(bshape)
    sin = jnp.sin(ang).reshape(bshape)
    tf = t.astype(jnp.float32)
    t1, t2 = tf[..., :half], tf[..., half:]
    return jnp.concatenate([t1 * cos - t2 * sin, t2 * cos + t1 * sin], axis=-1).astype(t.dtype)


def banded_attention(q, k, v, half_window, block, sink=None):
    b, L, hkv, g, dh = q.shape
    nb = -(-L // block)
    lp = nb * block
    qp = jnp.pad(q, ((0, 0), (0, lp - L), (0, 0), (0, 0), (0, 0))).reshape(b, nb, block, hkv, g, dh)

    def key_windows(t):
        tp = jnp.pad(t, ((0, 0), (block, lp - L + block), (0, 0), (0, 0))).reshape(b, nb + 2, block, hkv, dh)
        return jnp.concatenate([tp[:, :-2], tp[:, 1:-1], tp[:, 2:]], axis=2)

    kw = key_windows(k)
    vw = key_windows(v)
    qi = jnp.arange(nb)[:, None] * block + jnp.arange(block)[None, :]
    kj = (jnp.arange(nb)[:, None] - 1) * block + jnp.arange(3 * block)[None, :]
    kjb = kj[:, None, :]
    mask = (jnp.abs(kjb - qi[:, :, None]) <= half_window) & (kjb >= 0) & (kjb < L)

    s = jnp.einsum('bnqhgd,bnkhd->bnhgqk', qp.astype(jnp.float32), kw.astype(jnp.float32)) * (dh ** -0.5)
    s = jnp.where(mask[None, :, None, None], s, NEG_INF)
    m = jnp.max(s, axis=-1)
    if sink is not None:
        sk = sink.astype(jnp.float32)[None, None, :, :, None]
        m = jnp.maximum(m, sk)
    p = jnp.exp(s - m[..., None])
    denom = jnp.sum(p, axis=-1)
    if sink is not None:
        denom = denom + jnp.exp(sk - m)
    o = jnp.einsum('bnhgqk,bnkhd->bnqhgd', p, vw.astype(jnp.float32))
    o = o / jnp.moveaxis(denom, -1, 2)[..., None]
    o = o.reshape(b, lp, hkv, g, dh)[:, :L].astype(q.dtype)
    lse = jnp.moveaxis(m + jnp.log(denom), -1, 2).reshape(b, lp, hkv, g)[:, :L]
    return o, lse


def stride_gather(t, d):
    b, s = t.shape[0], t.shape[1]
    t = t.reshape((b, s // d, d) + t.shape[2:])
    t = jnp.moveaxis(t, 2, 1)
    return t.reshape((b * d, s // d) + t.shape[3:])


def stride_scatter(t, d, b):
    sd = t.shape[1]
    t = t.reshape((b, d, sd) + t.shape[2:])
    t = jnp.moveaxis(t, 1, 2)
    return t.reshape((b, sd * d) + t.shape[3:])


def depthwise_conv_centered(u, w, bias):
    up = jnp.pad(u, ((0, 0), (1, 1), (0, 0)))
    return up[:, :-2] * w[0] + up[:, 1:-1] * w[1] + up[:, 2:] * w[2] + bias


def token_mixer(h, w_in, sink, w_branch_a, w_branch_b, w_gate, b_gate, w_out):
    b, s, _ = h.shape
    proj = h @ w_in
    qa = proj[..., :A_Q_COLS].reshape(b, s, A_KV_HEADS, A_GROUP, HEAD_DIM)
    ka = proj[..., A_Q_COLS:A_Q_COLS + A_KV_COLS].reshape(b, s, A_KV_HEADS, HEAD_DIM)
    va = proj[..., A_Q_COLS + A_KV_COLS:A_COLS].reshape(b, s, A_KV_HEADS, HEAD_DIM)
    qa = rope(qa, s)
    ka = rope(ka, s)
    ya, _ = banded_attention(qa, ka, va, A_HALF_WINDOW, A_BLOCK,
                             sink=sink.reshape(A_KV_HEADS, A_GROUP))
    ya = ya.reshape(b, s, A_OUT)
    off = A_COLS
    qb = proj[..., off:off + B_PROJ_COLS].reshape(b, s, B_N_GROUPS, B_HEADS_PER_GROUP, HEAD_DIM)
    kb = proj[..., off + B_PROJ_COLS:off + 2 * B_PROJ_COLS].reshape(b, s, B_N_GROUPS, B_HEADS_PER_GROUP, HEAD_DIM)
    vb = proj[..., off + 2 * B_PROJ_COLS:off + 3 * B_PROJ_COLS].reshape(b, s, B_N_GROUPS, B_HEADS_PER_GROUP, HEAD_DIM)
    qb = rope(qb, s)
    kb = rope(kb, s)
    outs, lses = [], []
    for gi, (window, dil) in enumerate(B_PATTERNS):
        qg = stride_gather(qb[:, :, gi], dil)[:, :, :, None, :]
        kg = stride_gather(kb[:, :, gi], dil)
        vg = stride_gather(vb[:, :, gi], dil)
        og, lg = banded_attention(qg, kg, vg, window // (2 * dil), B_BLOCK)
        outs.append(stride_scatter(og[:, :, :, 0], dil, b))
        lses.append(stride_scatter(lg[:, :, :, 0], dil, b))
    outs = jnp.stack(outs, axis=2)
    wts = jax.nn.softmax(jnp.stack(lses, axis=2), axis=2)
    yb = jnp.sum(wts[..., None] * outs.astype(jnp.float32), axis=2).astype(h.dtype).reshape(b, s, B_OUT)
    gates = jax.nn.sigmoid((h @ w_gate + b_gate).astype(jnp.float32)).astype(h.dtype)
    ga, gb = gates[..., :D_MODEL], gates[..., D_MODEL:]
    merged = ga * (ya @ w_branch_a) + gb * (yb @ w_branch_b)
    return merged @ w_out


def conv_ffn(h, w_up, conv_w, conv_b, w_down):
    u = depthwise_conv_centered(h @ w_up, conv_w, conv_b)
    gate, up = u[..., :D_FF], u[..., D_FF:]
    return (jax.nn.gelu(gate, approximate=True) * up) @ w_down


def setup_inputs(seed: int = 0) -> dict:
    key = jax.random.key(seed)
    ks = jax.random.split(key, 20)
    f32 = jnp.float32

    def nrm(k, shape, scale):
        return jax.random.normal(k, shape, f32) * scale

    def gain(k):
        return 1.0 + 0.05 * jax.random.normal(k, (DEPTH, D_MODEL), f32)

    return {
        "x": jax.random.normal(ks[0], (BATCH, SEQ, D_MODEL), f32),
        "norm_mix_pre": gain(ks[1]),
        "w_in": nrm(ks[2], (DEPTH, D_MODEL, IN_COLS), D_MODEL ** -0.5),
        "sink": nrm(ks[3], (DEPTH, A_Q_HEADS), 1.0),
        "w_branch_a": nrm(ks[4], (DEPTH, A_OUT, D_MODEL), A_OUT ** -0.5),
        "w_branch_b": nrm(ks[5], (DEPTH, B_OUT, D_MODEL), B_OUT ** -0.5),
        "w_gate": nrm(ks[6], (DEPTH, D_MODEL, 2 * D_MODEL), D_MODEL ** -0.5),
        "b_gate": nrm(ks[7], (DEPTH, 2 * D_MODEL), 0.02),
        "w_out": nrm(ks[8], (DEPTH, D_MODEL, D_MODEL), D_MODEL ** -0.5),
        "norm_mix_post": gain(ks[9]),
        "norm_ffn_pre": gain(ks[10]),
        "w_up": nrm(ks[11], (DEPTH, D_MODEL, 2 * D_FF), D_MODEL ** -0.5),
        "conv_w": nrm(ks[12], (DEPTH, CONV_WIDTH, 2 * D_FF), CONV_WIDTH ** -0.5),
        "conv_b": nrm(ks[13], (DEPTH, 2 * D_FF), 0.02),
        "w_down": nrm(ks[14], (DEPTH, D_FF, D_MODEL), D_FF ** -0.5),
        "norm_ffn_post": gain(ks[15]),
    }


def reference(x, norm_mix_pre, w_in, sink, w_branch_a, w_branch_b, w_gate, b_gate, w_out,
              norm_mix_post, norm_ffn_pre, w_up, conv_w, conv_b, w_down, norm_ffn_post):
    for layer in range(DEPTH):
        h = rms_norm(x, norm_mix_pre[layer])
        mix = token_mixer(h, w_in[layer], sink[layer], w_branch_a[layer], w_branch_b[layer],
                          w_gate[layer], b_gate[layer], w_out[layer])
        x = x + rms_norm(mix, norm_mix_post[layer])
        h = rms_norm(x, norm_ffn_pre[layer])
        f = conv_ffn(h, w_up[layer], conv_w[layer], conv_b[layer], w_down[layer])
        x = x + rms_norm(f, norm_ffn_post[layer])
    return x
```

```python
import functools
import math

import jax
import jax.numpy as jnp
from jax import lax
from jax.experimental import pallas as pl
from jax.experimental.pallas import tpu as pltpu

D_MODEL = 1024
HEAD_DIM = 64
A_Q_HEADS = 8
A_KV_HEADS = 2
A_GROUP = A_Q_HEADS // A_KV_HEADS
A_HALF_WINDOW = 128
B_PATTERNS = ((128, 1), (512, 4), (2048, 16))
B_N_GROUPS = len(B_PATTERNS)
B_HEADS = 4
B_HALF_WINDOW = 64
ROPE_THETA = 10000.0
D_FF = 3 * D_MODEL
RMS_EPS = 1e-6
NEG_INF = -1e30

A_Q_COLS = A_Q_HEADS * HEAD_DIM
A_KV_COLS = A_KV_HEADS * HEAD_DIM
A_COLS = A_Q_COLS + 2 * A_KV_COLS
B_GROUP_COLS = B_HEADS * HEAD_DIM
B_PROJ_COLS = B_N_GROUPS * B_GROUP_COLS
IN_COLS = A_COLS + 3 * B_PROJ_COLS

LANES = 128
VMEM_LIMIT_BYTES = 56 * 1024 * 1024

ROW_TILE = 512
A_Q_BLOCK = 128
A_KEY_WINDOW = 3 * A_Q_BLOCK
A_STEP_ROWS = 1024
B_Q_BLOCK = 128
B_KEY_WINDOW = B_Q_BLOCK + 2 * B_HALF_WINDOW
B_SPAN = 2048
FF_CHUNK = 512
N_FF_CHUNKS = D_FF // FF_CHUNK
HALO = 8

BF16 = jnp.bfloat16
F32 = jnp.float32


def _dot(a, b):
    return jnp.dot(a, b, preferred_element_type=F32)


def _dot_nt(a, b):
    return lax.dot_general(a, b, (((1,), (1,)), ((), ())), preferred_element_type=F32)


def _rms_norm(x, gain):
    ms = jnp.mean(x * x, axis=-1, keepdims=True)
    return x * lax.rsqrt(ms + RMS_EPS) * gain


def _rope_chunk(p, cos, sin_signed, first_half):
    partner = jnp.where(first_half, pltpu.roll(p, 96, 1), pltpu.roll(p, 32, 1))
    return p * cos + partner * sin_signed


def _const_spec(shape):
    nd = len(shape)
    return pl.BlockSpec(shape, lambda *_: (0,) * nd, pipeline_mode=pl.Buffered(1))


def _in_proj_kernel(x_ref, gain_ref, w_ref, cos_ref, sin_ref,
                    qa_ref, ka_ref, va_ref,
                    qb0_ref, kb0_ref, vb0_ref,
                    qb1_ref, kb1_ref, vb1_ref,
                    qb2_ref, kb2_ref, vb2_ref,
                    slab_ref):
    t = x_ref.shape[1]
    h = _rms_norm(x_ref[0], gain_ref[...]).astype(BF16)
    cos = cos_ref[...]
    sin_signed = sin_ref[...]
    lane = lax.broadcasted_iota(jnp.int32, (t, LANES), 1)
    first_half = (lane % HEAD_DIM) < (HEAD_DIM // 2)
    low_head = lane < HEAD_DIM

    def proj(col0, ncols):
        return _dot(h, w_ref[:, col0:col0 + ncols])

    def rope(p, scale=None):
        chunks = []
        for c in range(p.shape[1] // LANES):
            r = _rope_chunk(p[:, c * LANES:(c + 1) * LANES], cos, sin_signed, first_half)
            chunks.append(r if scale is None else r * scale)
        return chunks

    def dup_heads(p):
        swapped = pltpu.roll(p, HEAD_DIM, 1)
        return [jnp.where(low_head, p, swapped), jnp.where(low_head, swapped, p)]

    scale = HEAD_DIM ** -0.5

    qa = rope(proj(0, A_Q_COLS), scale)
    for c, chunk in enumerate(qa):
        qa_ref[0, :, c * LANES:(c + 1) * LANES] = chunk.astype(BF16)
    ka = rope(proj(A_Q_COLS, A_KV_COLS))[0]
    for c, chunk in enumerate(dup_heads(ka)):
        ka_ref[0, :, c * LANES:(c + 1) * LANES] = chunk.astype(BF16)
    va = proj(A_Q_COLS + A_KV_COLS, A_KV_COLS)
    for c, chunk in enumerate(dup_heads(va)):
        va_ref[0, :, c * LANES:(c + 1) * LANES] = chunk.astype(BF16)

    out_refs = ((qb0_ref, kb0_ref, vb0_ref), (qb1_ref, kb1_ref, vb1_ref), (qb2_ref, kb2_ref, vb2_ref))
    for kind in range(3):
        for g, (_, dil) in enumerate(B_PATTERNS):
            col0 = A_COLS + kind * B_PROJ_COLS + g * B_GROUP_COLS
            p = proj(col0, B_GROUP_COLS)
            if kind == 0:
                chunks = rope(p, scale)
            elif kind == 1:
                chunks = rope(p)
            else:
                chunks = [p[:, c * LANES:(c + 1) * LANES] for c in range(B_GROUP_COLS // LANES)]
            o_ref = out_refs[g][kind]
            if dil == 1:
                for c, chunk in enumerate(chunks):
                    o_ref[0, 0, :, c * LANES:(c + 1) * LANES] = chunk.astype(BF16)
            else:
                for c, chunk in enumerate(chunks):
                    slab_ref[c] = chunk
                for c in range(len(chunks)):
                    for res in range(dil):
                        rows = slab_ref[c, pl.ds(res, t // dil, stride=dil), :]
                        o_ref[0, res, :, c * LANES:(c + 1) * LANES] = rows.astype(BF16)


def _in_proj(x, gain, w_in, cos, sin_signed):
    b, s, _ = x.shape
    t = ROW_TILE
    grid = (b, s // t)
    row_map = lambda bi, i: (bi, i, 0)
    out_shape = [
        jax.ShapeDtypeStruct((b, s, A_Q_COLS), BF16),
        jax.ShapeDtypeStruct((b, s, 2 * A_KV_COLS), BF16),
        jax.ShapeDtypeStruct((b, s, 2 * A_KV_COLS), BF16),
    ]
    out_specs = [
        pl.BlockSpec((1, t, A_Q_COLS), row_map),
        pl.BlockSpec((1, t, 2 * A_KV_COLS), row_map),
        pl.BlockSpec((1, t, 2 * A_KV_COLS), row_map),
    ]
    for _, dil in B_PATTERNS:
        for _ in range(3):
            out_shape.append(jax.ShapeDtypeStruct((b, dil, s // dil, B_GROUP_COLS), BF16))
            out_specs.append(pl.BlockSpec((1, dil, t // dil, B_GROUP_COLS), lambda bi, i: (bi, 0, i, 0)))
    return pl.pallas_call(
        _in_proj_kernel,
        grid=grid,
        in_specs=[
            pl.BlockSpec((1, t, D_MODEL), row_map),
            _const_spec((1, D_MODEL)),
            _const_spec((D_MODEL, IN_COLS)),
            pl.BlockSpec((t, LANES), lambda bi, i: (i, 0)),
            pl.BlockSpec((t, LANES), lambda bi, i: (i, 0)),
        ],
        out_specs=out_specs,
        out_shape=out_shape,
        scratch_shapes=[pltpu.VMEM((B_GROUP_COLS // LANES, t, LANES), F32)],
        compiler_params=pltpu.CompilerParams(
            dimension_semantics=("parallel", "parallel"), vmem_limit_bytes=VMEM_LIMIT_BYTES),
        name="in_proj",
    )(x, gain, w_in, cos, sin_signed)


def _attn_a_kernel(sink_ref, q_ref, k_ref, v_ref, o_ref):
    s = k_ref.shape[1]
    step_rows = q_ref.shape[1]
    blocks_per_step = step_rows // A_Q_BLOCK
    step = pl.program_id(1)
    row = lax.broadcasted_iota(jnp.int32, (A_Q_BLOCK, A_KEY_WINDOW), 0)
    col = lax.broadcasted_iota(jnp.int32, (A_Q_BLOCK, A_KEY_WINDOW), 1)
    delta = col - row
    lane = lax.broadcasted_iota(jnp.int32, (A_Q_BLOCK, LANES), 1)
    low_head = lane < HEAD_DIM

    def block(n, carry):
        r0 = pl.multiple_of(n * A_Q_BLOCK, A_Q_BLOCK)
        q0 = step * step_rows + r0
        start = jnp.clip(q0 - A_Q_BLOCK, 0, s - A_KEY_WINDOW)
        start = pl.multiple_of(start, A_Q_BLOCK)
        mask = jnp.abs(delta + (start - q0)) <= A_HALF_WINDOW
        k_win = k_ref[0, pl.ds(start, A_KEY_WINDOW), :]
        v_win = v_ref[0, pl.ds(start, A_KEY_WINDOW), :]
        q_blk = q_ref[0, pl.ds(r0, A_Q_BLOCK), :]
        for j in range(A_KV_HEADS):
            kd = k_win[:, j * LANES:(j + 1) * LANES]
            vd = v_win[:, j * LANES:(j + 1) * LANES]
            lhs = []
            for g in range(A_GROUP):
                c0 = j * A_GROUP * HEAD_DIM + (g // 2) * LANES
                pair = q_blk[:, c0:c0 + LANES]
                keep = low_head if g % 2 == 0 else jnp.logical_not(low_head)
                lhs.append(jnp.where(keep, pair, jnp.zeros_like(pair)))
            sc = _dot_nt(jnp.concatenate(lhs, axis=0), kd)
            probs, inv = [], []
            for g in range(A_GROUP):
                sg = jnp.where(mask, sc[g * A_Q_BLOCK:(g + 1) * A_Q_BLOCK], NEG_INF)
                sink = sink_ref[j * A_GROUP + g]
                m = jnp.maximum(jnp.max(sg, axis=-1, keepdims=True), sink)
                p = jnp.exp(sg - m)
                denom = jnp.sum(p, axis=-1, keepdims=True) + jnp.exp(sink - m)
                probs.append(p.astype(BF16))
                inv.append(1.0 / denom)
            o = _dot(jnp.concatenate(probs, axis=0), vd)
            for pr in range(A_GROUP // 2):
                lo = o[(2 * pr) * A_Q_BLOCK:(2 * pr + 1) * A_Q_BLOCK] * inv[2 * pr]
                hi = o[(2 * pr + 1) * A_Q_BLOCK:(2 * pr + 2) * A_Q_BLOCK] * inv[2 * pr + 1]
                c0 = j * A_GROUP * HEAD_DIM + pr * LANES
                o_ref[0, pl.ds(r0, A_Q_BLOCK), c0:c0 + LANES] = jnp.where(low_head, lo, hi).astype(BF16)
        return carry

    lax.fori_loop(0, blocks_per_step, block, 0)


def _attn_a(sink, qa, ka, va):
    b, s, _ = qa.shape
    step_rows = A_STEP_ROWS
    return pl.pallas_call(
        _attn_a_kernel,
        grid=(b, s // step_rows),
        in_specs=[
            pl.BlockSpec(memory_space=pltpu.SMEM),
            pl.BlockSpec((1, step_rows, A_Q_COLS), lambda bi, i: (bi, i, 0)),
            pl.BlockSpec((1, s, 2 * A_KV_COLS), lambda bi, i: (bi, 0, 0)),
            pl.BlockSpec((1, s, 2 * A_KV_COLS), lambda bi, i: (bi, 0, 0)),
        ],
        out_specs=pl.BlockSpec((1, step_rows, A_Q_COLS), lambda bi, i: (bi, i, 0)),
        out_shape=jax.ShapeDtypeStruct((b, s, A_Q_COLS), BF16),
        compiler_params=pltpu.CompilerParams(
            dimension_semantics=("parallel", "parallel"), vmem_limit_bytes=VMEM_LIMIT_BYTES),
        name="attn_a",
    )(sink, qa, ka, va)


def _attn_b_kernel(q_ref, k_ref, v_ref, o_ref, lse_ref, *, dil):
    sub_len = k_ref.shape[2]
    sub_rows = q_ref.shape[2]
    blocks = sub_rows // B_Q_BLOCK
    step = pl.program_id(1)
    row = lax.broadcasted_iota(jnp.int32, (2 * B_Q_BLOCK, B_KEY_WINDOW), 0) % B_Q_BLOCK
    col = lax.broadcasted_iota(jnp.int32, (2 * B_Q_BLOCK, B_KEY_WINDOW), 1)
    delta = col - row
    lane = lax.broadcasted_iota(jnp.int32, (B_Q_BLOCK, LANES), 1)
    low_head = lane < HEAD_DIM

    def unit(u, carry):
        res = u // blocks
        n = u % blocks
        r0 = pl.multiple_of(n * B_Q_BLOCK, B_Q_BLOCK)
        sub0 = step * sub_rows + r0
        start = jnp.clip(sub0 - B_HALF_WINDOW, 0, sub_len - B_KEY_WINDOW)
        start = pl.multiple_of(start, B_HALF_WINDOW)
        mask = jnp.abs(delta + (start - sub0)) <= B_HALF_WINDOW
        q_blk = q_ref[0, res, pl.ds(r0, B_Q_BLOCK), :]
        k_win = k_ref[0, res, pl.ds(start, B_KEY_WINDOW), :]
        v_win = v_ref[0, res, pl.ds(start, B_KEY_WINDOW), :]
        for pr in range(B_HEADS // 2):
            qp = q_blk[:, pr * LANES:(pr + 1) * LANES]
            kp = k_win[:, pr * LANES:(pr + 1) * LANES]
            vp = v_win[:, pr * LANES:(pr + 1) * LANES]
            zero = jnp.zeros_like(qp)
            lhs = jnp.concatenate([jnp.where(low_head, qp, zero), jnp.where(low_head, zero, qp)], axis=0)
            sc = jnp.where(mask, _dot_nt(lhs, kp), NEG_INF)
            m = jnp.max(sc, axis=-1, keepdims=True)
            p = jnp.exp(sc - m)
            denom = jnp.sum(p, axis=-1, keepdims=True)
            o = _dot(p.astype(BF16), vp) * (1.0 / denom)
            lse = m + jnp.log(denom)
            o_pair = jnp.where(low_head, o[:B_Q_BLOCK], o[B_Q_BLOCK:])
            lse_pair = jnp.where(low_head, lse[:B_Q_BLOCK], lse[B_Q_BLOCK:])
            if dil == 1:
                rows = pl.ds(r0, B_Q_BLOCK)
            else:
                rows = pl.ds(r0 * dil + res, B_Q_BLOCK, stride=dil)
            o_ref[0, pr, rows, :] = o_pair
            lse_ref[0, pr, rows, :] = lse_pair
        return carry

    lax.fori_loop(0, dil * blocks, unit, 0)


def _attn_b(q, k, v, dil):
    b, _, sub_len, _ = q.shape
    s = sub_len * dil
    sub_rows = B_SPAN // dil
    nslab = B_GROUP_COLS // LANES
    out_sds = jax.ShapeDtypeStruct((b, nslab, s, LANES), F32)
    out_spec = pl.BlockSpec((1, nslab, B_SPAN, LANES), lambda bi, i: (bi, 0, i, 0))
    return pl.pallas_call(
        functools.partial(_attn_b_kernel, dil=dil),
        grid=(b, s // B_SPAN),
        in_specs=[
            pl.BlockSpec((1, dil, sub_rows, B_GROUP_COLS), lambda bi, i: (bi, 0, i, 0)),
            pl.BlockSpec((1, dil, sub_len, B_GROUP_COLS), lambda bi, i: (bi, 0, 0, 0)),
            pl.BlockSpec((1, dil, sub_len, B_GROUP_COLS), lambda bi, i: (bi, 0, 0, 0)),
        ],
        out_specs=[out_spec, out_spec],
        out_shape=[out_sds, out_sds],
        compiler_params=pltpu.CompilerParams(
            dimension_semantics=("parallel", "parallel"), vmem_limit_bytes=VMEM_LIMIT_BYTES),
        name=f"attn_b_d{dil}",
    )(q, k, v)


def _mix_out_kernel(x_ref, ya_ref, o0_ref, l0_ref, o1_ref, l1_ref, o2_ref, l2_ref,
                    g_pre_ref, w_gate_ref, b_gate_ref, w_a_ref, w_b_ref, w_out_ref, g_post_ref,
                    out_ref):
    x = x_ref[0]
    h = _rms_norm(x, g_pre_ref[...]).astype(BF16)
    yb = []
    for c in range(B_GROUP_COLS // LANES):
        l0, l1, l2 = l0_ref[0, c], l1_ref[0, c], l2_ref[0, c]
        m = jnp.maximum(jnp.maximum(l0, l1), l2)
        e0, e1, e2 = jnp.exp(l0 - m), jnp.exp(l1 - m), jnp.exp(l2 - m)
        num = e0 * o0_ref[0, c] + e1 * o1_ref[0, c] + e2 * o2_ref[0, c]
        yb.append((num / (e0 + e1 + e2)).astype(BF16))
    yb = jnp.concatenate(yb, axis=-1)
    gates = jax.nn.sigmoid(_dot(h, w_gate_ref[...]) + b_gate_ref[...])
    merged = (gates[:, :D_MODEL] * _dot(ya_ref[0], w_a_ref[...])
              + gates[:, D_MODEL:] * _dot(yb, w_b_ref[...]))
    mix = _dot(merged.astype(BF16), w_out_ref[...])
    out_ref[0] = x + _rms_norm(mix, g_post_ref[...])


def _mix_out(x, ya, ob, g_pre, w_gate, b_gate, w_a, w_b, w_out, g_post):
    b, s, _ = x.shape
    t = ROW_TILE
    nslab = B_GROUP_COLS // LANES
    row_map = lambda bi, i: (bi, i, 0)
    slab_spec = pl.BlockSpec((1, nslab, t, LANES), lambda bi, i: (bi, 0, i, 0))
    flat_ob = [a for pair in ob for a in pair]
    return pl.pallas_call(
        _mix_out_kernel,
        grid=(b, s // t),
        in_specs=[
            pl.BlockSpec((1, t, D_MODEL), row_map),
            pl.BlockSpec((1, t, A_Q_COLS), row_map),
        ] + [slab_spec] * len(flat_ob) + [
            _const_spec((1, D_MODEL)),
            _const_spec((D_MODEL, 2 * D_MODEL)),
            _const_spec((1, 2 * D_MODEL)),
            _const_spec((A_Q_COLS, D_MODEL)),
            _const_spec((B_GROUP_COLS, D_MODEL)),
            _const_spec((D_MODEL, D_MODEL)),
            _const_spec((1, D_MODEL)),
        ],
        out_specs=pl.BlockSpec((1, t, D_MODEL), row_map),
        out_shape=jax.ShapeDtypeStruct((b, s, D_MODEL), F32),
        compiler_params=pltpu.CompilerParams(
            dimension_semantics=("parallel", "parallel"), vmem_limit_bytes=VMEM_LIMIT_BYTES),
        name="mix_out",
    )(x, ya, *flat_ob, g_pre, w_gate, b_gate, w_a, w_b, w_out, g_post)


def _gelu_tanh(x):
    c = math.sqrt(2.0 / math.pi)
    return 0.5 * x * (1.0 + jnp.tanh(c * (x + 0.044715 * (x * x * x))))


def _conv_ffn_kernel(x_ref, prev_ref, next_ref, g_pre_ref, w_up_ref, conv_w_ref, conv_b_ref,
                     w_down_ref, g_post_ref, out_ref, u_ref, acc_ref):
    t = x_ref.shape[1]
    i = pl.program_id(1)
    last = pl.num_programs(1) - 1
    x = x_ref[0]
    xe = jnp.concatenate([prev_ref[0], x, next_ref[0]], axis=0)
    row = lax.broadcasted_iota(jnp.int32, (t + 2 * HALO, 1), 0)
    valid = jnp.logical_and(jnp.logical_or(row >= HALO, i > 0),
                            jnp.logical_or(row < HALO + t, i < last))
    h = jnp.where(valid, _rms_norm(xe, g_pre_ref[...]), 0.0).astype(BF16)
    for c in range(N_FF_CHUNKS):
        u_ref[...] = _dot(h, w_up_ref[c])
        cw = conv_w_ref[c]
        u = (u_ref[HALO - 1:HALO - 1 + t] * cw[0:1]
             + u_ref[HALO:HALO + t] * cw[1:2]
             + u_ref[HALO + 1:HALO + 1 + t] * cw[2:3]
             + conv_b_ref[c])
        act = (_gelu_tanh(u[:, :FF_CHUNK]) * u[:, FF_CHUNK:]).astype(BF16)
        part = _dot(act, w_down_ref[c])
        if c == 0:
            acc_ref[...] = part
        else:
            acc_ref[...] += part
    out_ref[0] = x + _rms_norm(acc_ref[...], g_post_ref[...])


def _conv_ffn(x, g_pre, w_up, conv_w, conv_b, w_down, g_post):
    b, s, _ = x.shape
    t = ROW_TILE
    tiles = s // t
    halo_blocks_per_tile = t // HALO
    n_halo_blocks = s // HALO
    row_map = lambda bi, i: (bi, i, 0)
    prev_map = lambda bi, i: (bi, jnp.maximum(i * halo_blocks_per_tile - 1, 0), 0)
    next_map = lambda bi, i: (bi, jnp.minimum((i + 1) * halo_blocks_per_tile, n_halo_blocks - 1), 0)
    return pl.pallas_call(
        _conv_ffn_kernel,
        grid=(b, tiles),
        in_specs=[
            pl.BlockSpec((1, t, D_MODEL), row_map),
            pl.BlockSpec((1, HALO, D_MODEL), prev_map),
            pl.BlockSpec((1, HALO, D_MODEL), next_map),
            _const_spec((1, D_MODEL)),
            _const_spec((N_FF_CHUNKS, D_MODEL, 2 * FF_CHUNK)),
            _const_spec((N_FF_CHUNKS, 3, 2 * FF_CHUNK)),
            _const_spec((N_FF_CHUNKS, 1, 2 * FF_CHUNK)),
            _const_spec((N_FF_CHUNKS, FF_CHUNK, D_MODEL)),
            _const_spec((1, D_MODEL)),
        ],
        out_specs=pl.BlockSpec((1, t, D_MODEL), row_map),
        out_shape=jax.ShapeDtypeStruct((b, s, D_MODEL), F32),
        scratch_shapes=[
            pltpu.VMEM((t + 2 * HALO, 2 * FF_CHUNK), F32),
            pltpu.VMEM((t, D_MODEL), F32),
        ],
        compiler_params=pltpu.CompilerParams(
            dimension_semantics=("parallel", "parallel"), vmem_limit_bytes=VMEM_LIMIT_BYTES),
        name="conv_ffn",
    )(x, x, x, g_pre, w_up, conv_w, conv_b, w_down, g_post)


def _rope_tables(seq_len):
    half = HEAD_DIM // 2
    inv = ROPE_THETA ** (-jnp.arange(half, dtype=F32) / half)
    ang = jnp.arange(seq_len, dtype=F32)[:, None] * inv[None, :]
    cos, sin = jnp.cos(ang), jnp.sin(ang)
    cos_head = jnp.concatenate([cos, cos], axis=-1)
    sin_head = jnp.concatenate([-sin, sin], axis=-1)
    reps = LANES // HEAD_DIM
    return jnp.tile(cos_head, (1, reps)), jnp.tile(sin_head, (1, reps))


def _chunk_gate_up(a):
    lead = a.shape[:-1]
    a = a.reshape(lead + (2, N_FF_CHUNKS, FF_CHUNK))
    a = jnp.moveaxis(a, -2, 0)
    return a.reshape((N_FF_CHUNKS,) + lead + (2 * FF_CHUNK,))


def kernel(x, norm_mix_pre, w_in, sink, w_branch_a, w_branch_b, w_gate, b_gate, w_out,
           norm_mix_post, norm_ffn_pre, w_up, conv_w, conv_b, w_down, norm_ffn_post):
    b, s, d = x.shape
    assert d == D_MODEL and s % B_SPAN == 0 and s % A_STEP_ROWS == 0
    cos, sin_signed = _rope_tables(s)
    for layer in range(norm_mix_pre.shape[0]):
        proj = _in_proj(x, norm_mix_pre[layer][None], w_in[layer].astype(BF16), cos, sin_signed)
        qa, ka, va = proj[:3]
        ya = _attn_a(sink[layer], qa, ka, va)
        ob = []
        for g, (_, dil) in enumerate(B_PATTERNS):
            qb, kb, vb = proj[3 + 3 * g:6 + 3 * g]
            ob.append(_attn_b(qb, kb, vb, dil))
        x = _mix_out(x, ya, ob, norm_mix_pre[layer][None],
                     w_gate[layer].astype(BF16), b_gate[layer][None],
                     w_branch_a[layer].astype(BF16), w_branch_b[layer].astype(BF16),
                     w_out[layer].astype(BF16), norm_mix_post[layer][None])
        x = _conv_ffn(x, norm_ffn_pre[layer][None],
                      _chunk_gate_up(w_up[layer]).astype(BF16),
                      _chunk_gate_up(conv_w[layer]), _chunk_gate_up(conv_b[layer][None]),
                      w_down[layer].reshape(N_FF_CHUNKS, FF_CHUNK, D_MODEL).astype(BF16),
                      norm_ffn_post[layer][None])
    return x
```

```python
import functools
import math

import jax
import jax.numpy as jnp
from jax import lax
from jax.experimental import pallas as pl
from jax.experimental.pallas import tpu as pltpu

D_MODEL = 1024
HEAD_DIM = 64
A_Q_HEADS = 8
A_KV_HEADS = 2
A_GROUP = A_Q_HEADS // A_KV_HEADS
A_HALF_WINDOW = 128
B_PATTERNS = ((128, 1), (512, 4), (2048, 16))
B_N_GROUPS = len(B_PATTERNS)
B_HEADS = 4
B_HALF_WINDOW = 64
ROPE_THETA = 10000.0
D_FF = 3 * D_MODEL
RMS_EPS = 1e-6
NEG_INF = -1e30

A_Q_COLS = A_Q_HEADS * HEAD_DIM
A_KV_COLS = A_KV_HEADS * HEAD_DIM
A_COLS = A_Q_COLS + 2 * A_KV_COLS
B_GROUP_COLS = B_HEADS * HEAD_DIM
B_PROJ_COLS = B_N_GROUPS * B_GROUP_COLS
IN_COLS = A_COLS + 3 * B_PROJ_COLS

LANES = 128
VMEM_LIMIT_BYTES = 56 * 1024 * 1024

ROW_TILE = 512
A_Q_BLOCK = 128
A_KEY_WINDOW = 3 * A_Q_BLOCK
A_STEP_ROWS = 1024
B_Q_BLOCK = 128
B_KEY_WINDOW = B_Q_BLOCK + 2 * B_HALF_WINDOW
B_SPAN = 2048
A_UNROLL = 2
B_UNROLL = 4
FF_CHUNK = 512
N_FF_CHUNKS = D_FF // FF_CHUNK
HALO = 8

BF16 = jnp.bfloat16
F32 = jnp.float32


def _dot(a, b):
    return jnp.dot(a, b, preferred_element_type=F32)


def _dot_nt(a, b):
    return lax.dot_general(a, b, (((1,), (1,)), ((), ())), preferred_element_type=F32)


def _rms_norm(x, gain):
    ms = jnp.mean(x * x, axis=-1, keepdims=True)
    return x * lax.rsqrt(ms + RMS_EPS) * gain


def _rope_chunk(p, cos, sin_signed, first_half):
    partner = jnp.where(first_half, pltpu.roll(p, 96, 1), pltpu.roll(p, 32, 1))
    return p * cos + partner * sin_signed


def _const_spec(shape):
    nd = len(shape)
    return pl.BlockSpec(shape, lambda *_: (0,) * nd, pipeline_mode=pl.Buffered(1))


def _in_proj_kernel(x_ref, gain_ref, w_ref, cos_ref, sin_ref,
                    qa_ref, ka_ref, va_ref,
                    qb0_ref, kb0_ref, vb0_ref,
                    qb1_ref, kb1_ref, vb1_ref,
                    qb2_ref, kb2_ref, vb2_ref,
                    slab_ref):
    t = x_ref.shape[1]
    h = _rms_norm(x_ref[0], gain_ref[...]).astype(BF16)
    cos = cos_ref[...]
    sin_signed = sin_ref[...]
    lane = lax.broadcasted_iota(jnp.int32, (t, LANES), 1)
    first_half = (lane % HEAD_DIM) < (HEAD_DIM // 2)
    low_head = lane < HEAD_DIM

    def proj(col0, ncols):
        return _dot(h, w_ref[:, col0:col0 + ncols])

    def rope(p, scale=None):
        chunks = []
        for c in range(p.shape[1] // LANES):
            r = _rope_chunk(p[:, c * LANES:(c + 1) * LANES], cos, sin_signed, first_half)
            chunks.append(r if scale is None else r * scale)
        return chunks

    def dup_heads(p):
        swapped = pltpu.roll(p, HEAD_DIM, 1)
        return [jnp.where(low_head, p, swapped), jnp.where(low_head, swapped, p)]

    scale = HEAD_DIM ** -0.5

    qa = rope(proj(0, A_Q_COLS), scale)
    for c, chunk in enumerate(qa):
        qa_ref[0, :, c * LANES:(c + 1) * LANES] = chunk.astype(BF16)
    ka = rope(proj(A_Q_COLS, A_KV_COLS))[0]
    for c, chunk in enumerate(dup_heads(ka)):
        ka_ref[0, :, c * LANES:(c + 1) * LANES] = chunk.astype(BF16)
    va = proj(A_Q_COLS + A_KV_COLS, A_KV_COLS)
    for c, chunk in enumerate(dup_heads(va)):
        va_ref[0, :, c * LANES:(c + 1) * LANES] = chunk.astype(BF16)

    out_refs = ((qb0_ref, kb0_ref, vb0_ref), (qb1_ref, kb1_ref, vb1_ref), (qb2_ref, kb2_ref, vb2_ref))
    for kind in range(3):
        for g, (_, dil) in enumerate(B_PATTERNS):
            col0 = A_COLS + kind * B_PROJ_COLS + g * B_GROUP_COLS
            p = proj(col0, B_GROUP_COLS)
            if kind == 0:
                chunks = rope(p, scale)
            elif kind == 1:
                chunks = rope(p)
            else:
                chunks = [p[:, c * LANES:(c + 1) * LANES] for c in range(B_GROUP_COLS // LANES)]
            o_ref = out_refs[g][kind]
            if dil == 1:
                for c, chunk in enumerate(chunks):
                    o_ref[0, 0, :, c * LANES:(c + 1) * LANES] = chunk.astype(BF16)
            else:
                for c, chunk in enumerate(chunks):
                    slab_ref[c] = chunk
                for c in range(len(chunks)):
                    for res in range(dil):
                        rows = slab_ref[c, pl.ds(res, t // dil, stride=dil), :]
                        o_ref[0, res, :, c * LANES:(c + 1) * LANES] = rows.astype(BF16)


def _in_proj(x, gain, w_in, cos, sin_signed):
    b, s, _ = x.shape
    t = ROW_TILE
    grid = (b, s // t)
    row_map = lambda bi, i: (bi, i, 0)
    out_shape = [
        jax.ShapeDtypeStruct((b, s, A_Q_COLS), BF16),
        jax.ShapeDtypeStruct((b, s, 2 * A_KV_COLS), BF16),
        jax.ShapeDtypeStruct((b, s, 2 * A_KV_COLS), BF16),
    ]
    out_specs = [
        pl.BlockSpec((1, t, A_Q_COLS), row_map),
        pl.BlockSpec((1, t, 2 * A_KV_COLS), row_map),
        pl.BlockSpec((1, t, 2 * A_KV_COLS), row_map),
    ]
    for _, dil in B_PATTERNS:
        for _ in range(3):
            out_shape.append(jax.ShapeDtypeStruct((b, dil, s // dil, B_GROUP_COLS), BF16))
            out_specs.append(pl.BlockSpec((1, dil, t // dil, B_GROUP_COLS), lambda bi, i: (bi, 0, i, 0)))
    return pl.pallas_call(
        _in_proj_kernel,
        grid=grid,
        in_specs=[
            pl.BlockSpec((1, t, D_MODEL), row_map),
            _const_spec((1, D_MODEL)),
            _const_spec((D_MODEL, IN_COLS)),
            pl.BlockSpec((t, LANES), lambda bi, i: (i, 0)),
            pl.BlockSpec((t, LANES), lambda bi, i: (i, 0)),
        ],
        out_specs=out_specs,
        out_shape=out_shape,
        scratch_shapes=[pltpu.VMEM((B_GROUP_COLS // LANES, t, LANES), F32)],
        compiler_params=pltpu.CompilerParams(
            dimension_semantics=("parallel", "parallel"), vmem_limit_bytes=VMEM_LIMIT_BYTES),
        name="in_proj",
    )(x, gain, w_in, cos, sin_signed)


def _attn_a_kernel(sink_ref, bias_ref, q_ref, k_ref, v_ref, o_ref):
    s = k_ref.shape[1]
    step_rows = q_ref.shape[1]
    blocks_per_step = step_rows // A_Q_BLOCK
    step = pl.program_id(1)
    lane = lax.broadcasted_iota(jnp.int32, (A_Q_BLOCK, LANES), 1)
    low_head = lane < HEAD_DIM

    def block(n, carry):
        r0 = pl.multiple_of(n * A_Q_BLOCK, A_Q_BLOCK)
        q0 = step * step_rows + r0
        start = jnp.clip(q0 - A_Q_BLOCK, 0, s - A_KEY_WINDOW)
        start = pl.multiple_of(start, A_Q_BLOCK)
        bias = bias_ref[(q0 - start) // A_Q_BLOCK]
        k_win = k_ref[0, pl.ds(start, A_KEY_WINDOW), :]
        v_win = v_ref[0, pl.ds(start, A_KEY_WINDOW), :]
        q_blk = q_ref[0, pl.ds(r0, A_Q_BLOCK), :]
        for j in range(A_KV_HEADS):
            kd = k_win[:, j * LANES:(j + 1) * LANES]
            vd = v_win[:, j * LANES:(j + 1) * LANES]
            lhs = []
            for g in range(A_GROUP):
                c0 = j * A_GROUP * HEAD_DIM + (g // 2) * LANES
                pair = q_blk[:, c0:c0 + LANES]
                keep = low_head if g % 2 == 0 else jnp.logical_not(low_head)
                lhs.append(jnp.where(keep, pair, jnp.zeros_like(pair)))
            sc = _dot_nt(jnp.concatenate(lhs, axis=0), kd)
            probs, inv = [], []
            for g in range(A_GROUP):
                sg = sc[g * A_Q_BLOCK:(g + 1) * A_Q_BLOCK] + bias
                sink = sink_ref[j * A_GROUP + g]
                m = jnp.maximum(jnp.max(sg, axis=-1, keepdims=True), sink)
                p = jnp.exp(sg - m)
                denom = jnp.sum(p, axis=-1, keepdims=True) + jnp.exp(sink - m)
                probs.append(p.astype(BF16))
                inv.append(1.0 / denom)
            o = _dot(jnp.concatenate(probs, axis=0), vd)
            for pr in range(A_GROUP // 2):
                lo = o[(2 * pr) * A_Q_BLOCK:(2 * pr + 1) * A_Q_BLOCK] * inv[2 * pr]
                hi = o[(2 * pr + 1) * A_Q_BLOCK:(2 * pr + 2) * A_Q_BLOCK] * inv[2 * pr + 1]
                c0 = j * A_GROUP * HEAD_DIM + pr * LANES
                o_ref[0, pl.ds(r0, A_Q_BLOCK), c0:c0 + LANES] = jnp.where(low_head, lo, hi).astype(BF16)
        return carry

    lax.fori_loop(0, blocks_per_step, block, 0, unroll=A_UNROLL)


def _attn_a(sink, bias, qa, ka, va):
    b, s, _ = qa.shape
    step_rows = A_STEP_ROWS
    return pl.pallas_call(
        _attn_a_kernel,
        grid=(b, s // step_rows),
        in_specs=[
            pl.BlockSpec(memory_space=pltpu.SMEM),
            _const_spec(bias.shape),
            pl.BlockSpec((1, step_rows, A_Q_COLS), lambda bi, i: (bi, i, 0)),
            pl.BlockSpec((1, s, 2 * A_KV_COLS), lambda bi, i: (bi, 0, 0)),
            pl.BlockSpec((1, s, 2 * A_KV_COLS), lambda bi, i: (bi, 0, 0)),
        ],
        out_specs=pl.BlockSpec((1, step_rows, A_Q_COLS), lambda bi, i: (bi, i, 0)),
        out_shape=jax.ShapeDtypeStruct((b, s, A_Q_COLS), BF16),
        compiler_params=pltpu.CompilerParams(
            dimension_semantics=("parallel", "parallel"), vmem_limit_bytes=VMEM_LIMIT_BYTES),
        name="attn_a",
    )(sink, bias, qa, ka, va)


def _attn_b_kernel(bias_ref, q_ref, k_ref, v_ref, o_ref, lse_ref, *, dil):
    sub_len = k_ref.shape[2]
    sub_rows = q_ref.shape[2]
    blocks = sub_rows // B_Q_BLOCK
    step = pl.program_id(1)
    lane = lax.broadcasted_iota(jnp.int32, (B_Q_BLOCK, LANES), 1)
    low_head = lane < HEAD_DIM

    def unit(u, carry):
        res = u // blocks
        n = u % blocks
        r0 = pl.multiple_of(n * B_Q_BLOCK, B_Q_BLOCK)
        sub0 = step * sub_rows + r0
        start = jnp.clip(sub0 - B_HALF_WINDOW, 0, sub_len - B_KEY_WINDOW)
        start = pl.multiple_of(start, B_HALF_WINDOW)
        bias = bias_ref[(sub0 - start) // B_HALF_WINDOW]
        q_blk = q_ref[0, res, pl.ds(r0, B_Q_BLOCK), :]
        k_win = k_ref[0, res, pl.ds(start, B_KEY_WINDOW), :]
        v_win = v_ref[0, res, pl.ds(start, B_KEY_WINDOW), :]
        for pr in range(B_HEADS // 2):
            qp = q_blk[:, pr * LANES:(pr + 1) * LANES]
            kp = k_win[:, pr * LANES:(pr + 1) * LANES]
            vp = v_win[:, pr * LANES:(pr + 1) * LANES]
            zero = jnp.zeros_like(qp)
            lhs = jnp.concatenate([jnp.where(low_head, qp, zero), jnp.where(low_head, zero, qp)], axis=0)
            sc = _dot_nt(lhs, kp) + bias
            m = jnp.max(sc, axis=-1, keepdims=True)
            p = jnp.exp(sc - m)
            denom = jnp.sum(p, axis=-1, keepdims=True)
            o = _dot(p.astype(BF16), vp) * (1.0 / denom)
            lse = m + jnp.log(denom)
            o_pair = jnp.where(low_head, o[:B_Q_BLOCK], o[B_Q_BLOCK:])
            lse_pair = jnp.where(low_head, lse[:B_Q_BLOCK], lse[B_Q_BLOCK:])
            if dil == 1:
                rows = pl.ds(r0, B_Q_BLOCK)
            else:
                rows = pl.ds(r0 * dil + res, B_Q_BLOCK, stride=dil)
            o_ref[0, pr, rows, :] = o_pair
            lse_ref[0, pr, rows, :] = lse_pair
        return carry

    lax.fori_loop(0, dil * blocks, unit, 0, unroll=B_UNROLL)


def _attn_b(bias, q, k, v, dil):
    b, _, sub_len, _ = q.shape
    s = sub_len * dil
    sub_rows = B_SPAN // dil
    nslab = B_GROUP_COLS // LANES
    out_sds = jax.ShapeDtypeStruct((b, nslab, s, LANES), F32)
    out_spec = pl.BlockSpec((1, nslab, B_SPAN, LANES), lambda bi, i: (bi, 0, i, 0))
    return pl.pallas_call(
        functools.partial(_attn_b_kernel, dil=dil),
        grid=(b, s // B_SPAN),
        in_specs=[
            _const_spec(bias.shape),
            pl.BlockSpec((1, dil, sub_rows, B_GROUP_COLS), lambda bi, i: (bi, 0, i, 0)),
            pl.BlockSpec((1, dil, sub_len, B_GROUP_COLS), lambda bi, i: (bi, 0, 0, 0)),
            pl.BlockSpec((1, dil, sub_len, B_GROUP_COLS), lambda bi, i: (bi, 0, 0, 0)),
        ],
        out_specs=[out_spec, out_spec],
        out_shape=[out_sds, out_sds],
        compiler_params=pltpu.CompilerParams(
            dimension_semantics=("parallel", "parallel"), vmem_limit_bytes=VMEM_LIMIT_BYTES),
        name=f"attn_b_d{dil}",
    )(bias, q, k, v)


def _mix_out_kernel(x_ref, ya_ref, o0_ref, l0_ref, o1_ref, l1_ref, o2_ref, l2_ref,
                    g_pre_ref, w_gate_ref, b_gate_ref, w_a_ref, w_b_ref, w_out_ref, g_post_ref,
                    out_ref):
    x = x_ref[0]
    h = _rms_norm(x, g_pre_ref[...]).astype(BF16)
    yb = []
    for c in range(B_GROUP_COLS // LANES):
        l0, l1, l2 = l0_ref[0, c], l1_ref[0, c], l2_ref[0, c]
        m = jnp.maximum(jnp.maximum(l0, l1), l2)
        e0, e1, e2 = jnp.exp(l0 - m), jnp.exp(l1 - m), jnp.exp(l2 - m)
        num = e0 * o0_ref[0, c] + e1 * o1_ref[0, c] + e2 * o2_ref[0, c]
        yb.append((num / (e0 + e1 + e2)).astype(BF16))
    yb = jnp.concatenate(yb, axis=-1)
    gates = jax.nn.sigmoid(_dot(h, w_gate_ref[...]) + b_gate_ref[...])
    merged = (gates[:, :D_MODEL] * _dot(ya_ref[0], w_a_ref[...])
              + gates[:, D_MODEL:] * _dot(yb, w_b_ref[...]))
    mix = _dot(merged.astype(BF16), w_out_ref[...])
    out_ref[0] = x + _rms_norm(mix, g_post_ref[...])


def _mix_out(x, ya, ob, g_pre, w_gate, b_gate, w_a, w_b, w_out, g_post):
    b, s, _ = x.shape
    t = ROW_TILE
    nslab = B_GROUP_COLS // LANES
    row_map = lambda bi, i: (bi, i, 0)
    slab_spec = pl.BlockSpec((1, nslab, t, LANES), lambda bi, i: (bi, 0, i, 0))
    flat_ob = [a for pair in ob for a in pair]
    return pl.pallas_call(
        _mix_out_kernel,
        grid=(b, s // t),
        in_specs=[
            pl.BlockSpec((1, t, D_MODEL), row_map),
            pl.BlockSpec((1, t, A_Q_COLS), row_map),
        ] + [slab_spec] * len(flat_ob) + [
            _const_spec((1, D_MODEL)),
            _const_spec((D_MODEL, 2 * D_MODEL)),
            _const_spec((1, 2 * D_MODEL)),
            _const_spec((A_Q_COLS, D_MODEL)),
            _const_spec((B_GROUP_COLS, D_MODEL)),
            _const_spec((D_MODEL, D_MODEL)),
            _const_spec((1, D_MODEL)),
        ],
        out_specs=pl.BlockSpec((1, t, D_MODEL), row_map),
        out_shape=jax.ShapeDtypeStruct((b, s, D_MODEL), F32),
        compiler_params=pltpu.CompilerParams(
            dimension_semantics=("parallel", "parallel"), vmem_limit_bytes=VMEM_LIMIT_BYTES),
        name="mix_out",
    )(x, ya, *flat_ob, g_pre, w_gate, b_gate, w_a, w_b, w_out, g_post)


def _gelu_tanh(x):
    c = -2.0 * math.sqrt(2.0 / math.pi)
    return x / (1.0 + jnp.exp(x * (c + (c * 0.044715) * (x * x))))


def _conv_ffn_kernel(x_ref, prev_ref, next_ref, g_pre_ref, w_up_ref, conv_w_ref, conv_b_ref,
                     w_down_ref, g_post_ref, out_ref, acc_ref):
    t = x_ref.shape[1]
    i = pl.program_id(1)
    last = pl.num_programs(1) - 1
    x = x_ref[0]
    xe = jnp.concatenate([prev_ref[0], x, next_ref[0]], axis=0)
    rows = t + 2 * HALO
    row = lax.broadcasted_iota(jnp.int32, (rows, 1), 0)
    valid = jnp.logical_and(jnp.logical_or(row >= HALO, i > 0),
                            jnp.logical_or(row < HALO + t, i < last))
    h = jnp.where(valid, _rms_norm(xe, g_pre_ref[...]), 0.0).astype(BF16)
    for c in range(N_FF_CHUNKS):
        up = _dot(h, w_up_ref[c])
        cw = conv_w_ref[c]
        before = pltpu.roll(up, 1, 0)[HALO:HALO + t]
        after = pltpu.roll(up, rows - 1, 0)[HALO:HALO + t]
        u = before * cw[0:1] + up[HALO:HALO + t] * cw[1:2] + after * cw[2:3] + conv_b_ref[c]
        act = (_gelu_tanh(u[:, :FF_CHUNK]) * u[:, FF_CHUNK:]).astype(BF16)
        part = _dot(act, w_down_ref[c])
        if c == 0:
            acc_ref[...] = part
        else:
            acc_ref[...] += part
    out_ref[0] = x + _rms_norm(acc_ref[...], g_post_ref[...])


def _conv_ffn(x, g_pre, w_up, conv_w, conv_b, w_down, g_post):
    b, s, _ = x.shape
    t = ROW_TILE
    tiles = s // t
    halo_blocks_per_tile = t // HALO
    n_halo_blocks = s // HALO
    row_map = lambda bi, i: (bi, i, 0)
    prev_map = lambda bi, i: (bi, jnp.maximum(i * halo_blocks_per_tile - 1, 0), 0)
    next_map = lambda bi, i: (bi, jnp.minimum((i + 1) * halo_blocks_per_tile, n_halo_blocks - 1), 0)
    return pl.pallas_call(
        _conv_ffn_kernel,
        grid=(b, tiles),
        in_specs=[
            pl.BlockSpec((1, t, D_MODEL), row_map),
            pl.BlockSpec((1, HALO, D_MODEL), prev_map),
            pl.BlockSpec((1, HALO, D_MODEL), next_map),
            _const_spec((1, D_MODEL)),
            _const_spec((N_FF_CHUNKS, D_MODEL, 2 * FF_CHUNK)),
            _const_spec((N_FF_CHUNKS, 3, 2 * FF_CHUNK)),
            _const_spec((N_FF_CHUNKS, 1, 2 * FF_CHUNK)),
            _const_spec((N_FF_CHUNKS, FF_CHUNK, D_MODEL)),
            _const_spec((1, D_MODEL)),
        ],
        out_specs=pl.BlockSpec((1, t, D_MODEL), row_map),
        out_shape=jax.ShapeDtypeStruct((b, s, D_MODEL), F32),
        scratch_shapes=[pltpu.VMEM((t, D_MODEL), F32)],
        compiler_params=pltpu.CompilerParams(
            dimension_semantics=("parallel", "parallel"), vmem_limit_bytes=VMEM_LIMIT_BYTES),
        name="conv_ffn",
    )(x, x, x, g_pre, w_up, conv_w, conv_b, w_down, g_post)


def _rope_tables(seq_len):
    half = HEAD_DIM // 2
    inv = ROPE_THETA ** (-jnp.arange(half, dtype=F32) / half)
    ang = jnp.arange(seq_len, dtype=F32)[:, None] * inv[None, :]
    cos, sin = jnp.cos(ang), jnp.sin(ang)
    cos_head = jnp.concatenate([cos, cos], axis=-1)
    sin_head = jnp.concatenate([-sin, sin], axis=-1)
    reps = LANES // HEAD_DIM
    return jnp.tile(cos_head, (1, reps)), jnp.tile(sin_head, (1, reps))


def _band_bias(q_rows, stacked, keys, half_window, offset_step):
    off = jnp.arange(3, dtype=jnp.int32)[:, None, None] * offset_step
    i = (jnp.arange(stacked * q_rows, dtype=jnp.int32) % q_rows)[None, :, None]
    j = jnp.arange(keys, dtype=jnp.int32)[None, None, :]
    return jnp.where(jnp.abs(j - off - i) <= half_window, 0.0, NEG_INF).astype(F32)


def _chunk_gate_up(a):
    lead = a.shape[:-1]
    a = a.reshape(lead + (2, N_FF_CHUNKS, FF_CHUNK))
    a = jnp.moveaxis(a, -2, 0)
    return a.reshape((N_FF_CHUNKS,) + lead + (2 * FF_CHUNK,))


def kernel(x, norm_mix_pre, w_in, sink, w_branch_a, w_branch_b, w_gate, b_gate, w_out,
           norm_mix_post, norm_ffn_pre, w_up, conv_w, conv_b, w_down, norm_ffn_post):
    b, s, d = x.shape
    assert d == D_MODEL and s % B_SPAN == 0 and s % A_STEP_ROWS == 0
    cos, sin_signed = _rope_tables(s)
    bias_a = _band_bias(A_Q_BLOCK, 1, A_KEY_WINDOW, A_HALF_WINDOW, A_Q_BLOCK)
    bias_b = _band_bias(B_Q_BLOCK, 2, B_KEY_WINDOW, B_HALF_WINDOW, B_HALF_WINDOW)
    for layer in range(norm_mix_pre.shape[0]):
        proj = _in_proj(x, norm_mix_pre[layer][None], w_in[layer].astype(BF16), cos, sin_signed)
        qa, ka, va = proj[:3]
        ya = _attn_a(sink[layer], bias_a, qa, ka, va)
        ob = []
        for g, (_, dil) in enumerate(B_PATTERNS):
            qb, kb, vb = proj[3 + 3 * g:6 + 3 * g]
            ob.append(_attn_b(bias_b, qb, kb, vb, dil))
        x = _mix_out(x, ya, ob, norm_mix_pre[layer][None],
                     w_gate[layer].astype(BF16), b_gate[layer][None],
                     w_branch_a[layer].astype(BF16), w_branch_b[layer].astype(BF16),
                     w_out[layer].astype(BF16), norm_mix_post[layer][None])
        x = _conv_ffn(x, norm_ffn_pre[layer][None],
                      _chunk_gate_up(w_up[layer]).astype(BF16),
                      _chunk_gate_up(conv_w[layer]), _chunk_gate_up(conv_b[layer][None]),
                      w_down[layer].reshape(N_FF_CHUNKS, FF_CHUNK, D_MODEL).astype(BF16),
                      norm_ffn_post[layer][None])
    return x
```

```python
import functools
import math

import jax
import jax.numpy as jnp
from jax import lax
from jax.experimental import pallas as pl
from jax.experimental.pallas import tpu as pltpu

D_MODEL = 1024
HEAD_DIM = 64
A_Q_HEADS = 8
A_KV_HEADS = 2
A_GROUP = A_Q_HEADS // A_KV_HEADS
A_HALF_WINDOW = 128
B_PATTERNS = ((128, 1), (512, 4), (2048, 16))
B_N_GROUPS = len(B_PATTERNS)
B_HEADS = 4
B_HALF_WINDOW = 64
ROPE_THETA = 10000.0
D_FF = 3 * D_MODEL
RMS_EPS = 1e-6
NEG_INF = -1e30

A_Q_COLS = A_Q_HEADS * HEAD_DIM
A_KV_COLS = A_KV_HEADS * HEAD_DIM
A_COLS = A_Q_COLS + 2 * A_KV_COLS
B_GROUP_COLS = B_HEADS * HEAD_DIM
B_PROJ_COLS = B_N_GROUPS * B_GROUP_COLS
IN_COLS = A_COLS + 3 * B_PROJ_COLS

LANES = 128
VMEM_LIMIT_BYTES = 56 * 1024 * 1024

ROW_TILE = 512
A_Q_BLOCK = 128
A_KEY_WINDOW = 3 * A_Q_BLOCK
A_STEP_ROWS = 1024
B_Q_BLOCK = 128
B_KEY_WINDOW = B_Q_BLOCK + 2 * B_HALF_WINDOW
B_SPAN = 2048
A_UNROLL = 2
B_UNROLL = 4
FF_CHUNK = 512
N_FF_CHUNKS = D_FF // FF_CHUNK
HALO = 8

BF16 = jnp.bfloat16
F32 = jnp.float32


def _dot(a, b):
    return jnp.dot(a, b, preferred_element_type=F32)


def _dot_nt(a, b):
    return lax.dot_general(a, b, (((1,), (1,)), ((), ())), preferred_element_type=F32)


def _rms_norm(x, gain):
    ms = jnp.mean(x * x, axis=-1, keepdims=True)
    return x * lax.rsqrt(ms + RMS_EPS) * gain


def _rope_chunk(p, cos, sin_signed, first_half):
    partner = jnp.where(first_half, pltpu.roll(p, 96, 1), pltpu.roll(p, 32, 1))
    return p * cos + partner * sin_signed


def _const_spec(shape):
    nd = len(shape)
    return pl.BlockSpec(shape, lambda *_: (0,) * nd, pipeline_mode=pl.Buffered(1))


def _in_proj_kernel(x_ref, gain_ref, w_ref, cos_ref, sin_ref,
                    qa_ref, ka_ref, va_ref,
                    qb0_ref, kb0_ref, vb0_ref,
                    qb1_ref, kb1_ref, vb1_ref,
                    qb2_ref, kb2_ref, vb2_ref,
                    slab_ref):
    t = x_ref.shape[1]
    h = _rms_norm(x_ref[0], gain_ref[...]).astype(BF16)
    cos = cos_ref[...]
    sin_signed = sin_ref[...]
    lane = lax.broadcasted_iota(jnp.int32, (t, LANES), 1)
    first_half = (lane % HEAD_DIM) < (HEAD_DIM // 2)
    low_head = lane < HEAD_DIM

    def proj(col0, ncols):
        return _dot(h, w_ref[:, col0:col0 + ncols])

    def rope(p, scale=None):
        chunks = []
        for c in range(p.shape[1] // LANES):
            r = _rope_chunk(p[:, c * LANES:(c + 1) * LANES], cos, sin_signed, first_half)
            chunks.append(r if scale is None else r * scale)
        return chunks

    def dup_heads(p):
        swapped = pltpu.roll(p, HEAD_DIM, 1)
        return [jnp.where(low_head, p, swapped), jnp.where(low_head, swapped, p)]

    scale = HEAD_DIM ** -0.5

    qa = rope(proj(0, A_Q_COLS), scale)
    for c, chunk in enumerate(qa):
        qa_ref[0, :, c * LANES:(c + 1) * LANES] = chunk.astype(BF16)
    ka = rope(proj(A_Q_COLS, A_KV_COLS))[0]
    for c, chunk in enumerate(dup_heads(ka)):
        ka_ref[0, :, c * LANES:(c + 1) * LANES] = chunk.astype(BF16)
    va = proj(A_Q_COLS + A_KV_COLS, A_KV_COLS)
    for c, chunk in enumerate(dup_heads(va)):
        va_ref[0, :, c * LANES:(c + 1) * LANES] = chunk.astype(BF16)

    out_refs = ((qb0_ref, kb0_ref, vb0_ref), (qb1_ref, kb1_ref, vb1_ref), (qb2_ref, kb2_ref, vb2_ref))
    for kind in range(3):
        for g, (_, dil) in enumerate(B_PATTERNS):
            col0 = A_COLS + kind * B_PROJ_COLS + g * B_GROUP_COLS
            p = proj(col0, B_GROUP_COLS)
            if kind == 0:
                chunks = rope(p, scale)
            elif kind == 1:
                chunks = rope(p)
            else:
                chunks = [p[:, c * LANES:(c + 1) * LANES] for c in range(B_GROUP_COLS // LANES)]
            o_ref = out_refs[g][kind]
            if dil == 1:
                for c, chunk in enumerate(chunks):
                    o_ref[0, 0, :, c * LANES:(c + 1) * LANES] = chunk.astype(BF16)
            else:
                for c, chunk in enumerate(chunks):
                    slab_ref[c] = chunk
                for c in range(len(chunks)):
                    for res in range(dil):
                        rows = slab_ref[c, pl.ds(res, t // dil, stride=dil), :]
                        o_ref[0, res, :, c * LANES:(c + 1) * LANES] = rows.astype(BF16)


def _in_proj(x, gain, w_in, cos, sin_signed):
    b, s, _ = x.shape
    t = ROW_TILE
    grid = (b, s // t)
    row_map = lambda bi, i: (bi, i, 0)
    out_shape = [
        jax.ShapeDtypeStruct((b, s, A_Q_COLS), BF16),
        jax.ShapeDtypeStruct((b, s, 2 * A_KV_COLS), BF16),
        jax.ShapeDtypeStruct((b, s, 2 * A_KV_COLS), BF16),
    ]
    out_specs = [
        pl.BlockSpec((1, t, A_Q_COLS), row_map),
        pl.BlockSpec((1, t, 2 * A_KV_COLS), row_map),
        pl.BlockSpec((1, t, 2 * A_KV_COLS), row_map),
    ]
    for _, dil in B_PATTERNS:
        for _ in range(3):
            out_shape.append(jax.ShapeDtypeStruct((b, dil, s // dil, B_GROUP_COLS), BF16))
            out_specs.append(pl.BlockSpec((1, dil, t // dil, B_GROUP_COLS), lambda bi, i: (bi, 0, i, 0)))
    return pl.pallas_call(
        _in_proj_kernel,
        grid=grid,
        in_specs=[
            pl.BlockSpec((1, t, D_MODEL), row_map),
            _const_spec((1, D_MODEL)),
            _const_spec((D_MODEL, IN_COLS)),
            pl.BlockSpec((t, LANES), lambda bi, i: (i, 0)),
            pl.BlockSpec((t, LANES), lambda bi, i: (i, 0)),
        ],
        out_specs=out_specs,
        out_shape=out_shape,
        scratch_shapes=[pltpu.VMEM((B_GROUP_COLS // LANES, t, LANES), F32)],
        compiler_params=pltpu.CompilerParams(
            dimension_semantics=("parallel", "parallel"), vmem_limit_bytes=VMEM_LIMIT_BYTES),
        name="in_proj",
    )(x, gain, w_in, cos, sin_signed)


def _attn_a_kernel(sink_ref, bias_ref, q_ref, k_ref, v_ref, o_ref):
    s = k_ref.shape[1]
    step_rows = q_ref.shape[1]
    blocks_per_step = step_rows // A_Q_BLOCK
    step = pl.program_id(1)
    lane = lax.broadcasted_iota(jnp.int32, (A_Q_BLOCK, LANES), 1)
    low_head = lane < HEAD_DIM

    def block(n, carry):
        r0 = pl.multiple_of(n * A_Q_BLOCK, A_Q_BLOCK)
        q0 = step * step_rows + r0
        start = jnp.clip(q0 - A_Q_BLOCK, 0, s - A_KEY_WINDOW)
        start = pl.multiple_of(start, A_Q_BLOCK)
        bias = bias_ref[(q0 - start) // A_Q_BLOCK]
        k_win = k_ref[0, pl.ds(start, A_KEY_WINDOW), :]
        v_win = v_ref[0, pl.ds(start, A_KEY_WINDOW), :]
        q_blk = q_ref[0, pl.ds(r0, A_Q_BLOCK), :]
        for j in range(A_KV_HEADS):
            kd = k_win[:, j * LANES:(j + 1) * LANES]
            vd = v_win[:, j * LANES:(j + 1) * LANES]
            lhs = []
            for g in range(A_GROUP):
                c0 = j * A_GROUP * HEAD_DIM + (g // 2) * LANES
                pair = q_blk[:, c0:c0 + LANES]
                keep = low_head if g % 2 == 0 else jnp.logical_not(low_head)
                lhs.append(jnp.where(keep, pair, jnp.zeros_like(pair)))
            sc = _dot_nt(jnp.concatenate(lhs, axis=0), kd)
            probs, inv = [], []
            for g in range(A_GROUP):
                sg = sc[g * A_Q_BLOCK:(g + 1) * A_Q_BLOCK] + bias
                sink = sink_ref[j * A_GROUP + g]
                m = jnp.maximum(jnp.max(sg, axis=-1, keepdims=True), sink)
                p = jnp.exp(sg - m)
                denom = jnp.sum(p, axis=-1, keepdims=True) + jnp.exp(sink - m)
                probs.append(p.astype(BF16))
                inv.append(1.0 / denom)
            o = _dot(jnp.concatenate(probs, axis=0), vd)
            for pr in range(A_GROUP // 2):
                lo = o[(2 * pr) * A_Q_BLOCK:(2 * pr + 1) * A_Q_BLOCK] * inv[2 * pr]
                hi = o[(2 * pr + 1) * A_Q_BLOCK:(2 * pr + 2) * A_Q_BLOCK] * inv[2 * pr + 1]
                c0 = j * A_GROUP * HEAD_DIM + pr * LANES
                o_ref[0, pl.ds(r0, A_Q_BLOCK), c0:c0 + LANES] = jnp.where(low_head, lo, hi).astype(BF16)
        return carry

    lax.fori_loop(0, blocks_per_step, block, 0, unroll=A_UNROLL)


def _attn_a(sink, bias, qa, ka, va):
    b, s, _ = qa.shape
    step_rows = A_STEP_ROWS
    return pl.pallas_call(
        _attn_a_kernel,
        grid=(b, s // step_rows),
        in_specs=[
            pl.BlockSpec(memory_space=pltpu.SMEM),
            _const_spec(bias.shape),
            pl.BlockSpec((1, step_rows, A_Q_COLS), lambda bi, i: (bi, i, 0)),
            pl.BlockSpec((1, s, 2 * A_KV_COLS), lambda bi, i: (bi, 0, 0)),
            pl.BlockSpec((1, s, 2 * A_KV_COLS), lambda bi, i: (bi, 0, 0)),
        ],
        out_specs=pl.BlockSpec((1, step_rows, A_Q_COLS), lambda bi, i: (bi, i, 0)),
        out_shape=jax.ShapeDtypeStruct((b, s, A_Q_COLS), BF16),
        compiler_params=pltpu.CompilerParams(
            dimension_semantics=("parallel", "parallel"), vmem_limit_bytes=VMEM_LIMIT_BYTES),
        name="attn_a",
    )(sink, bias, qa, ka, va)


def _attn_b_kernel(bias_ref, q_ref, k_ref, v_ref, o_ref, lse_ref, *, dil):
    sub_len = k_ref.shape[2]
    sub_rows = q_ref.shape[2]
    blocks = sub_rows // B_Q_BLOCK
    step = pl.program_id(1)
    lane = lax.broadcasted_iota(jnp.int32, (B_Q_BLOCK, LANES), 1)
    low_head = lane < HEAD_DIM

    def unit(u, carry):
        res = u // blocks
        n = u % blocks
        r0 = pl.multiple_of(n * B_Q_BLOCK, B_Q_BLOCK)
        sub0 = step * sub_rows + r0
        start = jnp.clip(sub0 - B_HALF_WINDOW, 0, sub_len - B_KEY_WINDOW)
        start = pl.multiple_of(start, B_HALF_WINDOW)
        bias = bias_ref[(sub0 - start) // B_HALF_WINDOW]
        q_blk = q_ref[0, res, pl.ds(r0, B_Q_BLOCK), :]
        k_win = k_ref[0, res, pl.ds(start, B_KEY_WINDOW), :]
        v_win = v_ref[0, res, pl.ds(start, B_KEY_WINDOW), :]
        for pr in range(B_HEADS // 2):
            qp = q_blk[:, pr * LANES:(pr + 1) * LANES]
            kp = k_win[:, pr * LANES:(pr + 1) * LANES]
            vp = v_win[:, pr * LANES:(pr + 1) * LANES]
            zero = jnp.zeros_like(qp)
            lhs = jnp.concatenate([jnp.where(low_head, qp, zero), jnp.where(low_head, zero, qp)], axis=0)
            sc = _dot_nt(lhs, kp) + bias
            m = jnp.max(sc, axis=-1, keepdims=True)
            p = jnp.exp(sc - m)
            denom = jnp.sum(p, axis=-1, keepdims=True)
            o = _dot(p.astype(BF16), vp) * (1.0 / denom)
            lse = m + jnp.log(denom)
            o_pair = jnp.where(low_head, o[:B_Q_BLOCK], o[B_Q_BLOCK:])
            lse_pair = jnp.where(low_head, lse[:B_Q_BLOCK], lse[B_Q_BLOCK:])
            if dil == 1:
                rows = pl.ds(r0, B_Q_BLOCK)
            else:
                rows = pl.ds(r0 * dil + res, B_Q_BLOCK, stride=dil)
            o_ref[0, pr, rows, :] = o_pair
            lse_ref[0, pr, rows, :] = lse_pair
        return carry

    lax.fori_loop(0, dil * blocks, unit, 0, unroll=B_UNROLL)


def _attn_b(bias, q, k, v, dil):
    b, _, sub_len, _ = q.shape
    s = sub_len * dil
    sub_rows = B_SPAN // dil
    nslab = B_GROUP_COLS // LANES
    out_sds = jax.ShapeDtypeStruct((b, nslab, s, LANES), F32)
    out_spec = pl.BlockSpec((1, nslab, B_SPAN, LANES), lambda bi, i: (bi, 0, i, 0))
    return pl.pallas_call(
        functools.partial(_attn_b_kernel, dil=dil),
        grid=(b, s // B_SPAN),
        in_specs=[
            _const_spec(bias.shape),
            pl.BlockSpec((1, dil, sub_rows, B_GROUP_COLS), lambda bi, i: (bi, 0, i, 0)),
            pl.BlockSpec((1, dil, sub_len, B_GROUP_COLS), lambda bi, i: (bi, 0, 0, 0)),
            pl.BlockSpec((1, dil, sub_len, B_GROUP_COLS), lambda bi, i: (bi, 0, 0, 0)),
        ],
        out_specs=[out_spec, out_spec],
        out_shape=[out_sds, out_sds],
        compiler_params=pltpu.CompilerParams(
            dimension_semantics=("parallel", "parallel"), vmem_limit_bytes=VMEM_LIMIT_BYTES),
        name=f"attn_b_d{dil}",
    )(bias, q, k, v)


def _mix_out_kernel(x_ref, ya_ref, o0_ref, l0_ref, o1_ref, l1_ref, o2_ref, l2_ref,
                    g_pre_ref, w_gate_ref, b_gate_ref, w_a_ref, w_b_ref, w_out_ref, g_post_ref,
                    out_ref):
    x = x_ref[0]
    h = _rms_norm(x, g_pre_ref[...]).astype(BF16)
    yb = []
    for c in range(B_GROUP_COLS // LANES):
        l0, l1, l2 = l0_ref[0, c], l1_ref[0, c], l2_ref[0, c]
        m = jnp.maximum(jnp.maximum(l0, l1), l2)
        e0, e1, e2 = jnp.exp(l0 - m), jnp.exp(l1 - m), jnp.exp(l2 - m)
        num = e0 * o0_ref[0, c] + e1 * o1_ref[0, c] + e2 * o2_ref[0, c]
        yb.append((num / (e0 + e1 + e2)).astype(BF16))
    yb = jnp.concatenate(yb, axis=-1)
    gates = jax.nn.sigmoid(_dot(h, w_gate_ref[...]) + b_gate_ref[...])
    merged = (gates[:, :D_MODEL] * _dot(ya_ref[0], w_a_ref[...])
              + gates[:, D_MODEL:] * _dot(yb, w_b_ref[...]))
    mix = _dot(merged.astype(BF16), w_out_ref[...])
    out_ref[0] = x + _rms_norm(mix, g_post_ref[...])


def _mix_out(x, ya, ob, g_pre, w_gate, b_gate, w_a, w_b, w_out, g_post):
    b, s, _ = x.shape
    t = ROW_TILE
    nslab = B_GROUP_COLS // LANES
    row_map = lambda bi, i: (bi, i, 0)
    slab_spec = pl.BlockSpec((1, nslab, t, LANES), lambda bi, i: (bi, 0, i, 0))
    flat_ob = [a for pair in ob for a in pair]
    return pl.pallas_call(
        _mix_out_kernel,
        grid=(b, s // t),
        in_specs=[
            pl.BlockSpec((1, t, D_MODEL), row_map),
            pl.BlockSpec((1, t, A_Q_COLS), row_map),
        ] + [slab_spec] * len(flat_ob) + [
            _const_spec((1, D_MODEL)),
            _const_spec((D_MODEL, 2 * D_MODEL)),
            _const_spec((1, 2 * D_MODEL)),
            _const_spec((A_Q_COLS, D_MODEL)),
            _const_spec((B_GROUP_COLS, D_MODEL)),
            _const_spec((D_MODEL, D_MODEL)),
            _const_spec((1, D_MODEL)),
        ],
        out_specs=pl.BlockSpec((1, t, D_MODEL), row_map),
        out_shape=jax.ShapeDtypeStruct((b, s, D_MODEL), F32),
        compiler_params=pltpu.CompilerParams(
            dimension_semantics=("parallel", "parallel"), vmem_limit_bytes=VMEM_LIMIT_BYTES),
        name="mix_out",
    )(x, ya, *flat_ob, g_pre, w_gate, b_gate, w_a, w_b, w_out, g_post)


def _gelu_tanh(x):
    c = -2.0 * math.sqrt(2.0 / math.pi)
    return x / (1.0 + jnp.exp(x * (c + (c * 0.044715) * (x * x))))


def _conv_ffn_kernel(x_ref, prev_ref, next_ref, g_pre_ref, w_up_ref, conv_w_ref, conv_b_ref,
                     w_down_ref, g_post_ref, out_ref, u_ref, acc_ref):
    t = x_ref.shape[1]
    half = t // 2
    slabs = FF_CHUNK // LANES
    i = pl.program_id(1)
    last = pl.num_programs(1) - 1
    x = x_ref[0]
    xe = jnp.concatenate([prev_ref[0], x, next_ref[0]], axis=0)
    row = lax.broadcasted_iota(jnp.int32, (t + 2 * HALO, 1), 0)
    valid = jnp.logical_and(jnp.logical_or(row >= HALO, i > 0),
                            jnp.logical_or(row < HALO + t, i < last))
    h = jnp.where(valid, _rms_norm(xe, g_pre_ref[...]), 0.0).astype(BF16)

    def col_starts(c):
        return (c * FF_CHUNK, D_FF + c * FF_CHUNK)

    def project(c):
        for part, col0 in enumerate(col_starts(c)):
            up = _dot(h, w_up_ref[:, col0:col0 + FF_CHUNK])
            for j in range(slabs):
                u_ref[2 * (c % 2) + part, j] = up[:, j * LANES:(j + 1) * LANES]

    def conv(c, part):
        buf, col0 = 2 * (c % 2) + part, col_starts(c)[part]
        pieces = []
        for j in range(slabs):
            cols = slice(col0 + j * LANES, col0 + (j + 1) * LANES)
            w0, w1, w2 = conv_w_ref[0:1, cols], conv_w_ref[1:2, cols], conv_w_ref[2:3, cols]
            bias = conv_b_ref[:, cols]
            r = [u_ref[buf, j, pl.ds(HALO - 1 + k, half, stride=2), :] for k in range(4)]
            even = r[0] * w0 + r[1] * w1 + r[2] * w2 + bias
            odd = r[1] * w0 + r[2] * w1 + r[3] * w2 + bias
            pieces.append(jnp.concatenate([even, odd], axis=0))
        return jnp.concatenate(pieces, axis=-1)

    project(0)
    for c in range(N_FF_CHUNKS):
        if c + 1 < N_FF_CHUNKS:
            project(c + 1)
        act = (_gelu_tanh(conv(c, 0)) * conv(c, 1)).astype(BF16)
        part = _dot(act, w_down_ref[c * FF_CHUNK:(c + 1) * FF_CHUNK, :])
        if c == 0:
            acc_ref[...] = part
        else:
            acc_ref[...] += part
    y = _rms_norm(acc_ref[...], g_post_ref[...])
    for j in range(D_MODEL // LANES):
        buf, slab = divmod(j, slabs)
        u_ref[buf, slab, pl.ds(0, half, stride=2), :] = y[:half, j * LANES:(j + 1) * LANES]
        u_ref[buf, slab, pl.ds(1, half, stride=2), :] = y[half:, j * LANES:(j + 1) * LANES]
    y_nat = jnp.concatenate([u_ref[divmod(j, slabs)[0], divmod(j, slabs)[1], 0:t, :]
                             for j in range(D_MODEL // LANES)], axis=-1)
    out_ref[0] = x + y_nat


def _conv_ffn(x, g_pre, w_up, conv_w, conv_b, w_down, g_post):
    b, s, _ = x.shape
    t = ROW_TILE
    tiles = s // t
    halo_blocks_per_tile = t // HALO
    n_halo_blocks = s // HALO
    row_map = lambda bi, i: (bi, i, 0)
    prev_map = lambda bi, i: (bi, jnp.maximum(i * halo_blocks_per_tile - 1, 0), 0)
    next_map = lambda bi, i: (bi, jnp.minimum((i + 1) * halo_blocks_per_tile, n_halo_blocks - 1), 0)
    return pl.pallas_call(
        _conv_ffn_kernel,
        grid=(b, tiles),
        in_specs=[
            pl.BlockSpec((1, t, D_MODEL), row_map),
            pl.BlockSpec((1, HALO, D_MODEL), prev_map),
            pl.BlockSpec((1, HALO, D_MODEL), next_map),
            _const_spec((1, D_MODEL)),
            _const_spec((D_MODEL, 2 * D_FF)),
            _const_spec((3, 2 * D_FF)),
            _const_spec((1, 2 * D_FF)),
            _const_spec((D_FF, D_MODEL)),
            _const_spec((1, D_MODEL)),
        ],
        out_specs=pl.BlockSpec((1, t, D_MODEL), row_map),
        out_shape=jax.ShapeDtypeStruct((b, s, D_MODEL), F32),
        scratch_shapes=[
            pltpu.VMEM((4, FF_CHUNK // LANES, t + 2 * HALO, LANES), F32),
            pltpu.VMEM((t, D_MODEL), F32),
        ],
        compiler_params=pltpu.CompilerParams(
            dimension_semantics=("parallel", "parallel"), vmem_limit_bytes=VMEM_LIMIT_BYTES),
        name="conv_ffn",
    )(x, x, x, g_pre, w_up, conv_w, conv_b, w_down, g_post)


def _rope_tables(seq_len):
    half = HEAD_DIM // 2
    inv = ROPE_THETA ** (-jnp.arange(half, dtype=F32) / half)
    ang = jnp.arange(seq_len, dtype=F32)[:, None] * inv[None, :]
    cos, sin = jnp.cos(ang), jnp.sin(ang)
    cos_head = jnp.concatenate([cos, cos], axis=-1)
    sin_head = jnp.concatenate([-sin, sin], axis=-1)
    reps = LANES // HEAD_DIM
    return jnp.tile(cos_head, (1, reps)), jnp.tile(sin_head, (1, reps))


def _band_bias(q_rows, stacked, keys, half_window, offset_step):
    off = jnp.arange(3, dtype=jnp.int32)[:, None, None] * offset_step
    i = (jnp.arange(stacked * q_rows, dtype=jnp.int32) % q_rows)[None, :, None]
    j = jnp.arange(keys, dtype=jnp.int32)[None, None, :]
    return jnp.where(jnp.abs(j - off - i) <= half_window, 0.0, NEG_INF).astype(F32)


def kernel(x, norm_mix_pre, w_in, sink, w_branch_a, w_branch_b, w_gate, b_gate, w_out,
           norm_mix_post, norm_ffn_pre, w_up, conv_w, conv_b, w_down, norm_ffn_post):
    b, s, d = x.shape
    assert d == D_MODEL and s % B_SPAN == 0 and s % A_STEP_ROWS == 0
    cos, sin_signed = _rope_tables(s)
    bias_a = _band_bias(A_Q_BLOCK, 1, A_KEY_WINDOW, A_HALF_WINDOW, A_Q_BLOCK)
    bias_b = _band_bias(B_Q_BLOCK, 2, B_KEY_WINDOW, B_HALF_WINDOW, B_HALF_WINDOW)
    for layer in range(norm_mix_pre.shape[0]):
        proj = _in_proj(x, norm_mix_pre[layer][None], w_in[layer].astype(BF16), cos, sin_signed)
        qa, ka, va = proj[:3]
        ya = _attn_a(sink[layer], bias_a, qa, ka, va)
        ob = []
        for g, (_, dil) in enumerate(B_PATTERNS):
            qb, kb, vb = proj[3 + 3 * g:6 + 3 * g]
            ob.append(_attn_b(bias_b, qb, kb, vb, dil))
        x = _mix_out(x, ya, ob, norm_mix_pre[layer][None],
                     w_gate[layer].astype(BF16), b_gate[layer][None],
                     w_branch_a[layer].astype(BF16), w_branch_b[layer].astype(BF16),
                     w_out[layer].astype(BF16), norm_mix_post[layer][None])
        x = _conv_ffn(x, norm_ffn_pre[layer][None],
                      w_up[layer].astype(BF16), conv_w[layer], conv_b[layer][None],
                      w_down[layer].astype(BF16),
                      norm_ffn_post[layer][None])
    return x
```

```python
import functools
import math

import jax
import jax.numpy as jnp
from jax import lax
from jax.experimental import pallas as pl
from jax.experimental.pallas import tpu as pltpu

D_MODEL = 1024
HEAD_DIM = 64
A_Q_HEADS = 8
A_KV_HEADS = 2
A_GROUP = A_Q_HEADS // A_KV_HEADS
A_HALF_WINDOW = 128
B_PATTERNS = ((128, 1), (512, 4), (2048, 16))
B_N_GROUPS = len(B_PATTERNS)
B_HEADS = 4
HEAD_ORDER = (0, 2, 1, 3)
B_HALF_WINDOW = 64
ROPE_THETA = 10000.0
D_FF = 3 * D_MODEL
RMS_EPS = 1e-6
NEG_INF = -1e30
LOG2_E = math.log2(math.e)

A_Q_COLS = A_Q_HEADS * HEAD_DIM
A_KV_COLS = A_KV_HEADS * HEAD_DIM
A_COLS = A_Q_COLS + 2 * A_KV_COLS
B_GROUP_COLS = B_HEADS * HEAD_DIM
B_PROJ_COLS = B_N_GROUPS * B_GROUP_COLS
IN_COLS = A_COLS + 3 * B_PROJ_COLS

LANES = 128
VMEM_LIMIT_BYTES = 56 * 1024 * 1024

ROW_TILE = 512
A_Q_BLOCK = 128
A_KEY_WINDOW = 3 * A_Q_BLOCK
A_STEP_ROWS = 1024
B_Q_BLOCK = 128
B_KEY_WINDOW = B_Q_BLOCK + 2 * B_HALF_WINDOW
B_SPAN = 2048
A_UNROLL = 4
B_UNROLL = 8
FF_CHUNK = 512
N_FF_CHUNKS = D_FF // FF_CHUNK
HALO = 8

BF16 = jnp.bfloat16
F32 = jnp.float32


def _dot(a, b):
    return jnp.dot(a, b, preferred_element_type=F32)


def _dot_nt(a, b):
    return lax.dot_general(a, b, (((1,), (1,)), ((), ())), preferred_element_type=F32)


def _rms_norm(x, gain):
    ms = jnp.mean(x * x, axis=-1, keepdims=True)
    return x * lax.rsqrt(ms + RMS_EPS) * gain


def _rope_chunk(p, cos, sin_signed, first_half):
    partner = jnp.where(first_half, pltpu.roll(p, 96, 1), pltpu.roll(p, 32, 1))
    return p * cos + partner * sin_signed


def _const_spec(shape):
    nd = len(shape)
    return pl.BlockSpec(shape, lambda *_: (0,) * nd, pipeline_mode=pl.Buffered(1))


def _in_proj_kernel(x_ref, gain_ref, w_ref, cos_ref, sin_ref,
                    qa_ref, ka_ref, va_ref,
                    qb0_ref, kb0_ref, vb0_ref,
                    qb1_ref, kb1_ref, vb1_ref,
                    qb2_ref, kb2_ref, vb2_ref,
                    slab_ref):
    t = x_ref.shape[1]
    h = _rms_norm(x_ref[0], gain_ref[...]).astype(BF16)
    cos = cos_ref[...]
    sin_signed = sin_ref[...]
    lane = lax.broadcasted_iota(jnp.int32, (t, LANES), 1)
    first_half = (lane % HEAD_DIM) < (HEAD_DIM // 2)
    low_head = lane < HEAD_DIM

    def proj(col0, ncols):
        return _dot(h, w_ref[:, col0:col0 + ncols])

    def rope(p, scale=None):
        chunks = []
        for c in range(p.shape[1] // LANES):
            r = _rope_chunk(p[:, c * LANES:(c + 1) * LANES], cos, sin_signed, first_half)
            chunks.append(r if scale is None else r * scale)
        return chunks

    def dup_heads(p):
        swapped = pltpu.roll(p, HEAD_DIM, 1)
        return [jnp.where(low_head, p, swapped), jnp.where(low_head, swapped, p)]

    scale = HEAD_DIM ** -0.5 * LOG2_E

    qa = rope(proj(0, A_Q_COLS), scale)
    for c, chunk in enumerate(qa):
        qa_ref[0, :, c * LANES:(c + 1) * LANES] = chunk.astype(BF16)
    ka = rope(proj(A_Q_COLS, A_KV_COLS))[0]
    for c, chunk in enumerate(dup_heads(ka)):
        ka_ref[0, :, c * LANES:(c + 1) * LANES] = chunk.astype(BF16)
    va = proj(A_Q_COLS + A_KV_COLS, A_KV_COLS)
    for c, chunk in enumerate(dup_heads(va)):
        va_ref[0, :, c * LANES:(c + 1) * LANES] = chunk.astype(BF16)

    out_refs = ((qb0_ref, kb0_ref, vb0_ref), (qb1_ref, kb1_ref, vb1_ref), (qb2_ref, kb2_ref, vb2_ref))
    for kind in range(3):
        for g, (_, dil) in enumerate(B_PATTERNS):
            col0 = A_COLS + kind * B_PROJ_COLS + g * B_GROUP_COLS
            p = proj(col0, B_GROUP_COLS)
            if kind == 0:
                chunks = rope(p, scale)
            elif kind == 1:
                chunks = rope(p)
            else:
                chunks = [p[:, c * LANES:(c + 1) * LANES] for c in range(B_GROUP_COLS // LANES)]
            o_ref = out_refs[g][kind]
            if dil == 1:
                for c, chunk in enumerate(chunks):
                    o_ref[0, 0, :, c * LANES:(c + 1) * LANES] = chunk.astype(BF16)
            else:
                for c, chunk in enumerate(chunks):
                    slab_ref[c] = chunk
                for c in range(len(chunks)):
                    for res in range(dil):
                        rows = slab_ref[c, pl.ds(res, t // dil, stride=dil), :]
                        o_ref[0, res, :, c * LANES:(c + 1) * LANES] = rows.astype(BF16)


def _in_proj(x, gain, w_in, cos, sin_signed):
    b, s, _ = x.shape
    t = ROW_TILE
    grid = (b, s // t)
    row_map = lambda bi, i: (bi, i, 0)
    out_shape = [
        jax.ShapeDtypeStruct((b, s, A_Q_COLS), BF16),
        jax.ShapeDtypeStruct((b, s, 2 * A_KV_COLS), BF16),
        jax.ShapeDtypeStruct((b, s, 2 * A_KV_COLS), BF16),
    ]
    out_specs = [
        pl.BlockSpec((1, t, A_Q_COLS), row_map),
        pl.BlockSpec((1, t, 2 * A_KV_COLS), row_map),
        pl.BlockSpec((1, t, 2 * A_KV_COLS), row_map),
    ]
    for _, dil in B_PATTERNS:
        for _ in range(3):
            out_shape.append(jax.ShapeDtypeStruct((b, dil, s // dil, B_GROUP_COLS), BF16))
            out_specs.append(pl.BlockSpec((1, dil, t // dil, B_GROUP_COLS), lambda bi, i: (bi, 0, i, 0)))
    return pl.pallas_call(
        _in_proj_kernel,
        grid=grid,
        in_specs=[
            pl.BlockSpec((1, t, D_MODEL), row_map),
            _const_spec((1, D_MODEL)),
            _const_spec((D_MODEL, IN_COLS)),
            pl.BlockSpec((t, LANES), lambda bi, i: (i, 0)),
            pl.BlockSpec((t, LANES), lambda bi, i: (i, 0)),
        ],
        out_specs=out_specs,
        out_shape=out_shape,
        scratch_shapes=[pltpu.VMEM((B_GROUP_COLS // LANES, t, LANES), F32)],
        compiler_params=pltpu.CompilerParams(
            dimension_semantics=("parallel", "parallel"), vmem_limit_bytes=VMEM_LIMIT_BYTES),
        name="in_proj",
    )(x, gain, w_in, cos, sin_signed)


def _attn_a_kernel(sink_ref, bias_ref, q_ref, k_ref, v_ref, o_ref):
    s = k_ref.shape[1]
    step_rows = q_ref.shape[1]
    blocks_per_step = step_rows // A_Q_BLOCK
    step = pl.program_id(1)
    low_head = lax.broadcasted_iota(jnp.int32, (A_Q_BLOCK, LANES), 1) < HEAD_DIM
    low_key = lax.broadcasted_iota(jnp.int32, (A_KEY_WINDOW, LANES), 1) < HEAD_DIM

    def blocks(it, carry):
        work = []
        for nn in range(A_UNROLL):
            r0 = pl.multiple_of((it * A_UNROLL + nn) * A_Q_BLOCK, A_Q_BLOCK)
            q0 = step * step_rows + r0
            start = jnp.clip(q0 - A_Q_BLOCK, 0, s - A_KEY_WINDOW)
            start = pl.multiple_of(start, A_Q_BLOCK)
            bias = bias_ref[(q0 - start) // A_Q_BLOCK]
            k_win = k_ref[0, pl.ds(start, A_KEY_WINDOW), :]
            v_win = v_ref[0, pl.ds(start, A_KEY_WINDOW), :]
            q_blk = q_ref[0, pl.ds(r0, A_Q_BLOCK), :]
            for j in range(A_KV_HEADS):
                kd = k_win[:, j * LANES:(j + 1) * LANES]
                lhs = []
                for g in HEAD_ORDER:
                    c0 = j * A_GROUP * HEAD_DIM + (g // 2) * LANES
                    pair = q_blk[:, c0:c0 + LANES]
                    keep = low_head if g % 2 == 0 else jnp.logical_not(low_head)
                    lhs.append(jnp.where(keep, pair, jnp.zeros_like(pair)))
                sc = _dot_nt(jnp.concatenate(lhs, axis=0), kd)
                work.append((r0, j, bias, v_win[:, j * LANES:(j + 1) * LANES], sc))

        soft = []
        for r0, j, bias, vd, sc in work:
            probs, sink_terms = {}, {}
            for slot, g in enumerate(HEAD_ORDER):
                sg = sc[slot * A_Q_BLOCK:(slot + 1) * A_Q_BLOCK] + bias
                sink = sink_ref[j * A_GROUP + g] * LOG2_E
                m = jnp.maximum(jnp.max(sg, axis=-1, keepdims=True), sink)
                probs[g] = jnp.exp2(sg - m).astype(BF16)
                sink_terms[g] = jnp.exp2(sink - m)
            soft.append((r0, j, vd, probs, sink_terms))

        pairs = range(A_GROUP // 2)
        for r0, j, vd, probs, sink_terms in soft:
            one = jnp.ones_like(vd)
            r_even = _dot(jnp.concatenate([probs[2 * pr] for pr in pairs], axis=0),
                          jnp.where(low_key, vd, one))
            r_odd = _dot(jnp.concatenate([probs[2 * pr + 1] for pr in pairs], axis=0),
                         jnp.where(low_key, one, vd))
            for pr in pairs:
                even = r_even[pr * A_Q_BLOCK:(pr + 1) * A_Q_BLOCK]
                odd = r_odd[pr * A_Q_BLOCK:(pr + 1) * A_Q_BLOCK]
                num = jnp.where(low_head, even, odd)
                den = jnp.where(low_head, odd + sink_terms[2 * pr + 1], even + sink_terms[2 * pr])
                out = num / pltpu.roll(den, HEAD_DIM, 1)
                c0 = j * A_GROUP * HEAD_DIM + pr * LANES
                o_ref[0, pl.ds(r0, A_Q_BLOCK), c0:c0 + LANES] = out.astype(BF16)
        return carry

    lax.fori_loop(0, blocks_per_step // A_UNROLL, blocks, 0)


def _attn_a(sink, bias, qa, ka, va):
    b, s, _ = qa.shape
    step_rows = A_STEP_ROWS
    return pl.pallas_call(
        _attn_a_kernel,
        grid=(b, s // step_rows),
        in_specs=[
            pl.BlockSpec(memory_space=pltpu.SMEM),
            _const_spec(bias.shape),
            pl.BlockSpec((1, step_rows, A_Q_COLS), lambda bi, i: (bi, i, 0)),
            pl.BlockSpec((1, s, 2 * A_KV_COLS), lambda bi, i: (bi, 0, 0)),
            pl.BlockSpec((1, s, 2 * A_KV_COLS), lambda bi, i: (bi, 0, 0)),
        ],
        out_specs=pl.BlockSpec((1, step_rows, A_Q_COLS), lambda bi, i: (bi, i, 0)),
        out_shape=jax.ShapeDtypeStruct((b, s, A_Q_COLS), BF16),
        compiler_params=pltpu.CompilerParams(
            dimension_semantics=("parallel", "parallel"), vmem_limit_bytes=VMEM_LIMIT_BYTES),
        name="attn_a",
    )(sink, bias, qa, ka, va)


def _attn_b_kernel(bias_ref, q_ref, k_ref, v_ref, o_ref, lse_ref, *, dil):
    sub_len = k_ref.shape[2]
    sub_rows = q_ref.shape[2]
    blocks = sub_rows // B_Q_BLOCK
    step = pl.program_id(1)
    low_head = lax.broadcasted_iota(jnp.int32, (B_Q_BLOCK, LANES), 1) < HEAD_DIM
    low_key = lax.broadcasted_iota(jnp.int32, (B_KEY_WINDOW, LANES), 1) < HEAD_DIM

    def units(it, carry):
        work = []
        for uu in range(B_UNROLL):
            u = it * B_UNROLL + uu
            res = u // blocks
            r0 = pl.multiple_of((u % blocks) * B_Q_BLOCK, B_Q_BLOCK)
            sub0 = step * sub_rows + r0
            start = jnp.clip(sub0 - B_HALF_WINDOW, 0, sub_len - B_KEY_WINDOW)
            start = pl.multiple_of(start, B_HALF_WINDOW)
            bias = bias_ref[(sub0 - start) // B_HALF_WINDOW]
            q_blk = q_ref[0, res, pl.ds(r0, B_Q_BLOCK), :]
            k_win = k_ref[0, res, pl.ds(start, B_KEY_WINDOW), :]
            v_win = v_ref[0, res, pl.ds(start, B_KEY_WINDOW), :]
            if dil == 1:
                rows = pl.ds(r0, B_Q_BLOCK)
            else:
                rows = pl.ds(r0 * dil + res, B_Q_BLOCK, stride=dil)
            for pr in range(B_HEADS // 2):
                qp = q_blk[:, pr * LANES:(pr + 1) * LANES]
                zero = jnp.zeros_like(qp)
                lhs = jnp.concatenate([jnp.where(low_head, qp, zero), jnp.where(low_head, zero, qp)], axis=0)
                sc = _dot_nt(lhs, k_win[:, pr * LANES:(pr + 1) * LANES])
                work.append((rows, pr, bias, v_win[:, pr * LANES:(pr + 1) * LANES], sc))

        soft = []
        for rows, pr, bias, vp, sc in work:
            sc = sc + bias
            m = jnp.max(sc, axis=-1, keepdims=True)
            soft.append((rows, pr, vp, m, jnp.exp2(sc - m).astype(BF16)))

        for rows, pr, vp, m, p in soft:
            one = jnp.ones_like(vp)
            top = _dot(p[:B_Q_BLOCK], jnp.where(low_key, vp, one))
            bot = _dot(p[B_Q_BLOCK:], jnp.where(low_key, one, vp))
            denom = pltpu.roll(jnp.where(low_head, bot, top), HEAD_DIM, 1)
            o_ref[0, pr, rows, :] = jnp.where(low_head, top, bot) / denom
            lse_ref[0, pr, rows, :] = jnp.where(low_head, m[:B_Q_BLOCK], m[B_Q_BLOCK:]) + jnp.log2(denom)
        return carry

    lax.fori_loop(0, dil * blocks // B_UNROLL, units, 0)


def _attn_b(bias, q, k, v, dil):
    b, _, sub_len, _ = q.shape
    s = sub_len * dil
    sub_rows = B_SPAN // dil
    nslab = B_GROUP_COLS // LANES
    out_sds = jax.ShapeDtypeStruct((b, nslab, s, LANES), F32)
    out_spec = pl.BlockSpec((1, nslab, B_SPAN, LANES), lambda bi, i: (bi, 0, i, 0))
    return pl.pallas_call(
        functools.partial(_attn_b_kernel, dil=dil),
        grid=(b, s // B_SPAN),
        in_specs=[
            _const_spec(bias.shape),
            pl.BlockSpec((1, dil, sub_rows, B_GROUP_COLS), lambda bi, i: (bi, 0, i, 0)),
            pl.BlockSpec((1, dil, sub_len, B_GROUP_COLS), lambda bi, i: (bi, 0, 0, 0)),
            pl.BlockSpec((1, dil, sub_len, B_GROUP_COLS), lambda bi, i: (bi, 0, 0, 0)),
        ],
        out_specs=[out_spec, out_spec],
        out_shape=[out_sds, out_sds],
        compiler_params=pltpu.CompilerParams(
            dimension_semantics=("parallel", "parallel"), vmem_limit_bytes=VMEM_LIMIT_BYTES),
        name=f"attn_b_d{dil}",
    )(bias, q, k, v)


def _mix_out_kernel(x_ref, ya_ref, o0_ref, l0_ref, o1_ref, l1_ref, o2_ref, l2_ref,
                    g_pre_ref, w_gate_ref, b_gate_ref, w_a_ref, w_b_ref, w_out_ref, g_post_ref,
                    out_ref):
    x = x_ref[0]
    h = _rms_norm(x, g_pre_ref[...]).astype(BF16)
    yb = []
    for c in range(B_GROUP_COLS // LANES):
        l0, l1, l2 = l0_ref[0, c], l1_ref[0, c], l2_ref[0, c]
        m = jnp.maximum(jnp.maximum(l0, l1), l2)
        e0, e1, e2 = jnp.exp2(l0 - m), jnp.exp2(l1 - m), jnp.exp2(l2 - m)
        num = e0 * o0_ref[0, c] + e1 * o1_ref[0, c] + e2 * o2_ref[0, c]
        yb.append((num / (e0 + e1 + e2)).astype(BF16))
    yb = jnp.concatenate(yb, axis=-1)
    gates = jax.nn.sigmoid(_dot(h, w_gate_ref[...]) + b_gate_ref[...])
    merged = (gates[:, :D_MODEL] * _dot(ya_ref[0], w_a_ref[...])
              + gates[:, D_MODEL:] * _dot(yb, w_b_ref[...]))
    mix = _dot(merged.astype(BF16), w_out_ref[...])
    out_ref[0] = x + _rms_norm(mix, g_post_ref[...])


def _mix_out(x, ya, ob, g_pre, w_gate, b_gate, w_a, w_b, w_out, g_post):
    b, s, _ = x.shape
    t = ROW_TILE
    nslab = B_GROUP_COLS // LANES
    row_map = lambda bi, i: (bi, i, 0)
    slab_spec = pl.BlockSpec((1, nslab, t, LANES), lambda bi, i: (bi, 0, i, 0))
    flat_ob = [a for pair in ob for a in pair]
    return pl.pallas_call(
        _mix_out_kernel,
        grid=(b, s // t),
        in_specs=[
            pl.BlockSpec((1, t, D_MODEL), row_map),
            pl.BlockSpec((1, t, A_Q_COLS), row_map),
        ] + [slab_spec] * len(flat_ob) + [
            _const_spec((1, D_MODEL)),
            _const_spec((D_MODEL, 2 * D_MODEL)),
            _const_spec((1, 2 * D_MODEL)),
            _const_spec((A_Q_COLS, D_MODEL)),
            _const_spec((B_GROUP_COLS, D_MODEL)),
            _const_spec((D_MODEL, D_MODEL)),
            _const_spec((1, D_MODEL)),
        ],
        out_specs=pl.BlockSpec((1, t, D_MODEL), row_map),
        out_shape=jax.ShapeDtypeStruct((b, s, D_MODEL), F32),
        compiler_params=pltpu.CompilerParams(
            dimension_semantics=("parallel", "parallel"), vmem_limit_bytes=VMEM_LIMIT_BYTES),
        name="mix_out",
    )(x, ya, *flat_ob, g_pre, w_gate, b_gate, w_a, w_b, w_out, g_post)


def _gelu_tanh(x):
    c = -2.0 * math.sqrt(2.0 / math.pi)
    return x / (1.0 + jnp.exp(x * (c + (c * 0.044715) * (x * x))))


def _conv_ffn_kernel(x_ref, prev_ref, next_ref, g_pre_ref, w_up_ref, conv_w_ref, conv_b_ref,
                     w_down_ref, g_post_ref, out_ref, u_ref, acc_ref):
    t = x_ref.shape[1]
    half = t // 2
    slabs = FF_CHUNK // LANES
    i = pl.program_id(1)
    last = pl.num_programs(1) - 1
    x = x_ref[0]
    xe = jnp.concatenate([prev_ref[0], x, next_ref[0]], axis=0)
    row = lax.broadcasted_iota(jnp.int32, (t + 2 * HALO, 1), 0)
    valid = jnp.logical_and(jnp.logical_or(row >= HALO, i > 0),
                            jnp.logical_or(row < HALO + t, i < last))
    h = jnp.where(valid, _rms_norm(xe, g_pre_ref[...]), 0.0).astype(BF16)

    def col_starts(c):
        return (c * FF_CHUNK, D_FF + c * FF_CHUNK)

    def project(c):
        for part, col0 in enumerate(col_starts(c)):
            up = _dot(h, w_up_ref[:, col0:col0 + FF_CHUNK])
            for j in range(slabs):
                u_ref[2 * (c % 2) + part, j] = up[:, j * LANES:(j + 1) * LANES]

    def conv(c, part):
        buf, col0 = 2 * (c % 2) + part, col_starts(c)[part]
        pieces = []
        for j in range(slabs):
            cols = slice(col0 + j * LANES, col0 + (j + 1) * LANES)
            w0, w1, w2 = conv_w_ref[0:1, cols], conv_w_ref[1:2, cols], conv_w_ref[2:3, cols]
            bias = conv_b_ref[:, cols]
            r = [u_ref[buf, j, pl.ds(HALO - 1 + k, half, stride=2), :] for k in range(4)]
            even = r[0] * w0 + r[1] * w1 + r[2] * w2 + bias
            odd = r[1] * w0 + r[2] * w1 + r[3] * w2 + bias
            pieces.append(jnp.concatenate([even, odd], axis=0))
        return jnp.concatenate(pieces, axis=-1)

    project(0)
    for c in range(N_FF_CHUNKS):
        if c + 1 < N_FF_CHUNKS:
            project(c + 1)
        act = (_gelu_tanh(conv(c, 0)) * conv(c, 1)).astype(BF16)
        part = _dot(act, w_down_ref[c * FF_CHUNK:(c + 1) * FF_CHUNK, :])
        if c == 0:
            acc_ref[...] = part
        else:
            acc_ref[...] += part
    y = _rms_norm(acc_ref[...], g_post_ref[...])
    for j in range(D_MODEL // LANES):
        buf, slab = divmod(j, slabs)
        u_ref[buf, slab, pl.ds(0, half, stride=2), :] = y[:half, j * LANES:(j + 1) * LANES]
        u_ref[buf, slab, pl.ds(1, half, stride=2), :] = y[half:, j * LANES:(j + 1) * LANES]
    y_nat = jnp.concatenate([u_ref[divmod(j, slabs)[0], divmod(j, slabs)[1], 0:t, :]
                             for j in range(D_MODEL // LANES)], axis=-1)
    out_ref[0] = x + y_nat


def _conv_ffn(x, g_pre, w_up, conv_w, conv_b, w_down, g_post):
    b, s, _ = x.shape
    t = ROW_TILE
    tiles = s // t
    halo_blocks_per_tile = t // HALO
    n_halo_blocks = s // HALO
    row_map = lambda bi, i: (bi, i, 0)
    prev_map = lambda bi, i: (bi, jnp.maximum(i * halo_blocks_per_tile - 1, 0), 0)
    next_map = lambda bi, i: (bi, jnp.minimum((i + 1) * halo_blocks_per_tile, n_halo_blocks - 1), 0)
    return pl.pallas_call(
        _conv_ffn_kernel,
        grid=(b, tiles),
        in_specs=[
            pl.BlockSpec((1, t, D_MODEL), row_map),
            pl.BlockSpec((1, HALO, D_MODEL), prev_map),
            pl.BlockSpec((1, HALO, D_MODEL), next_map),
            _const_spec((1, D_MODEL)),
            _const_spec((D_MODEL, 2 * D_FF)),
            _const_spec((3, 2 * D_FF)),
            _const_spec((1, 2 * D_FF)),
            _const_spec((D_FF, D_MODEL)),
            _const_spec((1, D_MODEL)),
        ],
        out_specs=pl.BlockSpec((1, t, D_MODEL), row_map),
        out_shape=jax.ShapeDtypeStruct((b, s, D_MODEL), F32),
        scratch_shapes=[
            pltpu.VMEM((4, FF_CHUNK // LANES, t + 2 * HALO, LANES), F32),
            pltpu.VMEM((t, D_MODEL), F32),
        ],
        compiler_params=pltpu.CompilerParams(
            dimension_semantics=("parallel", "parallel"), vmem_limit_bytes=VMEM_LIMIT_BYTES),
        name="conv_ffn",
    )(x, x, x, g_pre, w_up, conv_w, conv_b, w_down, g_post)


def _rope_tables(seq_len):
    half = HEAD_DIM // 2
    inv = ROPE_THETA ** (-jnp.arange(half, dtype=F32) / half)
    ang = jnp.arange(seq_len, dtype=F32)[:, None] * inv[None, :]
    cos, sin = jnp.cos(ang), jnp.sin(ang)
    cos_head = jnp.concatenate([cos, cos], axis=-1)
    sin_head = jnp.concatenate([-sin, sin], axis=-1)
    reps = LANES // HEAD_DIM
    return jnp.tile(cos_head, (1, reps)), jnp.tile(sin_head, (1, reps))


def _band_bias(q_rows, stacked, keys, half_window, offset_step):
    off = jnp.arange(3, dtype=jnp.int32)[:, None, None] * offset_step
    i = (jnp.arange(stacked * q_rows, dtype=jnp.int32) % q_rows)[None, :, None]
    j = jnp.arange(keys, dtype=jnp.int32)[None, None, :]
    return jnp.where(jnp.abs(j - off - i) <= half_window, 0.0, NEG_INF).astype(F32)


def kernel(x, norm_mix_pre, w_in, sink, w_branch_a, w_branch_b, w_gate, b_gate, w_out,
           norm_mix_post, norm_ffn_pre, w_up, conv_w, conv_b, w_down, norm_ffn_post):
    b, s, d = x.shape
    assert d == D_MODEL and s % B_SPAN == 0 and s % A_STEP_ROWS == 0
    cos, sin_signed = _rope_tables(s)
    bias_a = _band_bias(A_Q_BLOCK, 1, A_KEY_WINDOW, A_HALF_WINDOW, A_Q_BLOCK)
    bias_b = _band_bias(B_Q_BLOCK, 2, B_KEY_WINDOW, B_HALF_WINDOW, B_HALF_WINDOW)
    for layer in range(norm_mix_pre.shape[0]):
        proj = _in_proj(x, norm_mix_pre[layer][None], w_in[layer].astype(BF16), cos, sin_signed)
        qa, ka, va = proj[:3]
        ya = _attn_a(sink[layer], bias_a, qa, ka, va)
        ob = []
        for g, (_, dil) in enumerate(B_PATTERNS):
            qb, kb, vb = proj[3 + 3 * g:6 + 3 * g]
            ob.append(_attn_b(bias_b, qb, kb, vb, dil))
        x = _mix_out(x, ya, ob, norm_mix_pre[layer][None],
                     w_gate[layer].astype(BF16), b_gate[layer][None],
                     w_branch_a[layer].astype(BF16), w_branch_b[layer].astype(BF16),
                     w_out[layer].astype(BF16), norm_mix_post[layer][None])
        x = _conv_ffn(x, norm_ffn_pre[layer][None],
                      w_up[layer].astype(BF16), conv_w[layer], conv_b[layer][None],
                      w_down[layer].astype(BF16),
                      norm_ffn_post[layer][None])
    return x
```

```python
import functools
import math

import jax
import jax.numpy as jnp
from jax import lax
from jax.experimental import pallas as pl
from jax.experimental.pallas import tpu as pltpu

D_MODEL = 1024
HEAD_DIM = 64
A_Q_HEADS = 8
A_KV_HEADS = 2
A_GROUP = A_Q_HEADS // A_KV_HEADS
A_HALF_WINDOW = 128
B_PATTERNS = ((128, 1), (512, 4), (2048, 16))
B_N_GROUPS = len(B_PATTERNS)
B_HEADS = 4
HEAD_ORDER = (0, 2, 1, 3)
B_HALF_WINDOW = 64
ROPE_THETA = 10000.0
D_FF = 3 * D_MODEL
RMS_EPS = 1e-6
NEG_INF = -1e30
LOG2_E = math.log2(math.e)

A_Q_COLS = A_Q_HEADS * HEAD_DIM
A_KV_COLS = A_KV_HEADS * HEAD_DIM
A_COLS = A_Q_COLS + 2 * A_KV_COLS
B_GROUP_COLS = B_HEADS * HEAD_DIM
B_PROJ_COLS = B_N_GROUPS * B_GROUP_COLS
IN_COLS = A_COLS + 3 * B_PROJ_COLS

LANES = 128
VMEM_LIMIT_BYTES = 56 * 1024 * 1024

ROW_TILE = 512
SUB_TILES = 2
FFN_SUB_TILES = 1
A_Q_BLOCK = 128
A_KEY_WINDOW = 3 * A_Q_BLOCK
A_STEP_ROWS = 1024
B_Q_BLOCK = 128
B_KEY_WINDOW = B_Q_BLOCK + 2 * B_HALF_WINDOW
B_SPAN = 2048
A_UNROLL = 4
B_UNROLL = 8
FF_CHUNK = 512
FF_CHUNKS = (FF_CHUNK,) * (D_FF // FF_CHUNK)
HALO = 8

BF16 = jnp.bfloat16
F32 = jnp.float32


def _dot(a, b):
    return jnp.dot(a, b, preferred_element_type=F32)


def _dot_nt(a, b):
    return lax.dot_general(a, b, (((1,), (1,)), ((), ())), preferred_element_type=F32)


def _rms_norm(x, gain):
    ms = jnp.mean(x * x, axis=-1, keepdims=True)
    return x * lax.rsqrt(ms + RMS_EPS) * gain


def _rope_chunk(p, cos, sin_signed, first_half):
    partner = jnp.where(first_half, pltpu.roll(p, 96, 1), pltpu.roll(p, 32, 1))
    return p * cos + partner * sin_signed


def _const_spec(shape):
    nd = len(shape)
    return pl.BlockSpec(shape, lambda *_: (0,) * nd, pipeline_mode=pl.Buffered(1))


def _in_proj_kernel(x_ref, gain_ref, w_ref, cos_ref, sin_ref,
                    qa_ref, ka_ref, va_ref,
                    qb0_ref, kb0_ref, vb0_ref,
                    qb1_ref, kb1_ref, vb1_ref,
                    qb2_ref, kb2_ref, vb2_ref,
                    slab_ref):
    t = ROW_TILE
    lane = lax.broadcasted_iota(jnp.int32, (t, LANES), 1)
    first_half = (lane % HEAD_DIM) < (HEAD_DIM // 2)
    low_head = lane < HEAD_DIM
    scale = HEAD_DIM ** -0.5 * LOG2_E
    out_refs = ((qb0_ref, kb0_ref, vb0_ref), (qb1_ref, kb1_ref, vb1_ref), (qb2_ref, kb2_ref, vb2_ref))

    def tile(sub):
        rows = slice(sub * t, (sub + 1) * t)
        h = _rms_norm(x_ref[0, rows, :], gain_ref[...]).astype(BF16)
        cos = cos_ref[rows, :]
        sin_signed = sin_ref[rows, :]

        def proj(col0, ncols):
            return _dot(h, w_ref[:, col0:col0 + ncols])

        def rope(p, mult=None):
            chunks = []
            for c in range(p.shape[1] // LANES):
                r = _rope_chunk(p[:, c * LANES:(c + 1) * LANES], cos, sin_signed, first_half)
                chunks.append(r if mult is None else r * mult)
            return chunks

        def dup_heads(p):
            swapped = pltpu.roll(p, HEAD_DIM, 1)
            return [jnp.where(low_head, p, swapped), jnp.where(low_head, swapped, p)]

        def windowed():
            for c, chunk in enumerate(rope(proj(0, A_Q_COLS), scale)):
                qa_ref[0, rows, c * LANES:(c + 1) * LANES] = chunk.astype(BF16)
            kv = proj(A_Q_COLS, 2 * A_KV_COLS)
            ka = rope(kv[:, :A_KV_COLS])[0]
            for c, chunk in enumerate(dup_heads(ka)):
                ka_ref[0, rows, c * LANES:(c + 1) * LANES] = chunk.astype(BF16)
            for c, chunk in enumerate(dup_heads(kv[:, A_KV_COLS:])):
                va_ref[0, rows, c * LANES:(c + 1) * LANES] = chunk.astype(BF16)

        def dilated(g):
            dil = B_PATTERNS[g][1]
            for kind in range(3):
                col0 = A_COLS + kind * B_PROJ_COLS + g * B_GROUP_COLS
                p = proj(col0, B_GROUP_COLS)
                if kind == 0:
                    chunks = rope(p, scale)
                elif kind == 1:
                    chunks = rope(p)
                else:
                    chunks = [p[:, c * LANES:(c + 1) * LANES] for c in range(B_GROUP_COLS // LANES)]
                o_ref = out_refs[g][kind]
                sub_rows = slice(sub * t // dil, (sub + 1) * t // dil)
                if dil == 1:
                    for c, chunk in enumerate(chunks):
                        o_ref[0, 0, sub_rows, c * LANES:(c + 1) * LANES] = chunk.astype(BF16)
                else:
                    for c, chunk in enumerate(chunks):
                        slab_ref[sub, kind, c] = chunk
                    for c in range(len(chunks)):
                        for res in range(dil):
                            picked = slab_ref[sub, kind, c, pl.ds(res, t // dil, stride=dil), :]
                            o_ref[0, res, sub_rows, c * LANES:(c + 1) * LANES] = picked.astype(BF16)

        dilated(2)
        dilated(1)
        windowed()
        dilated(0)

    for sub in range(SUB_TILES):
        tile(sub)


def _in_proj(x, gain, w_in, cos, sin_signed):
    b, s, _ = x.shape
    t = SUB_TILES * ROW_TILE
    grid = (b, s // t)
    row_map = lambda bi, i: (bi, i, 0)
    out_shape = [
        jax.ShapeDtypeStruct((b, s, A_Q_COLS), BF16),
        jax.ShapeDtypeStruct((b, s, 2 * A_KV_COLS), BF16),
        jax.ShapeDtypeStruct((b, s, 2 * A_KV_COLS), BF16),
    ]
    out_specs = [
        pl.BlockSpec((1, t, A_Q_COLS), row_map),
        pl.BlockSpec((1, t, 2 * A_KV_COLS), row_map),
        pl.BlockSpec((1, t, 2 * A_KV_COLS), row_map),
    ]
    for _, dil in B_PATTERNS:
        for _ in range(3):
            out_shape.append(jax.ShapeDtypeStruct((b, dil, s // dil, B_GROUP_COLS), BF16))
            out_specs.append(pl.BlockSpec((1, dil, t // dil, B_GROUP_COLS), lambda bi, i: (bi, 0, i, 0)))
    return pl.pallas_call(
        _in_proj_kernel,
        grid=grid,
        in_specs=[
            pl.BlockSpec((1, t, D_MODEL), row_map),
            _const_spec((1, D_MODEL)),
            _const_spec((D_MODEL, IN_COLS)),
            pl.BlockSpec((t, LANES), lambda bi, i: (i, 0)),
            pl.BlockSpec((t, LANES), lambda bi, i: (i, 0)),
        ],
        out_specs=out_specs,
        out_shape=out_shape,
        scratch_shapes=[pltpu.VMEM((SUB_TILES, 3, B_GROUP_COLS // LANES, ROW_TILE, LANES), F32)],
        compiler_params=pltpu.CompilerParams(
            dimension_semantics=("parallel", "parallel"), vmem_limit_bytes=VMEM_LIMIT_BYTES),
        name="in_proj",
    )(x, gain, w_in, cos, sin_signed)


def _attn_a_kernel(sink_ref, bias_ref, q_ref, k_ref, v_ref, o_ref):
    s = k_ref.shape[1]
    step_rows = q_ref.shape[1]
    blocks_per_step = step_rows // A_Q_BLOCK
    step = pl.program_id(1)
    low_head = lax.broadcasted_iota(jnp.int32, (A_Q_BLOCK, LANES), 1) < HEAD_DIM
    low_key = lax.broadcasted_iota(jnp.int32, (A_KEY_WINDOW, LANES), 1) < HEAD_DIM

    def blocks(it, carry):
        work = []
        for nn in range(A_UNROLL):
            blk = it * A_UNROLL + nn
            r0 = blk * A_Q_BLOCK
            q_blk_idx = step * blocks_per_step + blk
            start_blk = jnp.clip(q_blk_idx - 1, 0, s // A_Q_BLOCK - 3)
            start = start_blk * A_Q_BLOCK
            bias = bias_ref[q_blk_idx - start_blk]
            k_win = k_ref[0, pl.ds(start, A_KEY_WINDOW), :]
            v_win = v_ref[0, pl.ds(start, A_KEY_WINDOW), :]
            q_blk = q_ref[0, pl.ds(r0, A_Q_BLOCK), :]
            for j in range(A_KV_HEADS):
                kd = k_win[:, j * LANES:(j + 1) * LANES]
                lhs = []
                for g in HEAD_ORDER:
                    c0 = j * A_GROUP * HEAD_DIM + (g // 2) * LANES
                    pair = q_blk[:, c0:c0 + LANES]
                    keep = low_head if g % 2 == 0 else jnp.logical_not(low_head)
                    lhs.append(jnp.where(keep, pair, jnp.zeros_like(pair)))
                sc = _dot_nt(jnp.concatenate(lhs, axis=0), kd)
                work.append((r0, j, bias, v_win[:, j * LANES:(j + 1) * LANES], sc))

        soft = []
        for r0, j, bias, vd, sc in work:
            probs, sink_terms = {}, {}
            for slot, g in enumerate(HEAD_ORDER):
                sg = sc[slot * A_Q_BLOCK:(slot + 1) * A_Q_BLOCK] + bias
                sink = sink_ref[j * A_GROUP + g] * LOG2_E
                m = jnp.maximum(jnp.max(sg, axis=-1, keepdims=True), sink)
                probs[g] = jnp.exp2(sg - m).astype(BF16)
                sink_terms[g] = jnp.exp2(sink - m)
            soft.append((r0, j, vd, probs, sink_terms))

        pairs = range(A_GROUP // 2)
        for r0, j, vd, probs, sink_terms in soft:
            one = jnp.ones_like(vd)
            r_even = _dot(jnp.concatenate([probs[2 * pr] for pr in pairs], axis=0),
                          jnp.where(low_key, vd, one))
            r_odd = _dot(jnp.concatenate([probs[2 * pr + 1] for pr in pairs], axis=0),
                         jnp.where(low_key, one, vd))
            for pr in pairs:
                even = r_even[pr * A_Q_BLOCK:(pr + 1) * A_Q_BLOCK]
                odd = r_odd[pr * A_Q_BLOCK:(pr + 1) * A_Q_BLOCK]
                num = jnp.where(low_head, even, odd)
                den = jnp.where(low_head, odd + sink_terms[2 * pr + 1], even + sink_terms[2 * pr])
                out = num / pltpu.roll(den, HEAD_DIM, 1)
                c0 = j * A_GROUP * HEAD_DIM + pr * LANES
                o_ref[0, pl.ds(r0, A_Q_BLOCK), c0:c0 + LANES] = out.astype(BF16)
        return carry

    lax.fori_loop(0, blocks_per_step // A_UNROLL, blocks, 0)


def _attn_a(sink, bias, qa, ka, va):
    b, s, _ = qa.shape
    step_rows = A_STEP_ROWS
    return pl.pallas_call(
        _attn_a_kernel,
        grid=(b, s // step_rows),
        in_specs=[
            pl.BlockSpec(memory_space=pltpu.SMEM),
            _const_spec(bias.shape),
            pl.BlockSpec((1, step_rows, A_Q_COLS), lambda bi, i: (bi, i, 0)),
            pl.BlockSpec((1, s, 2 * A_KV_COLS), lambda bi, i: (bi, 0, 0)),
            pl.BlockSpec((1, s, 2 * A_KV_COLS), lambda bi, i: (bi, 0, 0)),
        ],
        out_specs=pl.BlockSpec((1, step_rows, A_Q_COLS), lambda bi, i: (bi, i, 0)),
        out_shape=jax.ShapeDtypeStruct((b, s, A_Q_COLS), BF16),
        compiler_params=pltpu.CompilerParams(
            dimension_semantics=("parallel", "parallel"), vmem_limit_bytes=VMEM_LIMIT_BYTES),
        name="attn_a",
    )(sink, bias, qa, ka, va)


def _attn_b_kernel(bias_ref, q_ref, k_ref, v_ref, o_ref, lse_ref, *, dil):
    sub_len = k_ref.shape[2]
    sub_rows = q_ref.shape[2]
    blocks = sub_rows // B_Q_BLOCK
    step = pl.program_id(1)
    low_head = lax.broadcasted_iota(jnp.int32, (B_Q_BLOCK, LANES), 1) < HEAD_DIM
    low_key = lax.broadcasted_iota(jnp.int32, (B_KEY_WINDOW, LANES), 1) < HEAD_DIM

    def units(it, carry):
        work = []
        for uu in range(B_UNROLL):
            u = it * B_UNROLL + uu
            res = u // blocks
            r0 = (u % blocks) * B_Q_BLOCK
            q_pos = (step * sub_rows + r0) // B_HALF_WINDOW
            start_pos = jnp.clip(q_pos - 1, 0, (sub_len - B_KEY_WINDOW) // B_HALF_WINDOW)
            start = start_pos * B_HALF_WINDOW
            bias = bias_ref[q_pos - start_pos]
            q_blk = q_ref[0, res, pl.ds(r0, B_Q_BLOCK), :]
            k_win = k_ref[0, res, pl.ds(start, B_KEY_WINDOW), :]
            v_win = v_ref[0, res, pl.ds(start, B_KEY_WINDOW), :]
            if dil == 1:
                rows = pl.ds(r0, B_Q_BLOCK)
            else:
                rows = pl.ds(r0 * dil + res, B_Q_BLOCK, stride=dil)
            for pr in range(B_HEADS // 2):
                qp = q_blk[:, pr * LANES:(pr + 1) * LANES]
                zero = jnp.zeros_like(qp)
                lhs = jnp.concatenate([jnp.where(low_head, qp, zero), jnp.where(low_head, zero, qp)], axis=0)
                sc = _dot_nt(lhs, k_win[:, pr * LANES:(pr + 1) * LANES])
                work.append((rows, pr, bias, v_win[:, pr * LANES:(pr + 1) * LANES], sc))

        soft = []
        for rows, pr, bias, vp, sc in work:
            sc = sc + bias
            m = jnp.max(sc, axis=-1, keepdims=True)
            soft.append((rows, pr, vp, m, jnp.exp2(sc - m).astype(BF16)))

        for rows, pr, vp, m, p in soft:
            one = jnp.ones_like(vp)
            top = _dot(p[:B_Q_BLOCK], jnp.where(low_key, vp, one))
            bot = _dot(p[B_Q_BLOCK:], jnp.where(low_key, one, vp))
            denom = pltpu.roll(jnp.where(low_head, bot, top), HEAD_DIM, 1)
            o_ref[0, pr, rows, :] = jnp.where(low_head, top, bot) / denom
            lse_ref[0, pr, rows, :] = jnp.where(low_head, m[:B_Q_BLOCK], m[B_Q_BLOCK:]) + jnp.log2(denom)
        return carry

    lax.fori_loop(0, dil * blocks // B_UNROLL, units, 0)


def _attn_b(bias, q, k, v, dil):
    b, _, sub_len, _ = q.shape
    s = sub_len * dil
    sub_rows = B_SPAN // dil
    nslab = B_GROUP_COLS // LANES
    out_sds = jax.ShapeDtypeStruct((b, nslab, s, LANES), F32)
    out_spec = pl.BlockSpec((1, nslab, B_SPAN, LANES), lambda bi, i: (bi, 0, i, 0))
    return pl.pallas_call(
        functools.partial(_attn_b_kernel, dil=dil),
        grid=(b, s // B_SPAN),
        in_specs=[
            _const_spec(bias.shape),
            pl.BlockSpec((1, dil, sub_rows, B_GROUP_COLS), lambda bi, i: (bi, 0, i, 0)),
            pl.BlockSpec((1, dil, sub_len, B_GROUP_COLS), lambda bi, i: (bi, 0, 0, 0)),
            pl.BlockSpec((1, dil, sub_len, B_GROUP_COLS), lambda bi, i: (bi, 0, 0, 0)),
        ],
        out_specs=[out_spec, out_spec],
        out_shape=[out_sds, out_sds],
        compiler_params=pltpu.CompilerParams(
            dimension_semantics=("parallel", "parallel"), vmem_limit_bytes=VMEM_LIMIT_BYTES),
        name=f"attn_b_d{dil}",
    )(bias, q, k, v)


def _mix_out_kernel(x_ref, ya_ref, o0_ref, l0_ref, o1_ref, l1_ref, o2_ref, l2_ref,
                    g_pre_ref, w_gate_ref, b_gate_ref, w_a_ref, w_b_ref, w_out_ref, g_post_ref,
                    out_ref):
    t = ROW_TILE

    def tile(sub):
        rows = slice(sub * t, (sub + 1) * t)
        x = x_ref[0, rows, :]
        h = _rms_norm(x, g_pre_ref[...]).astype(BF16)
        yb = []
        for c in range(B_GROUP_COLS // LANES):
            l0, l1, l2 = l0_ref[0, c, rows, :], l1_ref[0, c, rows, :], l2_ref[0, c, rows, :]
            m = jnp.maximum(jnp.maximum(l0, l1), l2)
            e0, e1, e2 = jnp.exp2(l0 - m), jnp.exp2(l1 - m), jnp.exp2(l2 - m)
            num = e0 * o0_ref[0, c, rows, :] + e1 * o1_ref[0, c, rows, :] + e2 * o2_ref[0, c, rows, :]
            yb.append((num / (e0 + e1 + e2)).astype(BF16))
        yb = jnp.concatenate(yb, axis=-1)
        gates = jax.nn.sigmoid(_dot(h, w_gate_ref[...]) + b_gate_ref[...])
        merged = (gates[:, :D_MODEL] * _dot(ya_ref[0, rows, :], w_a_ref[...])
                  + gates[:, D_MODEL:] * _dot(yb, w_b_ref[...]))
        mix = _dot(merged.astype(BF16), w_out_ref[...])
        out_ref[0, rows, :] = x + _rms_norm(mix, g_post_ref[...])

    for sub in range(SUB_TILES):
        tile(sub)


def _mix_out(x, ya, ob, g_pre, w_gate, b_gate, w_a, w_b, w_out, g_post):
    b, s, _ = x.shape
    t = SUB_TILES * ROW_TILE
    nslab = B_GROUP_COLS // LANES
    row_map = lambda bi, i: (bi, i, 0)
    slab_spec = pl.BlockSpec((1, nslab, t, LANES), lambda bi, i: (bi, 0, i, 0))
    flat_ob = [a for pair in ob for a in pair]
    return pl.pallas_call(
        _mix_out_kernel,
        grid=(b, s // t),
        in_specs=[
            pl.BlockSpec((1, t, D_MODEL), row_map),
            pl.BlockSpec((1, t, A_Q_COLS), row_map),
        ] + [slab_spec] * len(flat_ob) + [
            _const_spec((1, D_MODEL)),
            _const_spec((D_MODEL, 2 * D_MODEL)),
            _const_spec((1, 2 * D_MODEL)),
            _const_spec((A_Q_COLS, D_MODEL)),
            _const_spec((B_GROUP_COLS, D_MODEL)),
            _const_spec((D_MODEL, D_MODEL)),
            _const_spec((1, D_MODEL)),
        ],
        out_specs=pl.BlockSpec((1, t, D_MODEL), row_map),
        out_shape=jax.ShapeDtypeStruct((b, s, D_MODEL), F32),
        compiler_params=pltpu.CompilerParams(
            dimension_semantics=("parallel", "parallel"), vmem_limit_bytes=VMEM_LIMIT_BYTES),
        name="mix_out",
    )(x, ya, *flat_ob, g_pre, w_gate, b_gate, w_a, w_b, w_out, g_post)


def _gelu_tanh(x):
    c = -2.0 * math.sqrt(2.0 / math.pi)
    return x / (1.0 + jnp.exp(x * (c + (c * 0.044715) * (x * x))))


def _conv_ffn_kernel(x_ref, prev_ref, next_ref, g_pre_ref, w_up_ref, conv_w_ref, conv_b_ref,
                     w_down_ref, g_post_ref, out_ref, u_ref, acc_ref, perm_ref):
    t = ROW_TILE
    half = t // 2
    i = pl.program_id(1)
    last = pl.num_programs(1) - 1
    row = lax.broadcasted_iota(jnp.int32, (t + 2 * HALO, 1), 0)
    starts = [sum(FF_CHUNKS[:c]) for c in range(len(FF_CHUNKS))]

    def normed(sub):
        lo, hi = sub * t, (sub + 1) * t
        before = prev_ref[0] if sub == 0 else x_ref[0, lo - HALO:lo, :]
        after = next_ref[0] if sub == FFN_SUB_TILES - 1 else x_ref[0, hi:hi + HALO, :]
        xe = jnp.concatenate([before, x_ref[0, lo:hi, :], after], axis=0)
        hn = _rms_norm(xe, g_pre_ref[...])
        if sub == 0:
            hn = jnp.where(jnp.logical_or(row >= HALO, i > 0), hn, 0.0)
        if sub == FFN_SUB_TILES - 1:
            hn = jnp.where(jnp.logical_or(row < HALO + t, i < last), hn, 0.0)
        return hn.astype(BF16)

    def col_starts(c):
        return (starts[c], D_FF + starts[c])

    def project(h, c, slot):
        for part, col0 in enumerate(col_starts(c)):
            up = _dot(h, w_up_ref[:, col0:col0 + FF_CHUNKS[c]])
            for j in range(FF_CHUNKS[c] // LANES):
                u_ref[2 * slot + part, j] = up[:, j * LANES:(j + 1) * LANES]

    def conv(c, slot, part):
        buf, col0 = 2 * slot + part, col_starts(c)[part]
        pieces = []
        for j in range(FF_CHUNKS[c] // LANES):
            cols = slice(col0 + j * LANES, col0 + (j + 1) * LANES)
            w0, w1, w2 = conv_w_ref[0:1, cols], conv_w_ref[1:2, cols], conv_w_ref[2:3, cols]
            bias = conv_b_ref[:, cols]
            r = [u_ref[buf, j, pl.ds(HALO - 1 + k, half, stride=2), :] for k in range(4)]
            even = r[0] * w0 + r[1] * w1 + r[2] * w2 + bias
            odd = r[1] * w0 + r[2] * w1 + r[3] * w2 + bias
            pieces.append(jnp.concatenate([even, odd], axis=0))
        return jnp.concatenate(pieces, axis=-1)

    def finish(sub):
        y = _rms_norm(acc_ref[sub], g_post_ref[...])
        for j in range(D_MODEL // LANES):
            perm_ref[j, pl.ds(0, half, stride=2), :] = y[:half, j * LANES:(j + 1) * LANES]
            perm_ref[j, pl.ds(1, half, stride=2), :] = y[half:, j * LANES:(j + 1) * LANES]
        y_nat = jnp.concatenate([perm_ref[j] for j in range(D_MODEL // LANES)], axis=-1)
        rows = slice(sub * t, (sub + 1) * t)
        out_ref[0, rows, :] = x_ref[0, rows, :] + y_nat

    items = [(sub, c) for sub in range(FFN_SUB_TILES) for c in range(len(FF_CHUNKS))]
    hs = {0: normed(0)}
    project(hs[0], 0, 0)
    for k, (sub, c) in enumerate(items):
        if k + 1 < len(items):
            nsub, nc = items[k + 1]
            if nsub not in hs:
                hs[nsub] = normed(nsub)
            project(hs[nsub], nc, (k + 1) % 2)
        act = (_gelu_tanh(conv(c, k % 2, 0)) * conv(c, k % 2, 1)).astype(BF16)
        part = _dot(act, w_down_ref[starts[c]:starts[c] + FF_CHUNKS[c], :])
        if c == 0:
            acc_ref[sub] = part
        else:
            acc_ref[sub] += part
        if c == len(FF_CHUNKS) - 1:
            finish(sub)


def _conv_ffn(x, g_pre, w_up, conv_w, conv_b, w_down, g_post):
    b, s, _ = x.shape
    t = FFN_SUB_TILES * ROW_TILE
    tiles = s // t
    halo_blocks_per_tile = t // HALO
    n_halo_blocks = s // HALO
    row_map = lambda bi, i: (bi, i, 0)
    prev_map = lambda bi, i: (bi, jnp.maximum(i * halo_blocks_per_tile - 1, 0), 0)
    next_map = lambda bi, i: (bi, jnp.minimum((i + 1) * halo_blocks_per_tile, n_halo_blocks - 1), 0)
    return pl.pallas_call(
        _conv_ffn_kernel,
        grid=(b, tiles),
        in_specs=[
            pl.BlockSpec((1, t, D_MODEL), row_map),
            pl.BlockSpec((1, HALO, D_MODEL), prev_map),
            pl.BlockSpec((1, HALO, D_MODEL), next_map),
            _const_spec((1, D_MODEL)),
            _const_spec((D_MODEL, 2 * D_FF)),
            _const_spec((3, 2 * D_FF)),
            _const_spec((1, 2 * D_FF)),
            _const_spec((D_FF, D_MODEL)),
            _const_spec((1, D_MODEL)),
        ],
        out_specs=pl.BlockSpec((1, t, D_MODEL), row_map),
        out_shape=jax.ShapeDtypeStruct((b, s, D_MODEL), F32),
        scratch_shapes=[
            pltpu.VMEM((4, FF_CHUNK // LANES, ROW_TILE + 2 * HALO, LANES), F32),
            pltpu.VMEM((FFN_SUB_TILES, ROW_TILE, D_MODEL), F32),
            pltpu.VMEM((D_MODEL // LANES, ROW_TILE, LANES), F32),
        ],
        compiler_params=pltpu.CompilerParams(
            dimension_semantics=("parallel", "parallel"), vmem_limit_bytes=VMEM_LIMIT_BYTES),
        name="conv_ffn",
    )(x, x, x, g_pre, w_up, conv_w, conv_b, w_down, g_post)


def _rope_tables(seq_len):
    half = HEAD_DIM // 2
    inv = ROPE_THETA ** (-jnp.arange(half, dtype=F32) / half)
    ang = jnp.arange(seq_len, dtype=F32)[:, None] * inv[None, :]
    cos, sin = jnp.cos(ang), jnp.sin(ang)
    cos_head = jnp.concatenate([cos, cos], axis=-1)
    sin_head = jnp.concatenate([-sin, sin], axis=-1)
    reps = LANES // HEAD_DIM
    return jnp.tile(cos_head, (1, reps)), jnp.tile(sin_head, (1, reps))


def _band_bias(q_rows, stacked, keys, half_window, offset_step):
    off = jnp.arange(3, dtype=jnp.int32)[:, None, None] * offset_step
    i = (jnp.arange(stacked * q_rows, dtype=jnp.int32) % q_rows)[None, :, None]
    j = jnp.arange(keys, dtype=jnp.int32)[None, None, :]
    return jnp.where(jnp.abs(j - off - i) <= half_window, 0.0, NEG_INF).astype(F32)


def kernel(x, norm_mix_pre, w_in, sink, w_branch_a, w_branch_b, w_gate, b_gate, w_out,
           norm_mix_post, norm_ffn_pre, w_up, conv_w, conv_b, w_down, norm_ffn_post):
    b, s, d = x.shape
    assert d == D_MODEL and s % B_SPAN == 0 and s % A_STEP_ROWS == 0 and s % (SUB_TILES * ROW_TILE) == 0
    cos, sin_signed = _rope_tables(s)
    bias_a = _band_bias(A_Q_BLOCK, 1, A_KEY_WINDOW, A_HALF_WINDOW, A_Q_BLOCK)
    bias_b = _band_bias(B_Q_BLOCK, 2, B_KEY_WINDOW, B_HALF_WINDOW, B_HALF_WINDOW)
    for layer in range(norm_mix_pre.shape[0]):
        proj = _in_proj(x, norm_mix_pre[layer][None], w_in[layer].astype(BF16), cos, sin_signed)
        qa, ka, va = proj[:3]
        ya = _attn_a(sink[layer], bias_a, qa, ka, va)
        ob = []
        for g, (_, dil) in enumerate(B_PATTERNS):
            qb, kb, vb = proj[3 + 3 * g:6 + 3 * g]
            ob.append(_attn_b(bias_b, qb, kb, vb, dil))
        x = _mix_out(x, ya, ob, norm_mix_pre[layer][None],
                     w_gate[layer].astype(BF16), b_gate[layer][None],
                     w_branch_a[layer].astype(BF16), w_branch_b[layer].astype(BF16),
                     w_out[layer].astype(BF16), norm_mix_post[layer][None])
        x = _conv_ffn(x, norm_ffn_pre[layer][None],
                      w_up[layer].astype(BF16), conv_w[layer], conv_b[layer][None],
                      w_down[layer].astype(BF16),
                      norm_ffn_post[layer][None])
    return x
```

```python
import functools
import math

import jax
import jax.numpy as jnp
from jax import lax
from jax.experimental import pallas as pl
from jax.experimental.pallas import tpu as pltpu

D_MODEL = 1024
HEAD_DIM = 64
A_Q_HEADS = 8
A_KV_HEADS = 2
A_GROUP = A_Q_HEADS // A_KV_HEADS
A_HALF_WINDOW = 128
B_PATTERNS = ((128, 1), (512, 4), (2048, 16))
B_N_GROUPS = len(B_PATTERNS)
B_HEADS = 4
HEAD_ORDER = (0, 2, 1, 3)
B_HALF_WINDOW = 64
ROPE_THETA = 10000.0
D_FF = 3 * D_MODEL
RMS_EPS = 1e-6
NEG_INF = -1e30
LOG2_E = math.log2(math.e)

A_Q_COLS = A_Q_HEADS * HEAD_DIM
A_KV_COLS = A_KV_HEADS * HEAD_DIM
A_COLS = A_Q_COLS + 2 * A_KV_COLS
B_GROUP_COLS = B_HEADS * HEAD_DIM
B_PROJ_COLS = B_N_GROUPS * B_GROUP_COLS
IN_COLS = A_COLS + 3 * B_PROJ_COLS

LANES = 128
VMEM_LIMIT_BYTES = 56 * 1024 * 1024

ROW_TILE = 512
SUB_TILES = 2
FFN_SUB_TILES = 1
A_Q_BLOCK = 128
A_KEY_WINDOW = 3 * A_Q_BLOCK
B_Q_BLOCK = 128
B_KEY_WINDOW = B_Q_BLOCK + 2 * B_HALF_WINDOW
B_SPAN = 2048
B_UNROLL = 8
FF_CHUNK = 512
FF_CHUNKS = (FF_CHUNK,) * (D_FF // FF_CHUNK)
FF_LOOKAHEAD = 2
HALO = 8

BF16 = jnp.bfloat16
F32 = jnp.float32


def _dot(a, b):
    return jnp.dot(a, b, preferred_element_type=F32)


def _dot_nt(a, b):
    return lax.dot_general(a, b, (((1,), (1,)), ((), ())), preferred_element_type=F32)


def _rms_norm(x, gain):
    ms = jnp.mean(x * x, axis=-1, keepdims=True)
    return x * lax.rsqrt(ms + RMS_EPS) * gain


def _rope_chunk(p, cos, sin_signed, first_half):
    partner = jnp.where(first_half, pltpu.roll(p, 96, 1), pltpu.roll(p, 32, 1))
    return p * cos + partner * sin_signed


def _const_spec(shape):
    nd = len(shape)
    return pl.BlockSpec(shape, lambda *_: (0,) * nd, pipeline_mode=pl.Buffered(1))


def _in_proj_kernel(x_ref, gain_ref, w_ref, cos_ref, sin_ref,
                    qa_ref, ka_ref, va_ref,
                    qb0_ref, kb0_ref, vb0_ref,
                    qb1_ref, kb1_ref, vb1_ref,
                    qb2_ref, kb2_ref, vb2_ref,
                    slab_ref):
    t = ROW_TILE
    lane = lax.broadcasted_iota(jnp.int32, (t, LANES), 1)
    first_half = (lane % HEAD_DIM) < (HEAD_DIM // 2)
    low_head = lane < HEAD_DIM
    scale = HEAD_DIM ** -0.5 * LOG2_E
    out_refs = ((qb0_ref, kb0_ref, vb0_ref), (qb1_ref, kb1_ref, vb1_ref), (qb2_ref, kb2_ref, vb2_ref))

    def tile(sub):
        rows = slice(sub * t, (sub + 1) * t)
        h = _rms_norm(x_ref[0, rows, :], gain_ref[...]).astype(BF16)
        cos = cos_ref[rows, :]
        sin_signed = sin_ref[rows, :]

        def proj(col0, ncols):
            return _dot(h, w_ref[:, col0:col0 + ncols])

        def rope(p, mult=None):
            chunks = []
            for c in range(p.shape[1] // LANES):
                r = _rope_chunk(p[:, c * LANES:(c + 1) * LANES], cos, sin_signed, first_half)
                chunks.append(r if mult is None else r * mult)
            return chunks

        def dup_heads(p):
            swapped = pltpu.roll(p, HEAD_DIM, 1)
            return [jnp.where(low_head, p, swapped), jnp.where(low_head, swapped, p)]

        def windowed():
            for c, chunk in enumerate(rope(proj(0, A_Q_COLS), scale)):
                qa_ref[0, rows, c * LANES:(c + 1) * LANES] = chunk.astype(BF16)
            kv = proj(A_Q_COLS, 2 * A_KV_COLS)
            ka = rope(kv[:, :A_KV_COLS])[0]
            for c, chunk in enumerate(dup_heads(ka)):
                ka_ref[0, rows, c * LANES:(c + 1) * LANES] = chunk.astype(BF16)
            for c, chunk in enumerate(dup_heads(kv[:, A_KV_COLS:])):
                va_ref[0, rows, c * LANES:(c + 1) * LANES] = chunk.astype(BF16)

        def dilated(g):
            dil = B_PATTERNS[g][1]
            for kind in range(3):
                col0 = A_COLS + kind * B_PROJ_COLS + g * B_GROUP_COLS
                p = proj(col0, B_GROUP_COLS)
                if kind == 0:
                    chunks = rope(p, scale)
                elif kind == 1:
                    chunks = rope(p)
                else:
                    chunks = [p[:, c * LANES:(c + 1) * LANES] for c in range(B_GROUP_COLS // LANES)]
                o_ref = out_refs[g][kind]
                sub_rows = slice(sub * t // dil, (sub + 1) * t // dil)
                if dil == 1:
                    for c, chunk in enumerate(chunks):
                        o_ref[0, 0, sub_rows, c * LANES:(c + 1) * LANES] = chunk.astype(BF16)
                else:
                    for c, chunk in enumerate(chunks):
                        slab_ref[sub, kind, c] = chunk
                    for c in range(len(chunks)):
                        for res in range(dil):
                            picked = slab_ref[sub, kind, c, pl.ds(res, t // dil, stride=dil), :]
                            o_ref[0, res, sub_rows, c * LANES:(c + 1) * LANES] = picked.astype(BF16)

        dilated(2)
        dilated(1)
        windowed()
        dilated(0)

    for sub in range(SUB_TILES):
        tile(sub)


def _in_proj(x, gain, w_in, cos, sin_signed):
    b, s, _ = x.shape
    t = SUB_TILES * ROW_TILE
    grid = (b, s // t)
    row_map = lambda bi, i: (bi, i, 0)
    out_shape = [
        jax.ShapeDtypeStruct((b, s, A_Q_COLS), BF16),
        jax.ShapeDtypeStruct((b, s, 2 * A_KV_COLS), BF16),
        jax.ShapeDtypeStruct((b, s, 2 * A_KV_COLS), BF16),
    ]
    out_specs = [
        pl.BlockSpec((1, t, A_Q_COLS), row_map),
        pl.BlockSpec((1, t, 2 * A_KV_COLS), row_map),
        pl.BlockSpec((1, t, 2 * A_KV_COLS), row_map),
    ]
    for _, dil in B_PATTERNS:
        for _ in range(3):
            out_shape.append(jax.ShapeDtypeStruct((b, dil, s // dil, B_GROUP_COLS), BF16))
            out_specs.append(pl.BlockSpec((1, dil, t // dil, B_GROUP_COLS), lambda bi, i: (bi, 0, i, 0)))
    return pl.pallas_call(
        _in_proj_kernel,
        grid=grid,
        in_specs=[
            pl.BlockSpec((1, t, D_MODEL), row_map),
            _const_spec((1, D_MODEL)),
            _const_spec((D_MODEL, IN_COLS)),
            pl.BlockSpec((t, LANES), lambda bi, i: (i, 0)),
            pl.BlockSpec((t, LANES), lambda bi, i: (i, 0)),
        ],
        out_specs=out_specs,
        out_shape=out_shape,
        scratch_shapes=[pltpu.VMEM((SUB_TILES, 3, B_GROUP_COLS // LANES, ROW_TILE, LANES), F32)],
        compiler_params=pltpu.CompilerParams(
            dimension_semantics=("parallel", "parallel"), vmem_limit_bytes=VMEM_LIMIT_BYTES),
        name="in_proj",
    )(x, gain, w_in, cos, sin_signed)


def _attn_b_kernel(bias_ref, q_ref, k_ref, v_ref, o_ref, lse_ref, *, dil):
    sub_len = k_ref.shape[2]
    sub_rows = q_ref.shape[2]
    blocks = sub_rows // B_Q_BLOCK
    step = pl.program_id(1)
    low_head = lax.broadcasted_iota(jnp.int32, (B_Q_BLOCK, LANES), 1) < HEAD_DIM
    low_key = lax.broadcasted_iota(jnp.int32, (B_KEY_WINDOW, LANES), 1) < HEAD_DIM

    def units(it, carry):
        work = []
        for uu in range(B_UNROLL):
            u = it * B_UNROLL + uu
            res = u // blocks
            r0 = (u % blocks) * B_Q_BLOCK
            q_pos = (step * sub_rows + r0) // B_HALF_WINDOW
            start_pos = jnp.clip(q_pos - 1, 0, (sub_len - B_KEY_WINDOW) // B_HALF_WINDOW)
            start = start_pos * B_HALF_WINDOW
            bias = bias_ref[q_pos - start_pos]
            q_blk = q_ref[0, res, pl.ds(r0, B_Q_BLOCK), :]
            k_win = k_ref[0, res, pl.ds(start, B_KEY_WINDOW), :]
            v_win = v_ref[0, res, pl.ds(start, B_KEY_WINDOW), :]
            if dil == 1:
                rows = pl.ds(r0, B_Q_BLOCK)
            else:
                rows = pl.ds(r0 * dil + res, B_Q_BLOCK, stride=dil)
            for pr in range(B_HEADS // 2):
                qp = q_blk[:, pr * LANES:(pr + 1) * LANES]
                zero = jnp.zeros_like(qp)
                lhs = jnp.concatenate([jnp.where(low_head, qp, zero), jnp.where(low_head, zero, qp)], axis=0)
                sc = _dot_nt(lhs, k_win[:, pr * LANES:(pr + 1) * LANES])
                work.append((rows, pr, bias, v_win[:, pr * LANES:(pr + 1) * LANES], sc))

        soft = []
        for rows, pr, bias, vp, sc in work:
            sc = sc + bias
            m = jnp.max(sc, axis=-1, keepdims=True)
            soft.append((rows, pr, vp, m, jnp.exp2(sc - m).astype(BF16)))

        for rows, pr, vp, m, p in soft:
            one = jnp.ones_like(vp)
            top = _dot(p[:B_Q_BLOCK], jnp.where(low_key, vp, one))
            bot = _dot(p[B_Q_BLOCK:], jnp.where(low_key, one, vp))
            denom = pltpu.roll(jnp.where(low_head, bot, top), HEAD_DIM, 1)
            o_ref[0, pr, rows, :] = jnp.where(low_head, top, bot) / denom
            lse_ref[0, pr, rows, :] = jnp.where(low_head, m[:B_Q_BLOCK], m[B_Q_BLOCK:]) + jnp.log2(denom)
        return carry

    lax.fori_loop(0, dil * blocks // B_UNROLL, units, 0)


def _attn_b(bias, q, k, v, dil):
    b, _, sub_len, _ = q.shape
    s = sub_len * dil
    sub_rows = B_SPAN // dil
    nslab = B_GROUP_COLS // LANES
    out_sds = jax.ShapeDtypeStruct((b, nslab, s, LANES), F32)
    out_spec = pl.BlockSpec((1, nslab, B_SPAN, LANES), lambda bi, i: (bi, 0, i, 0))
    return pl.pallas_call(
        functools.partial(_attn_b_kernel, dil=dil),
        grid=(b, s // B_SPAN),
        in_specs=[
            _const_spec(bias.shape),
            pl.BlockSpec((1, dil, sub_rows, B_GROUP_COLS), lambda bi, i: (bi, 0, i, 0)),
            pl.BlockSpec((1, dil, sub_len, B_GROUP_COLS), lambda bi, i: (bi, 0, 0, 0)),
            pl.BlockSpec((1, dil, sub_len, B_GROUP_COLS), lambda bi, i: (bi, 0, 0, 0)),
        ],
        out_specs=[out_spec, out_spec],
        out_shape=[out_sds, out_sds],
        compiler_params=pltpu.CompilerParams(
            dimension_semantics=("parallel", "parallel"), vmem_limit_bytes=VMEM_LIMIT_BYTES),
        name=f"attn_b_d{dil}",
    )(bias, q, k, v)


def _window(main_ref, prev_ref, next_ref, lead, lo, hi, length, cols):
    parts = []
    if lo < 0:
        halo = prev_ref.shape[-2]
        parts.append(prev_ref[lead + (slice(halo + lo, halo), cols)])
        lo = 0
    parts.append(main_ref[lead + (slice(lo, min(hi, length)), cols)])
    if hi > length:
        parts.append(next_ref[lead + (slice(0, hi - length), cols)])
    return parts[0] if len(parts) == 1 else jnp.concatenate(parts, axis=0)


def _mix_attn_kernel(sink_ref, bias_a_ref, bias_b_ref, x_ref,
                     qa_ref, ka_ref, ka_p_ref, ka_n_ref, va_ref, va_p_ref, va_n_ref,
                     qb0_ref, kb0_ref, kb0_p_ref, kb0_n_ref, vb0_ref, vb0_p_ref, vb0_n_ref,
                     qb1_ref, kb1_ref, kb1_p_ref, kb1_n_ref, vb1_ref, vb1_p_ref, vb1_n_ref,
                     o2_ref, l2_ref,
                     g_pre_ref, w_gate_ref, b_gate_ref, w_a_ref, w_b_ref, w_out_ref, g_post_ref,
                     out_ref, ya_ref, ob_ref):
    t = ROW_TILE
    step_rows = t
    i = pl.program_id(1)
    last = pl.num_programs(1) - 1
    low_head = lax.broadcasted_iota(jnp.int32, (A_Q_BLOCK, LANES), 1) < HEAD_DIM
    low_key_a = lax.broadcasted_iota(jnp.int32, (A_KEY_WINDOW, LANES), 1) < HEAD_DIM
    low_key_b = lax.broadcasted_iota(jnp.int32, (B_KEY_WINDOW, LANES), 1) < HEAD_DIM
    dil1 = B_PATTERNS[1][1]
    sub_rows = step_rows // dil1

    def edge_index(first, final):
        idx = 1
        if first:
            idx = jnp.where(i == 0, 0, idx)
        if final:
            idx = jnp.where(i == last, 2, idx)
        return idx

    def scores(grp_idx):
        work_a, work_b = [], []
        n = grp_idx
        lo, hi = (n - 1) * A_Q_BLOCK, (n + 2) * A_Q_BLOCK
        bias = bias_a_ref[edge_index(n == 0, hi > step_rows)]
        q_blk = qa_ref[0, n * A_Q_BLOCK:(n + 1) * A_Q_BLOCK, :]
        for j in range(A_KV_HEADS):
            cols = slice(j * LANES, (j + 1) * LANES)
            kd = _window(ka_ref, ka_p_ref, ka_n_ref, (0,), lo, hi, step_rows, cols)
            vd = _window(va_ref, va_p_ref, va_n_ref, (0,), lo, hi, step_rows, cols)
            lhs = []
            for g in HEAD_ORDER:
                c0 = j * A_GROUP * HEAD_DIM + (g // 2) * LANES
                pair = q_blk[:, c0:c0 + LANES]
                keep = low_head if g % 2 == 0 else jnp.logical_not(low_head)
                lhs.append(jnp.where(keep, pair, jnp.zeros_like(pair)))
            sc = _dot_nt(jnp.concatenate(lhs, axis=0), kd)
            work_a.append((n, j, bias, vd, sc))
        units = [(0, (0, 0), grp_idx, step_rows), (1, (0, grp_idx), 0, sub_rows)]
        for grp, lead, n, length in units:
            q_ref, k_ref, kp_ref, kn_ref, v_ref, vp_ref, vn_ref = (
                (qb0_ref, kb0_ref, kb0_p_ref, kb0_n_ref, vb0_ref, vb0_p_ref, vb0_n_ref) if grp == 0 else
                (qb1_ref, kb1_ref, kb1_p_ref, kb1_n_ref, vb1_ref, vb1_p_ref, vb1_n_ref))
            lo, hi = n * B_Q_BLOCK - B_HALF_WINDOW, (n + 1) * B_Q_BLOCK + B_HALF_WINDOW
            bias = bias_b_ref[edge_index(lo < 0, hi > length)]
            if grp == 0:
                rows = pl.ds(n * B_Q_BLOCK, B_Q_BLOCK)
            else:
                rows = pl.ds(n * B_Q_BLOCK * dil1 + lead[1], B_Q_BLOCK, stride=dil1)
            for pr in range(B_HEADS // 2):
                cols = slice(pr * LANES, (pr + 1) * LANES)
                qp = q_ref[lead + (slice(n * B_Q_BLOCK, (n + 1) * B_Q_BLOCK), cols)]
                kp = _window(k_ref, kp_ref, kn_ref, lead, lo, hi, length, cols)
                vp = _window(v_ref, vp_ref, vn_ref, lead, lo, hi, length, cols)
                zero = jnp.zeros_like(qp)
                lhs = jnp.concatenate([jnp.where(low_head, qp, zero), jnp.where(low_head, zero, qp)], axis=0)
                work_b.append((grp, rows, pr, bias, vp, _dot_nt(lhs, kp)))
        return work_a, work_b

    def softmaxes(work):
        work_a, work_b = work
        soft_a, soft_b = [], []
        for n, j, bias, vd, sc in work_a:
            probs, sink_terms = {}, {}
            for slot, g in enumerate(HEAD_ORDER):
                sg = sc[slot * A_Q_BLOCK:(slot + 1) * A_Q_BLOCK] + bias
                sink = sink_ref[j * A_GROUP + g] * LOG2_E
                m = jnp.maximum(jnp.max(sg, axis=-1, keepdims=True), sink)
                probs[g] = jnp.exp2(sg - m).astype(BF16)
                sink_terms[g] = jnp.exp2(sink - m)
            soft_a.append((n, j, vd, probs, sink_terms))
        for grp, rows, pr, bias, vp, sc in work_b:
            sc = sc + bias
            m = jnp.max(sc, axis=-1, keepdims=True)
            soft_b.append((grp, rows, pr, vp, m, jnp.exp2(sc - m).astype(BF16)))
        return soft_a, soft_b

    def values(soft):
        soft_a, soft_b = soft
        pairs = range(A_GROUP // 2)
        for n, j, vd, probs, sink_terms in soft_a:
            one = jnp.ones_like(vd)
            r_even = _dot(jnp.concatenate([probs[2 * pr] for pr in pairs], axis=0),
                          jnp.where(low_key_a, vd, one))
            r_odd = _dot(jnp.concatenate([probs[2 * pr + 1] for pr in pairs], axis=0),
                         jnp.where(low_key_a, one, vd))
            for pr in pairs:
                even = r_even[pr * A_Q_BLOCK:(pr + 1) * A_Q_BLOCK]
                odd = r_odd[pr * A_Q_BLOCK:(pr + 1) * A_Q_BLOCK]
                num = jnp.where(low_head, even, odd)
                den = jnp.where(low_head, odd + sink_terms[2 * pr + 1], even + sink_terms[2 * pr])
                c0 = j * A_GROUP * HEAD_DIM + pr * LANES
                ya_ref[n * A_Q_BLOCK:(n + 1) * A_Q_BLOCK, c0:c0 + LANES] = (
                    num / pltpu.roll(den, HEAD_DIM, 1)).astype(BF16)
        for grp, rows, pr, vp, m, p in soft_b:
            one = jnp.ones_like(vp)
            top = _dot(p[:B_Q_BLOCK], jnp.where(low_key_b, vp, one))
            bot = _dot(p[B_Q_BLOCK:], jnp.where(low_key_b, one, vp))
            denom = pltpu.roll(jnp.where(low_head, bot, top), HEAD_DIM, 1)
            ob_ref[grp, 0, pr, rows, :] = jnp.where(low_head, top, bot) / denom
            ob_ref[grp, 1, pr, rows, :] = jnp.where(low_head, m[:B_Q_BLOCK], m[B_Q_BLOCK:]) + jnp.log2(denom)

    n_groups = t // A_Q_BLOCK
    assert step_rows == t and n_groups == dil1 == t // B_Q_BLOCK
    gate_chunk = 2 * D_MODEL // n_groups
    x = x_ref[0]
    h = _rms_norm(x, g_pre_ref[...]).astype(BF16)

    def gate_logits(j):
        cols = slice(j * gate_chunk, (j + 1) * gate_chunk)
        return _dot(h, w_gate_ref[:, cols]) + b_gate_ref[:, cols]

    z = [None] * n_groups
    work = scores(0)
    z[0] = gate_logits(0)
    soft = softmaxes(work)
    work = scores(1)
    values(soft)
    z[1] = gate_logits(1)
    soft = softmaxes(work)
    work = scores(2)
    values(soft)
    soft = softmaxes(work)
    work = scores(3)
    values(soft)
    z[2] = gate_logits(2)
    z[3] = gate_logits(3)
    values(softmaxes(work))

    yb = []
    for c in range(B_GROUP_COLS // LANES):
        l0, l1, l2 = ob_ref[0, 1, c], ob_ref[1, 1, c], l2_ref[0, c]
        m = jnp.maximum(jnp.maximum(l0, l1), l2)
        e0, e1, e2 = jnp.exp2(l0 - m), jnp.exp2(l1 - m), jnp.exp2(l2 - m)
        num = e0 * ob_ref[0, 0, c] + e1 * ob_ref[1, 0, c] + e2 * o2_ref[0, c]
        yb.append((num / (e0 + e1 + e2)).astype(BF16))
    yb = jnp.concatenate(yb, axis=-1)
    gates = jax.nn.sigmoid(jnp.concatenate(z, axis=-1))
    merged = (gates[:, :D_MODEL] * _dot(ya_ref[...], w_a_ref[...])
              + gates[:, D_MODEL:] * _dot(yb, w_b_ref[...]))
    mix = _dot(merged.astype(BF16), w_out_ref[...])
    out_ref[0] = x + _rms_norm(mix, g_post_ref[...])


def _halo_specs(lead_blocks, rows, halo, total_rows, cols):
    nlead = len(lead_blocks)
    per = rows // halo
    nhalo = total_rows // halo
    zeros = (0,) * nlead
    main = pl.BlockSpec((1,) + lead_blocks + (rows, cols), lambda bi, i: (bi,) + zeros + (i, 0))
    prev = pl.BlockSpec((1,) + lead_blocks + (halo, cols),
                        lambda bi, i: (bi,) + zeros + (jnp.maximum(i * per - 1, 0), 0))
    nxt = pl.BlockSpec((1,) + lead_blocks + (halo, cols),
                       lambda bi, i: (bi,) + zeros + (jnp.minimum((i + 1) * per, nhalo - 1), 0))
    return [main, prev, nxt]


def _mix_attn(x, sink, bias_a, bias_b, qa, ka, va, qkv_b0, qkv_b1, ob2,
              g_pre, w_gate, b_gate, w_a, w_b, w_out, g_post):
    b, s, _ = x.shape
    t = ROW_TILE
    dil1 = B_PATTERNS[1][1]
    nslab = B_GROUP_COLS // LANES
    row_map = lambda bi, i: (bi, i, 0)
    slab_spec = pl.BlockSpec((1, nslab, t, LANES), lambda bi, i: (bi, 0, i, 0))
    qb0, kb0, vb0 = qkv_b0
    qb1, kb1, vb1 = qkv_b1
    kv_a = _halo_specs((), t, A_Q_BLOCK, s, 2 * A_KV_COLS)
    kv_b0 = _halo_specs((1,), t, B_HALF_WINDOW, s, B_GROUP_COLS)
    kv_b1 = _halo_specs((dil1,), t // dil1, B_HALF_WINDOW, s // dil1, B_GROUP_COLS)
    in_specs = ([pl.BlockSpec(memory_space=pltpu.SMEM), _const_spec(bias_a.shape), _const_spec(bias_b.shape),
                 pl.BlockSpec((1, t, D_MODEL), row_map),
                 pl.BlockSpec((1, t, A_Q_COLS), row_map)] + kv_a + kv_a
                + [kv_b0[0]] + kv_b0 + kv_b0
                + [kv_b1[0]] + kv_b1 + kv_b1
                + [slab_spec, slab_spec,
                   _const_spec((1, D_MODEL)),
                   _const_spec((D_MODEL, 2 * D_MODEL)),
                   _const_spec((1, 2 * D_MODEL)),
                   _const_spec((A_Q_COLS, D_MODEL)),
                   _const_spec((B_GROUP_COLS, D_MODEL)),
                   _const_spec((D_MODEL, D_MODEL)),
                   _const_spec((1, D_MODEL))])
    return pl.pallas_call(
        _mix_attn_kernel,
        grid=(b, s // t),
        in_specs=in_specs,
        out_specs=pl.BlockSpec((1, t, D_MODEL), row_map),
        out_shape=jax.ShapeDtypeStruct((b, s, D_MODEL), F32),
        scratch_shapes=[
            pltpu.VMEM((t, A_Q_COLS), BF16),
            pltpu.VMEM((2, 2, nslab, t, LANES), F32),
        ],
        compiler_params=pltpu.CompilerParams(
            dimension_semantics=("parallel", "parallel"), vmem_limit_bytes=VMEM_LIMIT_BYTES),
        name="mix_attn",
    )(sink, bias_a, bias_b, x, qa, ka, ka, ka, va, va, va,
      qb0, kb0, kb0, kb0, vb0, vb0, vb0, qb1, kb1, kb1, kb1, vb1, vb1, vb1,
      ob2[0], ob2[1], g_pre, w_gate, b_gate, w_a, w_b, w_out, g_post)


def _gelu_tanh(x):
    c = -2.0 * math.sqrt(2.0 / math.pi)
    return x / (1.0 + jnp.exp(x * (c + (c * 0.044715) * (x * x))))


def _conv_ffn_kernel(x_ref, prev_ref, next_ref, g_pre_ref, w_up_ref, conv_w_ref, conv_b_ref,
                     w_down_ref, g_post_ref, out_ref, u_ref, acc_ref, perm_ref):
    t = ROW_TILE
    half = t // 2
    i = pl.program_id(1)
    last = pl.num_programs(1) - 1
    row = lax.broadcasted_iota(jnp.int32, (t + 2 * HALO, 1), 0)
    starts = [sum(FF_CHUNKS[:c]) for c in range(len(FF_CHUNKS))]

    def normed(sub):
        lo, hi = sub * t, (sub + 1) * t
        before = prev_ref[0] if sub == 0 else x_ref[0, lo - HALO:lo, :]
        after = next_ref[0] if sub == FFN_SUB_TILES - 1 else x_ref[0, hi:hi + HALO, :]
        xe = jnp.concatenate([before, x_ref[0, lo:hi, :], after], axis=0)
        hn = _rms_norm(xe, g_pre_ref[...])
        if sub == 0:
            hn = jnp.where(jnp.logical_or(row >= HALO, i > 0), hn, 0.0)
        if sub == FFN_SUB_TILES - 1:
            hn = jnp.where(jnp.logical_or(row < HALO + t, i < last), hn, 0.0)
        return hn.astype(BF16)

    def col_starts(c):
        return (starts[c], D_FF + starts[c])

    def project(h, c, slot):
        for part, col0 in enumerate(col_starts(c)):
            up = _dot(h, w_up_ref[:, col0:col0 + FF_CHUNKS[c]])
            for j in range(FF_CHUNKS[c] // LANES):
                u_ref[2 * slot + part, j] = up[:, j * LANES:(j + 1) * LANES]

    def conv(c, slot, part):
        buf, col0 = 2 * slot + part, col_starts(c)[part]
        pieces = []
        for j in range(FF_CHUNKS[c] // LANES):
            cols = slice(col0 + j * LANES, col0 + (j + 1) * LANES)
            w0, w1, w2 = conv_w_ref[0:1, cols], conv_w_ref[1:2, cols], conv_w_ref[2:3, cols]
            bias = conv_b_ref[:, cols]
            r = [u_ref[buf, j, pl.ds(HALO - 1 + k, half, stride=2), :] for k in range(4)]
            even = r[0] * w0 + r[1] * w1 + r[2] * w2 + bias
            odd = r[1] * w0 + r[2] * w1 + r[3] * w2 + bias
            pieces.append(jnp.concatenate([even, odd], axis=0))
        return jnp.concatenate(pieces, axis=-1)

    def finish(sub):
        y = _rms_norm(acc_ref[sub], g_post_ref[...])
        for j in range(D_MODEL // LANES):
            perm_ref[j, pl.ds(0, half, stride=2), :] = y[:half, j * LANES:(j + 1) * LANES]
            perm_ref[j, pl.ds(1, half, stride=2), :] = y[half:, j * LANES:(j + 1) * LANES]
        y_nat = jnp.concatenate([perm_ref[j] for j in range(D_MODEL // LANES)], axis=-1)
        rows = slice(sub * t, (sub + 1) * t)
        out_ref[0, rows, :] = x_ref[0, rows, :] + y_nat

    items = [(sub, c) for sub in range(FFN_SUB_TILES) for c in range(len(FF_CHUNKS))]
    hs = {}

    def project_item(k):
        nsub, nc = items[k]
        if nsub not in hs:
            hs[nsub] = normed(nsub)
        project(hs[nsub], nc, k % (FF_LOOKAHEAD + 1))

    for k in range(min(FF_LOOKAHEAD, len(items))):
        project_item(k)
    for k, (sub, c) in enumerate(items):
        if k + FF_LOOKAHEAD < len(items):
            project_item(k + FF_LOOKAHEAD)
        slot = k % (FF_LOOKAHEAD + 1)
        act = (_gelu_tanh(conv(c, slot, 0)) * conv(c, slot, 1)).astype(BF16)
        part = _dot(act, w_down_ref[starts[c]:starts[c] + FF_CHUNKS[c], :])
        if c == 0:
            acc_ref[sub] = part
        else:
            acc_ref[sub] += part
        if c == len(FF_CHUNKS) - 1:
            finish(sub)


def _conv_ffn(x, g_pre, w_up, conv_w, conv_b, w_down, g_post):
    b, s, _ = x.shape
    t = FFN_SUB_TILES * ROW_TILE
    tiles = s // t
    halo_blocks_per_tile = t // HALO
    n_halo_blocks = s // HALO
    row_map = lambda bi, i: (bi, i, 0)
    prev_map = lambda bi, i: (bi, jnp.maximum(i * halo_blocks_per_tile - 1, 0), 0)
    next_map = lambda bi, i: (bi, jnp.minimum((i + 1) * halo_blocks_per_tile, n_halo_blocks - 1), 0)
    return pl.pallas_call(
        _conv_ffn_kernel,
        grid=(b, tiles),
        in_specs=[
            pl.BlockSpec((1, t, D_MODEL), row_map),
            pl.BlockSpec((1, HALO, D_MODEL), prev_map),
            pl.BlockSpec((1, HALO, D_MODEL), next_map),
            _const_spec((1, D_MODEL)),
            _const_spec((D_MODEL, 2 * D_FF)),
            _const_spec((3, 2 * D_FF)),
            _const_spec((1, 2 * D_FF)),
            _const_spec((D_FF, D_MODEL)),
            _const_spec((1, D_MODEL)),
        ],
        out_specs=pl.BlockSpec((1, t, D_MODEL), row_map),
        out_shape=jax.ShapeDtypeStruct((b, s, D_MODEL), F32),
        scratch_shapes=[
            pltpu.VMEM((2 * (FF_LOOKAHEAD + 1), FF_CHUNK // LANES, ROW_TILE + 2 * HALO, LANES), F32),
            pltpu.VMEM((FFN_SUB_TILES, ROW_TILE, D_MODEL), F32),
            pltpu.VMEM((D_MODEL // LANES, ROW_TILE, LANES), F32),
        ],
        compiler_params=pltpu.CompilerParams(
            dimension_semantics=("parallel", "parallel"), vmem_limit_bytes=VMEM_LIMIT_BYTES),
        name="conv_ffn",
    )(x, x, x, g_pre, w_up, conv_w, conv_b, w_down, g_post)


def _rope_tables(seq_len):
    half = HEAD_DIM // 2
    inv = ROPE_THETA ** (-jnp.arange(half, dtype=F32) / half)
    ang = jnp.arange(seq_len, dtype=F32)[:, None] * inv[None, :]
    cos, sin = jnp.cos(ang), jnp.sin(ang)
    cos_head = jnp.concatenate([cos, cos], axis=-1)
    sin_head = jnp.concatenate([-sin, sin], axis=-1)
    reps = LANES // HEAD_DIM
    return jnp.tile(cos_head, (1, reps)), jnp.tile(sin_head, (1, reps))


def _band_bias(q_rows, stacked, keys, half_window, lead):
    i = (jnp.arange(stacked * q_rows, dtype=jnp.int32) % q_rows)[None, :, None]
    j = jnp.arange(keys, dtype=jnp.int32)[None, None, :]
    e = jnp.arange(3, dtype=jnp.int32)[:, None, None]
    band = jnp.abs(j - lead - i) <= half_window
    inside = jnp.logical_and(jnp.logical_or(e != 0, j >= lead), jnp.logical_or(e != 2, j < lead + q_rows))
    return jnp.where(jnp.logical_and(band, inside), 0.0, NEG_INF).astype(F32)


def _far_band_bias():
    off = jnp.arange(3, dtype=jnp.int32)[:, None, None] * B_HALF_WINDOW
    i = (jnp.arange(2 * B_Q_BLOCK, dtype=jnp.int32) % B_Q_BLOCK)[None, :, None]
    j = jnp.arange(B_KEY_WINDOW, dtype=jnp.int32)[None, None, :]
    return jnp.where(jnp.abs(j - off - i) <= B_HALF_WINDOW, 0.0, NEG_INF).astype(F32)


def kernel(x, norm_mix_pre, w_in, sink, w_branch_a, w_branch_b, w_gate, b_gate, w_out,
           norm_mix_post, norm_ffn_pre, w_up, conv_w, conv_b, w_down, norm_ffn_post):
    b, s, d = x.shape
    assert d == D_MODEL and s % B_SPAN == 0 and s % (SUB_TILES * ROW_TILE) == 0
    cos, sin_signed = _rope_tables(s)
    bias_a = _band_bias(A_Q_BLOCK, 1, A_KEY_WINDOW, A_HALF_WINDOW, A_Q_BLOCK)
    bias_b = _band_bias(B_Q_BLOCK, 2, B_KEY_WINDOW, B_HALF_WINDOW, B_HALF_WINDOW)
    bias_b_far = _far_band_bias()
    for layer in range(norm_mix_pre.shape[0]):
        proj = _in_proj(x, norm_mix_pre[layer][None], w_in[layer].astype(BF16), cos, sin_signed)
        qa, ka, va = proj[:3]
        ob2 = _attn_b(bias_b_far, *proj[9:12], B_PATTERNS[2][1])
        x = _mix_attn(x, sink[layer], bias_a, bias_b, qa, ka, va, proj[3:6], proj[6:9], ob2,
                      norm_mix_pre[layer][None],
                      w_gate[layer].astype(BF16), b_gate[layer][None],
                      w_branch_a[layer].astype(BF16), w_branch_b[layer].astype(BF16),
                      w_out[layer].astype(BF16), norm_mix_post[layer][None])
        x = _conv_ffn(x, norm_ffn_pre[layer][None],
                      w_up[layer].astype(BF16), conv_w[layer], conv_b[layer][None],
                      w_down[layer].astype(BF16),
                      norm_ffn_post[layer][None])
    return x
```

```python
import functools
import math

import jax
import jax.numpy as jnp
from jax import lax
from jax.experimental import pallas as pl
from jax.experimental.pallas import tpu as pltpu

D_MODEL = 1024
HEAD_DIM = 64
A_Q_HEADS = 8
A_KV_HEADS = 2
A_GROUP = A_Q_HEADS // A_KV_HEADS
A_HALF_WINDOW = 128
B_PATTERNS = ((128, 1), (512, 4), (2048, 16))
B_N_GROUPS = len(B_PATTERNS)
B_HEADS = 4
HEAD_ORDER = (0, 2, 1, 3)
B_HALF_WINDOW = 64
ROPE_THETA = 10000.0
D_FF = 3 * D_MODEL
RMS_EPS = 1e-6
NEG_INF = -1e30
LOG2_E = math.log2(math.e)

A_Q_COLS = A_Q_HEADS * HEAD_DIM
A_KV_COLS = A_KV_HEADS * HEAD_DIM
A_COLS = A_Q_COLS + 2 * A_KV_COLS
B_GROUP_COLS = B_HEADS * HEAD_DIM
B_PROJ_COLS = B_N_GROUPS * B_GROUP_COLS
IN_COLS = A_COLS + 3 * B_PROJ_COLS
A_KVA_COLS = 4 * A_KV_COLS
B_QKV_COLS = 3 * B_GROUP_COLS

LANES = 128
VMEM_LIMIT_BYTES = 56 * 1024 * 1024

ROW_TILE = 512
SUB_TILES = 2
FFN_SUB_TILES = 2
A_Q_BLOCK = 128
A_KEY_WINDOW = 3 * A_Q_BLOCK
B_Q_BLOCK = 128
B_KEY_WINDOW = B_Q_BLOCK + 2 * B_HALF_WINDOW
B_SPAN = 2048
B_UNROLL = 8
FF_CHUNK = 512
FF_CHUNKS = (FF_CHUNK,) * (D_FF // FF_CHUNK)
FF_LOOKAHEAD = 2
HALO = 8

BF16 = jnp.bfloat16
F32 = jnp.float32


def _dot(a, b):
    return jnp.dot(a, b, preferred_element_type=F32)


def _dot_nt(a, b):
    return lax.dot_general(a, b, (((1,), (1,)), ((), ())), preferred_element_type=F32)


def _rms_norm(x, gain):
    ms = jnp.mean(x * x, axis=-1, keepdims=True)
    return x * lax.rsqrt(ms + RMS_EPS) * gain


def _rope_chunk(p, cos, sin_signed, first_half):
    partner = jnp.where(first_half, pltpu.roll(p, 96, 1), pltpu.roll(p, 32, 1))
    return p * cos + partner * sin_signed


def _const_spec(shape):
    nd = len(shape)
    return pl.BlockSpec(shape, lambda *_: (0,) * nd, pipeline_mode=pl.Buffered(1))


def _in_proj_kernel(x_ref, gain_ref, w_ref, cos_ref, sin_ref,
                    qa_ref, kva_ref, qkv0_ref, qkv1_ref, qkv2_ref, slab_ref):
    t = ROW_TILE
    lane = lax.broadcasted_iota(jnp.int32, (t, LANES), 1)
    first_half = (lane % HEAD_DIM) < (HEAD_DIM // 2)
    low_head = lane < HEAD_DIM
    scale = HEAD_DIM ** -0.5 * LOG2_E
    out_refs = (qkv0_ref, qkv1_ref, qkv2_ref)

    def tile(sub):
        rows = slice(sub * t, (sub + 1) * t)
        h = _rms_norm(x_ref[0, rows, :], gain_ref[...]).astype(BF16)
        cos = cos_ref[rows, :]
        sin_signed = sin_ref[rows, :]

        def proj(col0, ncols):
            return _dot(h, w_ref[:, col0:col0 + ncols])

        def rope(p, mult=None):
            chunks = []
            for c in range(p.shape[1] // LANES):
                r = _rope_chunk(p[:, c * LANES:(c + 1) * LANES], cos, sin_signed, first_half)
                chunks.append(r if mult is None else r * mult)
            return chunks

        def dup_heads(p):
            swapped = pltpu.roll(p, HEAD_DIM, 1)
            return [jnp.where(low_head, p, swapped), jnp.where(low_head, swapped, p)]

        def windowed():
            for c, chunk in enumerate(rope(proj(0, A_Q_COLS), scale)):
                qa_ref[0, rows, c * LANES:(c + 1) * LANES] = chunk.astype(BF16)
            kv = proj(A_Q_COLS, 2 * A_KV_COLS)
            ka = rope(kv[:, :A_KV_COLS])[0]
            for c, chunk in enumerate(dup_heads(ka) + dup_heads(kv[:, A_KV_COLS:])):
                kva_ref[0, rows, c * LANES:(c + 1) * LANES] = chunk.astype(BF16)

        def dilated(g):
            dil = B_PATTERNS[g][1]
            for kind in range(3):
                col0 = A_COLS + kind * B_PROJ_COLS + g * B_GROUP_COLS
                p = proj(col0, B_GROUP_COLS)
                if kind == 0:
                    chunks = rope(p, scale)
                elif kind == 1:
                    chunks = rope(p)
                else:
                    chunks = [p[:, c * LANES:(c + 1) * LANES] for c in range(B_GROUP_COLS // LANES)]
                o_ref = out_refs[g]
                sub_rows = slice(sub * t // dil, (sub + 1) * t // dil)
                col = lambda c: slice(kind * B_GROUP_COLS + c * LANES, kind * B_GROUP_COLS + (c + 1) * LANES)
                if dil == 1:
                    for c, chunk in enumerate(chunks):
                        o_ref[0, 0, sub_rows, col(c)] = chunk.astype(BF16)
                else:
                    for c, chunk in enumerate(chunks):
                        slab_ref[sub, kind, c] = chunk
                    for c in range(len(chunks)):
                        for res in range(dil):
                            picked = slab_ref[sub, kind, c, pl.ds(res, t // dil, stride=dil), :]
                            o_ref[0, res, sub_rows, col(c)] = picked.astype(BF16)

        dilated(2)
        dilated(1)
        windowed()
        dilated(0)

    for sub in range(SUB_TILES):
        tile(sub)


def _in_proj(x, gain, w_in, cos, sin_signed):
    b, s, _ = x.shape
    t = SUB_TILES * ROW_TILE
    grid = (b, s // t)
    row_map = lambda bi, i: (bi, i, 0)
    out_shape = [
        jax.ShapeDtypeStruct((b, s, A_Q_COLS), BF16),
        jax.ShapeDtypeStruct((b, s, A_KVA_COLS), BF16),
    ]
    out_specs = [
        pl.BlockSpec((1, t, A_Q_COLS), row_map),
        pl.BlockSpec((1, t, A_KVA_COLS), row_map),
    ]
    for _, dil in B_PATTERNS:
        out_shape.append(jax.ShapeDtypeStruct((b, dil, s // dil, B_QKV_COLS), BF16))
        out_specs.append(pl.BlockSpec((1, dil, t // dil, B_QKV_COLS), lambda bi, i: (bi, 0, i, 0)))
    return pl.pallas_call(
        _in_proj_kernel,
        grid=grid,
        in_specs=[
            pl.BlockSpec((1, t, D_MODEL), row_map),
            _const_spec((1, D_MODEL)),
            _const_spec((D_MODEL, IN_COLS)),
            pl.BlockSpec((t, LANES), lambda bi, i: (i, 0)),
            pl.BlockSpec((t, LANES), lambda bi, i: (i, 0)),
        ],
        out_specs=out_specs,
        out_shape=out_shape,
        scratch_shapes=[pltpu.VMEM((SUB_TILES, 3, B_GROUP_COLS // LANES, ROW_TILE, LANES), F32)],
        compiler_params=pltpu.CompilerParams(
            dimension_semantics=("parallel", "parallel"), vmem_limit_bytes=VMEM_LIMIT_BYTES),
        name="in_proj",
    )(x, gain, w_in, cos, sin_signed)


def _attn_b_kernel(bias_ref, q_ref, kv_ref, o_ref, lse_ref, *, dil):
    sub_len = kv_ref.shape[2]
    sub_rows = q_ref.shape[2]
    blocks = sub_rows // B_Q_BLOCK
    step = pl.program_id(1)
    low_head = lax.broadcasted_iota(jnp.int32, (B_Q_BLOCK, LANES), 1) < HEAD_DIM
    low_key = lax.broadcasted_iota(jnp.int32, (B_KEY_WINDOW, LANES), 1) < HEAD_DIM

    def units(it, carry):
        work = []
        for uu in range(B_UNROLL):
            u = it * B_UNROLL + uu
            res = u // blocks
            r0 = (u % blocks) * B_Q_BLOCK
            q_pos = (step * sub_rows + r0) // B_HALF_WINDOW
            start_pos = jnp.clip(q_pos - 1, 0, (sub_len - B_KEY_WINDOW) // B_HALF_WINDOW)
            start = start_pos * B_HALF_WINDOW
            bias = bias_ref[q_pos - start_pos]
            q_blk = q_ref[0, res, pl.ds(r0, B_Q_BLOCK), 0:B_GROUP_COLS]
            k_win = kv_ref[0, res, pl.ds(start, B_KEY_WINDOW), B_GROUP_COLS:2 * B_GROUP_COLS]
            v_win = kv_ref[0, res, pl.ds(start, B_KEY_WINDOW), 2 * B_GROUP_COLS:3 * B_GROUP_COLS]
            if dil == 1:
                rows = pl.ds(r0, B_Q_BLOCK)
            else:
                rows = pl.ds(r0 * dil + res, B_Q_BLOCK, stride=dil)
            for pr in range(B_HEADS // 2):
                qp = q_blk[:, pr * LANES:(pr + 1) * LANES]
                zero = jnp.zeros_like(qp)
                lhs = jnp.concatenate([jnp.where(low_head, qp, zero), jnp.where(low_head, zero, qp)], axis=0)
                sc = _dot_nt(lhs, k_win[:, pr * LANES:(pr + 1) * LANES])
                work.append((rows, pr, bias, v_win[:, pr * LANES:(pr + 1) * LANES], sc))

        soft = []
        for rows, pr, bias, vp, sc in work:
            sc = sc + bias
            m = jnp.max(sc, axis=-1, keepdims=True)
            soft.append((rows, pr, vp, m, jnp.exp2(sc - m).astype(BF16)))

        for rows, pr, vp, m, p in soft:
            one = jnp.ones_like(vp)
            top = _dot(p[:B_Q_BLOCK], jnp.where(low_key, vp, one))
            bot = _dot(p[B_Q_BLOCK:], jnp.where(low_key, one, vp))
            denom = pltpu.roll(jnp.where(low_head, bot, top), HEAD_DIM, 1)
            o_ref[0, pr, rows, :] = jnp.where(low_head, top, bot) / denom
            lse_ref[0, pr, rows, :] = jnp.where(low_head, m[:B_Q_BLOCK], m[B_Q_BLOCK:]) + jnp.log2(denom)
        return carry

    lax.fori_loop(0, dil * blocks // B_UNROLL, units, 0)


def _attn_b(bias, qkv, dil):
    b, _, sub_len, _ = qkv.shape
    s = sub_len * dil
    sub_rows = B_SPAN // dil
    nslab = B_GROUP_COLS // LANES
    out_sds = jax.ShapeDtypeStruct((b, nslab, s, LANES), F32)
    out_spec = pl.BlockSpec((1, nslab, B_SPAN, LANES), lambda bi, i: (bi, 0, i, 0))
    return pl.pallas_call(
        functools.partial(_attn_b_kernel, dil=dil),
        grid=(b, s // B_SPAN),
        in_specs=[
            _const_spec(bias.shape),
            pl.BlockSpec((1, dil, sub_rows, B_QKV_COLS), lambda bi, i: (bi, 0, i, 0)),
            pl.BlockSpec((1, dil, sub_len, B_QKV_COLS), lambda bi, i: (bi, 0, 0, 0)),
        ],
        out_specs=[out_spec, out_spec],
        out_shape=[out_sds, out_sds],
        compiler_params=pltpu.CompilerParams(
            dimension_semantics=("parallel", "parallel"), vmem_limit_bytes=VMEM_LIMIT_BYTES),
        name=f"attn_b_d{dil}",
    )(bias, qkv, qkv)


def _window(main_ref, prev_ref, next_ref, lead, lo, hi, length, cols):
    parts = []
    if lo < 0:
        halo = prev_ref.shape[-2]
        parts.append(prev_ref[lead + (slice(halo + lo, halo), cols)])
        lo = 0
    parts.append(main_ref[lead + (slice(lo, min(hi, length)), cols)])
    if hi > length:
        parts.append(next_ref[lead + (slice(0, hi - length), cols)])
    return parts[0] if len(parts) == 1 else jnp.concatenate(parts, axis=0)


def _mix_attn_kernel(sink_ref, bias_a_ref, bias_b_ref, x_ref,
                     qa_ref, kva_ref, kva_p_ref, kva_n_ref,
                     b0_ref, b0_p_ref, b0_n_ref, b1_ref, b1_p_ref, b1_n_ref,
                     o2_ref, l2_ref,
                     g_pre_ref, w_gate_ref, b_gate_ref, w_a_ref, w_b_ref, w_out_ref, g_post_ref,
                     out_ref, ya_ref, ob_ref):
    t = ROW_TILE
    step_rows = t
    i = pl.program_id(1)
    last = pl.num_programs(1) - 1
    low_head = lax.broadcasted_iota(jnp.int32, (A_Q_BLOCK, LANES), 1) < HEAD_DIM
    low_key_a = lax.broadcasted_iota(jnp.int32, (A_KEY_WINDOW, LANES), 1) < HEAD_DIM
    low_key_b = lax.broadcasted_iota(jnp.int32, (B_KEY_WINDOW, LANES), 1) < HEAD_DIM
    dil1 = B_PATTERNS[1][1]
    sub_rows = step_rows // dil1

    def edge_index(first, final):
        idx = 1
        if first:
            idx = jnp.where(i == 0, 0, idx)
        if final:
            idx = jnp.where(i == last, 2, idx)
        return idx

    def scores(grp_idx):
        work_a, work_b = [], []
        n = grp_idx
        lo, hi = (n - 1) * A_Q_BLOCK, (n + 2) * A_Q_BLOCK
        bias = bias_a_ref[edge_index(n == 0, hi > step_rows)]
        q_blk = qa_ref[0, n * A_Q_BLOCK:(n + 1) * A_Q_BLOCK, :]
        for j in range(A_KV_HEADS):
            kcols = slice(j * LANES, (j + 1) * LANES)
            vcols = slice(2 * A_KV_COLS + j * LANES, 2 * A_KV_COLS + (j + 1) * LANES)
            kd = _window(kva_ref, kva_p_ref, kva_n_ref, (0,), lo, hi, step_rows, kcols)
            vd = _window(kva_ref, kva_p_ref, kva_n_ref, (0,), lo, hi, step_rows, vcols)
            lhs = []
            for g in HEAD_ORDER:
                c0 = j * A_GROUP * HEAD_DIM + (g // 2) * LANES
                pair = q_blk[:, c0:c0 + LANES]
                keep = low_head if g % 2 == 0 else jnp.logical_not(low_head)
                lhs.append(jnp.where(keep, pair, jnp.zeros_like(pair)))
            sc = _dot_nt(jnp.concatenate(lhs, axis=0), kd)
            work_a.append((n, j, bias, vd, sc))
        units = [(0, (0, 0), grp_idx, step_rows), (1, (0, grp_idx), 0, sub_rows)]
        for grp, lead, n, length in units:
            m_ref, p_ref, n_ref = (b0_ref, b0_p_ref, b0_n_ref) if grp == 0 else (b1_ref, b1_p_ref, b1_n_ref)
            lo, hi = n * B_Q_BLOCK - B_HALF_WINDOW, (n + 1) * B_Q_BLOCK + B_HALF_WINDOW
            bias = bias_b_ref[edge_index(lo < 0, hi > length)]
            if grp == 0:
                rows = pl.ds(n * B_Q_BLOCK, B_Q_BLOCK)
            else:
                rows = pl.ds(n * B_Q_BLOCK * dil1 + lead[1], B_Q_BLOCK, stride=dil1)
            for pr in range(B_HEADS // 2):
                cols = [slice(kind * B_GROUP_COLS + pr * LANES, kind * B_GROUP_COLS + (pr + 1) * LANES)
                        for kind in range(3)]
                qp = m_ref[lead + (slice(n * B_Q_BLOCK, (n + 1) * B_Q_BLOCK), cols[0])]
                kp = _window(m_ref, p_ref, n_ref, lead, lo, hi, length, cols[1])
                vp = _window(m_ref, p_ref, n_ref, lead, lo, hi, length, cols[2])
                zero = jnp.zeros_like(qp)
                lhs = jnp.concatenate([jnp.where(low_head, qp, zero), jnp.where(low_head, zero, qp)], axis=0)
                work_b.append((grp, rows, pr, bias, vp, _dot_nt(lhs, kp)))
        return work_a, work_b

    def softmaxes(work):
        work_a, work_b = work
        soft_a, soft_b = [], []
        for n, j, bias, vd, sc in work_a:
            probs, sink_terms = {}, {}
            for slot, g in enumerate(HEAD_ORDER):
                sg = sc[slot * A_Q_BLOCK:(slot + 1) * A_Q_BLOCK] + bias
                sink = sink_ref[j * A_GROUP + g] * LOG2_E
                m = jnp.maximum(jnp.max(sg, axis=-1, keepdims=True), sink)
                probs[g] = jnp.exp2(sg - m).astype(BF16)
                sink_terms[g] = jnp.exp2(sink - m)
            soft_a.append((n, j, vd, probs, sink_terms))
        for grp, rows, pr, bias, vp, sc in work_b:
            sc = sc + bias
            m = jnp.max(sc, axis=-1, keepdims=True)
            soft_b.append((grp, rows, pr, vp, m, jnp.exp2(sc - m).astype(BF16)))
        return soft_a, soft_b

    def values(soft):
        soft_a, soft_b = soft
        pairs = range(A_GROUP // 2)
        for n, j, vd, probs, sink_terms in soft_a:
            one = jnp.ones_like(vd)
            r_even = _dot(jnp.concatenate([probs[2 * pr] for pr in pairs], axis=0),
                          jnp.where(low_key_a, vd, one))
            r_odd = _dot(jnp.concatenate([probs[2 * pr + 1] for pr in pairs], axis=0),
                         jnp.where(low_key_a, one, vd))
            for pr in pairs:
                even = r_even[pr * A_Q_BLOCK:(pr + 1) * A_Q_BLOCK]
                odd = r_odd[pr * A_Q_BLOCK:(pr + 1) * A_Q_BLOCK]
                num = jnp.where(low_head, even, odd)
                den = jnp.where(low_head, odd + sink_terms[2 * pr + 1], even + sink_terms[2 * pr])
                c0 = j * A_GROUP * HEAD_DIM + pr * LANES
                ya_ref[n * A_Q_BLOCK:(n + 1) * A_Q_BLOCK, c0:c0 + LANES] = (
                    num / pltpu.roll(den, HEAD_DIM, 1)).astype(BF16)
        for grp, rows, pr, vp, m, p in soft_b:
            one = jnp.ones_like(vp)
            top = _dot(p[:B_Q_BLOCK], jnp.where(low_key_b, vp, one))
            bot = _dot(p[B_Q_BLOCK:], jnp.where(low_key_b, one, vp))
            denom = pltpu.roll(jnp.where(low_head, bot, top), HEAD_DIM, 1)
            ob_ref[grp, 0, pr, rows, :] = jnp.where(low_head, top, bot) / denom
            ob_ref[grp, 1, pr, rows, :] = jnp.where(low_head, m[:B_Q_BLOCK], m[B_Q_BLOCK:]) + jnp.log2(denom)

    n_groups = t // A_Q_BLOCK
    assert step_rows == t and n_groups == dil1 == t // B_Q_BLOCK
    gate_chunk = 2 * D_MODEL // n_groups
    x = x_ref[0]
    h = _rms_norm(x, g_pre_ref[...]).astype(BF16)

    def gate_logits(j):
        cols = slice(j * gate_chunk, (j + 1) * gate_chunk)
        return _dot(h, w_gate_ref[:, cols]) + b_gate_ref[:, cols]

    z = [None] * n_groups
    work = scores(0)
    z[0] = gate_logits(0)
    soft = softmaxes(work)
    work = scores(1)
    values(soft)
    z[1] = gate_logits(1)
    soft = softmaxes(work)
    work = scores(2)
    values(soft)
    soft = softmaxes(work)
    work = scores(3)
    values(soft)
    z[2] = gate_logits(2)
    z[3] = gate_logits(3)
    values(softmaxes(work))

    yb = []
    for c in range(B_GROUP_COLS // LANES):
        l0, l1, l2 = ob_ref[0, 1, c], ob_ref[1, 1, c], l2_ref[0, c]
        m = jnp.maximum(jnp.maximum(l0, l1), l2)
        e0, e1, e2 = jnp.exp2(l0 - m), jnp.exp2(l1 - m), jnp.exp2(l2 - m)
        num = e0 * ob_ref[0, 0, c] + e1 * ob_ref[1, 0, c] + e2 * o2_ref[0, c]
        yb.append((num / (e0 + e1 + e2)).astype(BF16))
    yb = jnp.concatenate(yb, axis=-1)
    gates = jax.nn.sigmoid(jnp.concatenate(z, axis=-1))
    merged = (gates[:, :D_MODEL] * _dot(ya_ref[...], w_a_ref[...])
              + gates[:, D_MODEL:] * _dot(yb, w_b_ref[...]))
    mix = _dot(merged.astype(BF16), w_out_ref[...])
    out_ref[0] = x + _rms_norm(mix, g_post_ref[...])


def _halo_specs(lead_blocks, rows, halo, total_rows, cols):
    nlead = len(lead_blocks)
    per = rows // halo
    nhalo = total_rows // halo
    zeros = (0,) * nlead
    main = pl.BlockSpec((1,) + lead_blocks + (rows, cols), lambda bi, i: (bi,) + zeros + (i, 0))
    prev = pl.BlockSpec((1,) + lead_blocks + (halo, cols),
                        lambda bi, i: (bi,) + zeros + (jnp.maximum(i * per - 1, 0), 0))
    nxt = pl.BlockSpec((1,) + lead_blocks + (halo, cols),
                       lambda bi, i: (bi,) + zeros + (jnp.minimum((i + 1) * per, nhalo - 1), 0))
    return [main, prev, nxt]


def _mix_attn(x, sink, bias_a, bias_b, qa, kva, qkv_b0, qkv_b1, ob2,
              g_pre, w_gate, b_gate, w_a, w_b, w_out, g_post):
    b, s, _ = x.shape
    t = ROW_TILE
    dil1 = B_PATTERNS[1][1]
    nslab = B_GROUP_COLS // LANES
    row_map = lambda bi, i: (bi, i, 0)
    slab_spec = pl.BlockSpec((1, nslab, t, LANES), lambda bi, i: (bi, 0, i, 0))
    in_specs = ([pl.BlockSpec(memory_space=pltpu.SMEM), _const_spec(bias_a.shape), _const_spec(bias_b.shape),
                 pl.BlockSpec((1, t, D_MODEL), row_map),
                 pl.BlockSpec((1, t, A_Q_COLS), row_map)]
                + _halo_specs((), t, A_Q_BLOCK, s, A_KVA_COLS)
                + _halo_specs((1,), t, B_HALF_WINDOW, s, B_QKV_COLS)
                + _halo_specs((dil1,), t // dil1, B_HALF_WINDOW, s // dil1, B_QKV_COLS)
                + [slab_spec, slab_spec,
                   _const_spec((1, D_MODEL)),
                   _const_spec((D_MODEL, 2 * D_MODEL)),
                   _const_spec((1, 2 * D_MODEL)),
                   _const_spec((A_Q_COLS, D_MODEL)),
                   _const_spec((B_GROUP_COLS, D_MODEL)),
                   _const_spec((D_MODEL, D_MODEL)),
                   _const_spec((1, D_MODEL))])
    return pl.pallas_call(
        _mix_attn_kernel,
        grid=(b, s // t),
        in_specs=in_specs,
        out_specs=pl.BlockSpec((1, t, D_MODEL), row_map),
        out_shape=jax.ShapeDtypeStruct((b, s, D_MODEL), F32),
        scratch_shapes=[
            pltpu.VMEM((t, A_Q_COLS), BF16),
            pltpu.VMEM((2, 2, nslab, t, LANES), F32),
        ],
        compiler_params=pltpu.CompilerParams(
            dimension_semantics=("parallel", "parallel"), vmem_limit_bytes=VMEM_LIMIT_BYTES),
        name="mix_attn",
    )(sink, bias_a, bias_b, x, qa, kva, kva, kva, qkv_b0, qkv_b0, qkv_b0, qkv_b1, qkv_b1, qkv_b1,
      ob2[0], ob2[1], g_pre, w_gate, b_gate, w_a, w_b, w_out, g_post)


def _gelu_tanh(x):
    c = -2.0 * math.sqrt(2.0 / math.pi)
    return x / (1.0 + jnp.exp(x * (c + (c * 0.044715) * (x * x))))


def _conv_ffn_kernel(x_ref, prev_ref, next_ref, g_pre_ref, w_up_ref, conv_w_ref, conv_b_ref,
                     w_down_ref, g_post_ref, out_ref, u_ref, acc_ref, perm_ref):
    t = ROW_TILE
    half = t // 2
    i = pl.program_id(1)
    last = pl.num_programs(1) - 1
    row = lax.broadcasted_iota(jnp.int32, (t + 2 * HALO, 1), 0)
    starts = [sum(FF_CHUNKS[:c]) for c in range(len(FF_CHUNKS))]

    def col_starts(c):
        return (starts[c], D_FF + starts[c])

    def project(h, c, slot):
        for part, col0 in enumerate(col_starts(c)):
            up = _dot(h, w_up_ref[:, col0:col0 + FF_CHUNKS[c]])
            for j in range(FF_CHUNKS[c] // LANES):
                u_ref[2 * slot + part, j] = up[:, j * LANES:(j + 1) * LANES]

    def conv(c, slot, part):
        buf, col0 = 2 * slot + part, col_starts(c)[part]
        pieces = []
        for j in range(FF_CHUNKS[c] // LANES):
            cols = slice(col0 + j * LANES, col0 + (j + 1) * LANES)
            w0, w1, w2 = conv_w_ref[0:1, cols], conv_w_ref[1:2, cols], conv_w_ref[2:3, cols]
            bias = conv_b_ref[:, cols]
            r = [u_ref[buf, j, pl.ds(HALO - 1 + k, half, stride=2), :] for k in range(4)]
            even = r[0] * w0 + r[1] * w1 + r[2] * w2 + bias
            odd = r[1] * w0 + r[2] * w1 + r[3] * w2 + bias
            pieces.append(jnp.concatenate([even, odd], axis=0))
        return jnp.concatenate(pieces, axis=-1)

    def tile(sub, carry):
        lo = pl.multiple_of(sub * t, t)
        x = x_ref[0, pl.ds(lo, t), :]
        first, final = sub == 0, sub == FFN_SUB_TILES - 1
        inner_before = x_ref[0, pl.ds(pl.multiple_of(jnp.maximum(lo - HALO, 0), HALO), HALO), :]
        inner_after = x_ref[0, pl.ds(pl.multiple_of(jnp.minimum(lo + t, (FFN_SUB_TILES - 1) * t), HALO), HALO), :]
        before = jnp.where(first, prev_ref[0], inner_before)
        after = jnp.where(final, next_ref[0], inner_after)
        hn = _rms_norm(jnp.concatenate([before, x, after], axis=0), g_pre_ref[...])
        hn = jnp.where(jnp.logical_or(row >= HALO, jnp.logical_or(i > 0, sub > 0)), hn, 0.0)
        hn = jnp.where(jnp.logical_or(row < HALO + t, jnp.logical_or(i < last, sub < FFN_SUB_TILES - 1)), hn, 0.0)
        h = hn.astype(BF16)

        n_chunks = len(FF_CHUNKS)
        for c in range(min(FF_LOOKAHEAD, n_chunks)):
            project(h, c, c % (FF_LOOKAHEAD + 1))
        for c in range(n_chunks):
            if c + FF_LOOKAHEAD < n_chunks:
                project(h, c + FF_LOOKAHEAD, (c + FF_LOOKAHEAD) % (FF_LOOKAHEAD + 1))
            slot = c % (FF_LOOKAHEAD + 1)
            act = (_gelu_tanh(conv(c, slot, 0)) * conv(c, slot, 1)).astype(BF16)
            part = _dot(act, w_down_ref[starts[c]:starts[c] + FF_CHUNKS[c], :])
            if c == 0:
                acc_ref[...] = part
            else:
                acc_ref[...] += part
        y = _rms_norm(acc_ref[...], g_post_ref[...])
        for j in range(D_MODEL // LANES):
            perm_ref[j, pl.ds(0, half, stride=2), :] = y[:half, j * LANES:(j + 1) * LANES]
            perm_ref[j, pl.ds(1, half, stride=2), :] = y[half:, j * LANES:(j + 1) * LANES]
        y_nat = jnp.concatenate([perm_ref[j] for j in range(D_MODEL // LANES)], axis=-1)
        out_ref[0, pl.ds(lo, t), :] = x + y_nat
        return carry

    lax.fori_loop(0, FFN_SUB_TILES, tile, 0)


def _conv_ffn(x, g_pre, w_up, conv_w, conv_b, w_down, g_post):
    b, s, _ = x.shape
    t = FFN_SUB_TILES * ROW_TILE
    tiles = s // t
    halo_blocks_per_tile = t // HALO
    n_halo_blocks = s // HALO
    row_map = lambda bi, i: (bi, i, 0)
    prev_map = lambda bi, i: (bi, jnp.maximum(i * halo_blocks_per_tile - 1, 0), 0)
    next_map = lambda bi, i: (bi, jnp.minimum((i + 1) * halo_blocks_per_tile, n_halo_blocks - 1), 0)
    return pl.pallas_call(
        _conv_ffn_kernel,
        grid=(b, tiles),
        in_specs=[
            pl.BlockSpec((1, t, D_MODEL), row_map),
            pl.BlockSpec((1, HALO, D_MODEL), prev_map),
            pl.BlockSpec((1, HALO, D_MODEL), next_map),
            _const_spec((1, D_MODEL)),
            _const_spec((D_MODEL, 2 * D_FF)),
            _const_spec((3, 2 * D_FF)),
            _const_spec((1, 2 * D_FF)),
            _const_spec((D_FF, D_MODEL)),
            _const_spec((1, D_MODEL)),
        ],
        out_specs=pl.BlockSpec((1, t, D_MODEL), row_map),
        out_shape=jax.ShapeDtypeStruct((b, s, D_MODEL), F32),
        scratch_shapes=[
            pltpu.VMEM((2 * (FF_LOOKAHEAD + 1), FF_CHUNK // LANES, ROW_TILE + 2 * HALO, LANES), F32),
            pltpu.VMEM((ROW_TILE, D_MODEL), F32),
            pltpu.VMEM((D_MODEL // LANES, ROW_TILE, LANES), F32),
        ],
        compiler_params=pltpu.CompilerParams(
            dimension_semantics=("parallel", "parallel"), vmem_limit_bytes=VMEM_LIMIT_BYTES),
        name="conv_ffn",
    )(x, x, x, g_pre, w_up, conv_w, conv_b, w_down, g_post)


def _rope_tables(seq_len):
    half = HEAD_DIM // 2
    inv = ROPE_THETA ** (-jnp.arange(half, dtype=F32) / half)
    ang = jnp.arange(seq_len, dtype=F32)[:, None] * inv[None, :]
    cos, sin = jnp.cos(ang), jnp.sin(ang)
    cos_head = jnp.concatenate([cos, cos], axis=-1)
    sin_head = jnp.concatenate([-sin, sin], axis=-1)
    reps = LANES // HEAD_DIM
    return jnp.tile(cos_head, (1, reps)), jnp.tile(sin_head, (1, reps))


def _band_bias(q_rows, stacked, keys, half_window, lead):
    i = (jnp.arange(stacked * q_rows, dtype=jnp.int32) % q_rows)[None, :, None]
    j = jnp.arange(keys, dtype=jnp.int32)[None, None, :]
    e = jnp.arange(3, dtype=jnp.int32)[:, None, None]
    band = jnp.abs(j - lead - i) <= half_window
    inside = jnp.logical_and(jnp.logical_or(e != 0, j >= lead), jnp.logical_or(e != 2, j < lead + q_rows))
    return jnp.where(jnp.logical_and(band, inside), 0.0, NEG_INF).astype(F32)


def _far_band_bias():
    off = jnp.arange(3, dtype=jnp.int32)[:, None, None] * B_HALF_WINDOW
    i = (jnp.arange(2 * B_Q_BLOCK, dtype=jnp.int32) % B_Q_BLOCK)[None, :, None]
    j = jnp.arange(B_KEY_WINDOW, dtype=jnp.int32)[None, None, :]
    return jnp.where(jnp.abs(j - off - i) <= B_HALF_WINDOW, 0.0, NEG_INF).astype(F32)


def kernel(x, norm_mix_pre, w_in, sink, w_branch_a, w_branch_b, w_gate, b_gate, w_out,
           norm_mix_post, norm_ffn_pre, w_up, conv_w, conv_b, w_down, norm_ffn_post):
    b, s, d = x.shape
    assert d == D_MODEL and s % B_SPAN == 0 and s % (SUB_TILES * ROW_TILE) == 0
    cos, sin_signed = _rope_tables(s)
    bias_a = _band_bias(A_Q_BLOCK, 1, A_KEY_WINDOW, A_HALF_WINDOW, A_Q_BLOCK)
    bias_b = _band_bias(B_Q_BLOCK, 2, B_KEY_WINDOW, B_HALF_WINDOW, B_HALF_WINDOW)
    bias_b_far = _far_band_bias()
    for layer in range(norm_mix_pre.shape[0]):
        proj = _in_proj(x, norm_mix_pre[layer][None], w_in[layer].astype(BF16), cos, sin_signed)
        qa, kva, qkv_b0, qkv_b1, qkv_b2 = proj
        ob2 = _attn_b(bias_b_far, qkv_b2, B_PATTERNS[2][1])
        x = _mix_attn(x, sink[layer], bias_a, bias_b, qa, kva, qkv_b0, qkv_b1, ob2,
                      norm_mix_pre[layer][None],
                      w_gate[layer].astype(BF16), b_gate[layer][None],
                      w_branch_a[layer].astype(BF16), w_branch_b[layer].astype(BF16),
                      w_out[layer].astype(BF16), norm_mix_post[layer][None])
        x = _conv_ffn(x, norm_ffn_pre[layer][None],
                      w_up[layer].astype(BF16), conv_w[layer], conv_b[layer][None],
                      w_down[layer].astype(BF16),
                      norm_ffn_post[layer][None])
    return x
```

```python
import functools
import math

import jax
import jax.numpy as jnp
from jax import lax
from jax.experimental import pallas as pl
from jax.experimental.pallas import tpu as pltpu

D_MODEL = 1024
HEAD_DIM = 64
A_Q_HEADS = 8
A_KV_HEADS = 2
A_GROUP = A_Q_HEADS // A_KV_HEADS
A_HALF_WINDOW = 128
B_PATTERNS = ((128, 1), (512, 4), (2048, 16))
B_N_GROUPS = len(B_PATTERNS)
B_HEADS = 4
HEAD_ORDER = (0, 2, 1, 3)
B_HALF_WINDOW = 64
ROPE_THETA = 10000.0
D_FF = 3 * D_MODEL
RMS_EPS = 1e-6
NEG_INF = -1e30
LOG2_E = math.log2(math.e)

A_Q_COLS = A_Q_HEADS * HEAD_DIM
A_KV_COLS = A_KV_HEADS * HEAD_DIM
A_COLS = A_Q_COLS + 2 * A_KV_COLS
B_GROUP_COLS = B_HEADS * HEAD_DIM
B_PROJ_COLS = B_N_GROUPS * B_GROUP_COLS
IN_COLS = A_COLS + 3 * B_PROJ_COLS
A_KVA_COLS = 4 * A_KV_COLS
B_QKV_COLS = 3 * B_GROUP_COLS

LANES = 128
VMEM_LIMIT_BYTES = 56 * 1024 * 1024

ROW_TILE = 512
SUB_TILES = 2
FFN_SUB_TILES = 2
A_Q_BLOCK = 128
A_KEY_WINDOW = 3 * A_Q_BLOCK
B_Q_BLOCK = 128
B_KEY_WINDOW = B_Q_BLOCK + 2 * B_HALF_WINDOW
B_SPAN = 2048
B_UNROLL = 8
FF_CHUNK = 512
FF_CHUNKS = (FF_CHUNK,) * (D_FF // FF_CHUNK)
FF_LOOKAHEAD = 2
HALO = 8

BF16 = jnp.bfloat16
F32 = jnp.float32


def _dot(a, b):
    return jnp.dot(a, b, preferred_element_type=F32)


def _dot_nt(a, b):
    return lax.dot_general(a, b, (((1,), (1,)), ((), ())), preferred_element_type=F32)


def _rms_norm(x, gain):
    ms = jnp.mean(x * x, axis=-1, keepdims=True)
    return x * lax.rsqrt(ms + RMS_EPS) * gain


def _rope_chunk(p, cos, sin_signed, first_half):
    partner = jnp.where(first_half, pltpu.roll(p, 96, 1), pltpu.roll(p, 32, 1))
    return p * cos + partner * sin_signed


def _const_spec(shape):
    nd = len(shape)
    return pl.BlockSpec(shape, lambda *_: (0,) * nd, pipeline_mode=pl.Buffered(1))


def _in_proj_kernel(x_ref, gain_ref, w_ref, cos_ref, sin_ref,
                    qa_ref, kva_ref, qkv0_ref, qkv1_ref, qkv2_ref, slab_ref):
    t = ROW_TILE
    lane = lax.broadcasted_iota(jnp.int32, (t, LANES), 1)
    first_half = (lane % HEAD_DIM) < (HEAD_DIM // 2)
    low_head = lane < HEAD_DIM
    scale = HEAD_DIM ** -0.5 * LOG2_E
    out_refs = (qkv0_ref, qkv1_ref, qkv2_ref)

    def tile(sub):
        rows = slice(sub * t, (sub + 1) * t)
        h = _rms_norm(x_ref[0, rows, :], gain_ref[...]).astype(BF16)
        cos = cos_ref[rows, :]
        sin_signed = sin_ref[rows, :]

        def proj(col0, ncols):
            return _dot(h, w_ref[:, col0:col0 + ncols])

        def rope(p, mult=None):
            chunks = []
            for c in range(p.shape[1] // LANES):
                r = _rope_chunk(p[:, c * LANES:(c + 1) * LANES], cos, sin_signed, first_half)
                chunks.append(r if mult is None else r * mult)
            return chunks

        def dup_heads(p):
            swapped = pltpu.roll(p, HEAD_DIM, 1)
            return [jnp.where(low_head, p, swapped), jnp.where(low_head, swapped, p)]

        def windowed():
            for c, chunk in enumerate(rope(proj(0, A_Q_COLS), scale)):
                qa_ref[0, rows, c * LANES:(c + 1) * LANES] = chunk.astype(BF16)
            kv = proj(A_Q_COLS, 2 * A_KV_COLS)
            ka = rope(kv[:, :A_KV_COLS])[0]
            for c, chunk in enumerate(dup_heads(ka) + dup_heads(kv[:, A_KV_COLS:])):
                kva_ref[0, rows, c * LANES:(c + 1) * LANES] = chunk.astype(BF16)

        def dilated(g):
            dil = B_PATTERNS[g][1]
            for kind in range(3):
                col0 = A_COLS + kind * B_PROJ_COLS + g * B_GROUP_COLS
                p = proj(col0, B_GROUP_COLS)
                if kind == 0:
                    chunks = rope(p, scale)
                elif kind == 1:
                    chunks = rope(p)
                else:
                    chunks = [p[:, c * LANES:(c + 1) * LANES] for c in range(B_GROUP_COLS // LANES)]
                o_ref = out_refs[g]
                sub_rows = slice(sub * t // dil, (sub + 1) * t // dil)
                col = lambda c: slice(kind * B_GROUP_COLS + c * LANES, kind * B_GROUP_COLS + (c + 1) * LANES)
                if dil == 1:
                    for c, chunk in enumerate(chunks):
                        o_ref[0, 0, sub_rows, col(c)] = chunk.astype(BF16)
                else:
                    for c, chunk in enumerate(chunks):
                        slab_ref[sub, kind, c] = chunk
                    for c in range(len(chunks)):
                        for res in range(dil):
                            picked = slab_ref[sub, kind, c, pl.ds(res, t // dil, stride=dil), :]
                            o_ref[0, res, sub_rows, col(c)] = picked.astype(BF16)

        dilated(2)
        dilated(1)
        windowed()
        dilated(0)

    for sub in range(SUB_TILES):
        tile(sub)


def _in_proj(x, gain, w_in, cos, sin_signed):
    b, s, _ = x.shape
    t = SUB_TILES * ROW_TILE
    grid = (b, s // t)
    row_map = lambda bi, i: (bi, i, 0)
    out_shape = [
        jax.ShapeDtypeStruct((b, s, A_Q_COLS), BF16),
        jax.ShapeDtypeStruct((b, s, A_KVA_COLS), BF16),
    ]
    out_specs = [
        pl.BlockSpec((1, t, A_Q_COLS), row_map),
        pl.BlockSpec((1, t, A_KVA_COLS), row_map),
    ]
    for _, dil in B_PATTERNS:
        out_shape.append(jax.ShapeDtypeStruct((b, dil, s // dil, B_QKV_COLS), BF16))
        out_specs.append(pl.BlockSpec((1, dil, t // dil, B_QKV_COLS), lambda bi, i: (bi, 0, i, 0)))
    return pl.pallas_call(
        _in_proj_kernel,
        grid=grid,
        in_specs=[
            pl.BlockSpec((1, t, D_MODEL), row_map),
            _const_spec((1, D_MODEL)),
            _const_spec((D_MODEL, IN_COLS)),
            pl.BlockSpec((t, LANES), lambda bi, i: (i, 0)),
            pl.BlockSpec((t, LANES), lambda bi, i: (i, 0)),
        ],
        out_specs=out_specs,
        out_shape=out_shape,
        scratch_shapes=[pltpu.VMEM((SUB_TILES, 3, B_GROUP_COLS // LANES, ROW_TILE, LANES), F32)],
        compiler_params=pltpu.CompilerParams(
            dimension_semantics=("parallel", "parallel"), vmem_limit_bytes=VMEM_LIMIT_BYTES),
        name="in_proj",
    )(x, gain, w_in, cos, sin_signed)


def _attn_b_kernel(bias_ref, q_ref, kv_ref, *rest, dil, n_cast):
    o_ref, lse_ref = rest[n_cast:n_cast + 2]
    for src_ref, dst_ref in zip(rest[:n_cast], rest[n_cast + 2:]):
        dst_ref[...] = src_ref[...].astype(BF16)
    sub_len = kv_ref.shape[2]
    sub_rows = q_ref.shape[2]
    blocks = sub_rows // B_Q_BLOCK
    step = pl.program_id(1)
    low_head = lax.broadcasted_iota(jnp.int32, (B_Q_BLOCK, LANES), 1) < HEAD_DIM
    low_key = lax.broadcasted_iota(jnp.int32, (B_KEY_WINDOW, LANES), 1) < HEAD_DIM

    def units(it, carry):
        work = []
        for uu in range(B_UNROLL):
            u = it * B_UNROLL + uu
            res = u // blocks
            r0 = (u % blocks) * B_Q_BLOCK
            q_pos = (step * sub_rows + r0) // B_HALF_WINDOW
            start_pos = jnp.clip(q_pos - 1, 0, (sub_len - B_KEY_WINDOW) // B_HALF_WINDOW)
            start = start_pos * B_HALF_WINDOW
            bias = bias_ref[q_pos - start_pos]
            q_blk = q_ref[0, res, pl.ds(r0, B_Q_BLOCK), 0:B_GROUP_COLS]
            k_win = kv_ref[0, res, pl.ds(start, B_KEY_WINDOW), B_GROUP_COLS:2 * B_GROUP_COLS]
            v_win = kv_ref[0, res, pl.ds(start, B_KEY_WINDOW), 2 * B_GROUP_COLS:3 * B_GROUP_COLS]
            if dil == 1:
                rows = pl.ds(r0, B_Q_BLOCK)
            else:
                rows = pl.ds(r0 * dil + res, B_Q_BLOCK, stride=dil)
            for pr in range(B_HEADS // 2):
                qp = q_blk[:, pr * LANES:(pr + 1) * LANES]
                zero = jnp.zeros_like(qp)
                lhs = jnp.concatenate([jnp.where(low_head, qp, zero), jnp.where(low_head, zero, qp)], axis=0)
                sc = _dot_nt(lhs, k_win[:, pr * LANES:(pr + 1) * LANES])
                work.append((rows, pr, bias, v_win[:, pr * LANES:(pr + 1) * LANES], sc))

        soft = []
        for rows, pr, bias, vp, sc in work:
            sc = sc + bias
            m = jnp.max(sc, axis=-1, keepdims=True)
            soft.append((rows, pr, vp, m, jnp.exp2(sc - m).astype(BF16)))

        for rows, pr, vp, m, p in soft:
            one = jnp.ones_like(vp)
            top = _dot(p[:B_Q_BLOCK], jnp.where(low_key, vp, one))
            bot = _dot(p[B_Q_BLOCK:], jnp.where(low_key, one, vp))
            denom = pltpu.roll(jnp.where(low_head, bot, top), HEAD_DIM, 1)
            o_ref[0, pr, rows, :] = jnp.where(low_head, top, bot) / denom
            lse_ref[0, pr, rows, :] = jnp.where(low_head, m[:B_Q_BLOCK], m[B_Q_BLOCK:]) + jnp.log2(denom)
        return carry

    lax.fori_loop(0, dil * blocks // B_UNROLL, units, 0)


def _attn_b(bias, qkv, dil, cast_weights=()):
    b, _, sub_len, _ = qkv.shape
    s = sub_len * dil
    sub_rows = B_SPAN // dil
    nslab = B_GROUP_COLS // LANES
    out_sds = jax.ShapeDtypeStruct((b, nslab, s, LANES), F32)
    out_spec = pl.BlockSpec((1, nslab, B_SPAN, LANES), lambda bi, i: (bi, 0, i, 0))
    steps_per_seq = s // B_SPAN
    n_steps = b * steps_per_seq
    chunk_specs = [pl.BlockSpec((w.shape[0] // n_steps, w.shape[1]), lambda bi, i: (bi * steps_per_seq + i, 0))
                   for w in cast_weights]
    return pl.pallas_call(
        functools.partial(_attn_b_kernel, dil=dil, n_cast=len(cast_weights)),
        grid=(b, steps_per_seq),
        in_specs=[
            _const_spec(bias.shape),
            pl.BlockSpec((1, dil, sub_rows, B_QKV_COLS), lambda bi, i: (bi, 0, i, 0)),
            pl.BlockSpec((1, dil, sub_len, B_QKV_COLS), lambda bi, i: (bi, 0, 0, 0)),
        ] + chunk_specs,
        out_specs=[out_spec, out_spec] + chunk_specs,
        out_shape=[out_sds, out_sds] + [jax.ShapeDtypeStruct(w.shape, BF16) for w in cast_weights],
        compiler_params=pltpu.CompilerParams(
            dimension_semantics=("parallel", "parallel"), vmem_limit_bytes=VMEM_LIMIT_BYTES),
        name=f"attn_b_d{dil}",
    )(bias, qkv, qkv, *cast_weights)


def _window(main_ref, prev_ref, next_ref, lead, lo, hi, length, cols):
    parts = []
    if lo < 0:
        halo = prev_ref.shape[-2]
        parts.append(prev_ref[lead + (slice(halo + lo, halo), cols)])
        lo = 0
    parts.append(main_ref[lead + (slice(lo, min(hi, length)), cols)])
    if hi > length:
        parts.append(next_ref[lead + (slice(0, hi - length), cols)])
    return parts[0] if len(parts) == 1 else jnp.concatenate(parts, axis=0)


def _mix_attn_kernel(sink_ref, bias_a_ref, bias_b_ref, x_ref,
                     qa_ref, kva_ref, kva_p_ref, kva_n_ref,
                     b0_ref, b0_p_ref, b0_n_ref, b1_ref, b1_p_ref, b1_n_ref,
                     o2_ref, l2_ref,
                     g_pre_ref, w_gate_ref, b_gate_ref, w_a_ref, w_b_ref, w_out_ref, g_post_ref,
                     out_ref, ya_ref, ob_ref):
    t = ROW_TILE
    step_rows = t
    i = pl.program_id(1)
    last = pl.num_programs(1) - 1
    low_head = lax.broadcasted_iota(jnp.int32, (A_Q_BLOCK, LANES), 1) < HEAD_DIM
    low_key_a = lax.broadcasted_iota(jnp.int32, (A_KEY_WINDOW, LANES), 1) < HEAD_DIM
    low_key_b = lax.broadcasted_iota(jnp.int32, (B_KEY_WINDOW, LANES), 1) < HEAD_DIM
    dil1 = B_PATTERNS[1][1]
    sub_rows = step_rows // dil1

    def edge_index(first, final):
        idx = 1
        if first:
            idx = jnp.where(i == 0, 0, idx)
        if final:
            idx = jnp.where(i == last, 2, idx)
        return idx

    def scores(grp_idx):
        work_a, work_b = [], []
        n = grp_idx
        lo, hi = (n - 1) * A_Q_BLOCK, (n + 2) * A_Q_BLOCK
        bias = bias_a_ref[edge_index(n == 0, hi > step_rows)]
        q_blk = qa_ref[0, n * A_Q_BLOCK:(n + 1) * A_Q_BLOCK, :]
        for j in range(A_KV_HEADS):
            kcols = slice(j * LANES, (j + 1) * LANES)
            vcols = slice(2 * A_KV_COLS + j * LANES, 2 * A_KV_COLS + (j + 1) * LANES)
            kd = _window(kva_ref, kva_p_ref, kva_n_ref, (0,), lo, hi, step_rows, kcols)
            vd = _window(kva_ref, kva_p_ref, kva_n_ref, (0,), lo, hi, step_rows, vcols)
            lhs = []
            for g in HEAD_ORDER:
                c0 = j * A_GROUP * HEAD_DIM + (g // 2) * LANES
                pair = q_blk[:, c0:c0 + LANES]
                keep = low_head if g % 2 == 0 else jnp.logical_not(low_head)
                lhs.append(jnp.where(keep, pair, jnp.zeros_like(pair)))
            sc = _dot_nt(jnp.concatenate(lhs, axis=0), kd)
            work_a.append((n, j, bias, vd, sc))
        units = [(0, (0, 0), grp_idx, step_rows), (1, (0, grp_idx), 0, sub_rows)]
        for grp, lead, n, length in units:
            m_ref, p_ref, n_ref = (b0_ref, b0_p_ref, b0_n_ref) if grp == 0 else (b1_ref, b1_p_ref, b1_n_ref)
            lo, hi = n * B_Q_BLOCK - B_HALF_WINDOW, (n + 1) * B_Q_BLOCK + B_HALF_WINDOW
            bias = bias_b_ref[edge_index(lo < 0, hi > length)]
            if grp == 0:
                rows = pl.ds(n * B_Q_BLOCK, B_Q_BLOCK)
            else:
                rows = pl.ds(n * B_Q_BLOCK * dil1 + lead[1], B_Q_BLOCK, stride=dil1)
            for pr in range(B_HEADS // 2):
                cols = [slice(kind * B_GROUP_COLS + pr * LANES, kind * B_GROUP_COLS + (pr + 1) * LANES)
                        for kind in range(3)]
                qp = m_ref[lead + (slice(n * B_Q_BLOCK, (n + 1) * B_Q_BLOCK), cols[0])]
                kp = _window(m_ref, p_ref, n_ref, lead, lo, hi, length, cols[1])
                vp = _window(m_ref, p_ref, n_ref, lead, lo, hi, length, cols[2])
                zero = jnp.zeros_like(qp)
                lhs = jnp.concatenate([jnp.where(low_head, qp, zero), jnp.where(low_head, zero, qp)], axis=0)
                work_b.append((grp, rows, pr, bias, vp, _dot_nt(lhs, kp)))
        return work_a, work_b

    def softmaxes(work):
        work_a, work_b = work
        soft_a, soft_b = [], []
        for n, j, bias, vd, sc in work_a:
            probs, sink_terms = {}, {}
            for slot, g in enumerate(HEAD_ORDER):
                sg = sc[slot * A_Q_BLOCK:(slot + 1) * A_Q_BLOCK] + bias
                sink = sink_ref[j * A_GROUP + g] * LOG2_E
                m = jnp.maximum(jnp.max(sg, axis=-1, keepdims=True), sink)
                probs[g] = jnp.exp2(sg - m).astype(BF16)
                sink_terms[g] = jnp.exp2(sink - m)
            soft_a.append((n, j, vd, probs, sink_terms))
        for grp, rows, pr, bias, vp, sc in work_b:
            sc = sc + bias
            m = jnp.max(sc, axis=-1, keepdims=True)
            soft_b.append((grp, rows, pr, vp, m, jnp.exp2(sc - m).astype(BF16)))
        return soft_a, soft_b

    def values(soft):
        soft_a, soft_b = soft
        pairs = range(A_GROUP // 2)
        for n, j, vd, probs, sink_terms in soft_a:
            one = jnp.ones_like(vd)
            r_even = _dot(jnp.concatenate([probs[2 * pr] for pr in pairs], axis=0),
                          jnp.where(low_key_a, vd, one))
            r_odd = _dot(jnp.concatenate([probs[2 * pr + 1] for pr in pairs], axis=0),
                         jnp.where(low_key_a, one, vd))
            for pr in pairs:
                even = r_even[pr * A_Q_BLOCK:(pr + 1) * A_Q_BLOCK]
                odd = r_odd[pr * A_Q_BLOCK:(pr + 1) * A_Q_BLOCK]
                num = jnp.where(low_head, even, odd)
                den = jnp.where(low_head, odd + sink_terms[2 * pr + 1], even + sink_terms[2 * pr])
                c0 = j * A_GROUP * HEAD_DIM + pr * LANES
                ya_ref[n * A_Q_BLOCK:(n + 1) * A_Q_BLOCK, c0:c0 + LANES] = (
                    num / pltpu.roll(den, HEAD_DIM, 1)).astype(BF16)
        for grp, rows, pr, vp, m, p in soft_b:
            one = jnp.ones_like(vp)
            top = _dot(p[:B_Q_BLOCK], jnp.where(low_key_b, vp, one))
            bot = _dot(p[B_Q_BLOCK:], jnp.where(low_key_b, one, vp))
            denom = pltpu.roll(jnp.where(low_head, bot, top), HEAD_DIM, 1)
            ob_ref[grp, 0, pr, rows, :] = jnp.where(low_head, top, bot) / denom
            ob_ref[grp, 1, pr, rows, :] = jnp.where(low_head, m[:B_Q_BLOCK], m[B_Q_BLOCK:]) + jnp.log2(denom)

    n_groups = t // A_Q_BLOCK
    assert step_rows == t and n_groups == dil1 == t // B_Q_BLOCK
    gate_chunk = 2 * D_MODEL // n_groups
    x = x_ref[0]
    h = _rms_norm(x, g_pre_ref[...]).astype(BF16)

    def gate_logits(j):
        cols = slice(j * gate_chunk, (j + 1) * gate_chunk)
        return _dot(h, w_gate_ref[:, cols]) + b_gate_ref[:, cols]

    z = [None] * n_groups
    work = scores(0)
    z[0] = gate_logits(0)
    soft = softmaxes(work)
    work = scores(1)
    values(soft)
    z[1] = gate_logits(1)
    soft = softmaxes(work)
    work = scores(2)
    values(soft)
    soft = softmaxes(work)
    work = scores(3)
    values(soft)
    z[2] = gate_logits(2)
    z[3] = gate_logits(3)
    values(softmaxes(work))

    yb = []
    for c in range(B_GROUP_COLS // LANES):
        l0, l1, l2 = ob_ref[0, 1, c], ob_ref[1, 1, c], l2_ref[0, c]
        m = jnp.maximum(jnp.maximum(l0, l1), l2)
        e0, e1, e2 = jnp.exp2(l0 - m), jnp.exp2(l1 - m), jnp.exp2(l2 - m)
        num = e0 * ob_ref[0, 0, c] + e1 * ob_ref[1, 0, c] + e2 * o2_ref[0, c]
        yb.append((num / (e0 + e1 + e2)).astype(BF16))
    yb = jnp.concatenate(yb, axis=-1)
    gates = jax.nn.sigmoid(jnp.concatenate(z, axis=-1))
    merged = (gates[:, :D_MODEL] * _dot(ya_ref[...], w_a_ref[...])
              + gates[:, D_MODEL:] * _dot(yb, w_b_ref[...]))
    mix = _dot(merged.astype(BF16), w_out_ref[...])
    out_ref[0] = x + _rms_norm(mix, g_post_ref[...])


def _halo_specs(lead_blocks, rows, halo, total_rows, cols):
    nlead = len(lead_blocks)
    per = rows // halo
    nhalo = total_rows // halo
    zeros = (0,) * nlead
    main = pl.BlockSpec((1,) + lead_blocks + (rows, cols), lambda bi, i: (bi,) + zeros + (i, 0))
    prev = pl.BlockSpec((1,) + lead_blocks + (halo, cols),
                        lambda bi, i: (bi,) + zeros + (jnp.maximum(i * per - 1, 0), 0))
    nxt = pl.BlockSpec((1,) + lead_blocks + (halo, cols),
                       lambda bi, i: (bi,) + zeros + (jnp.minimum((i + 1) * per, nhalo - 1), 0))
    return [main, prev, nxt]


def _mix_attn(x, sink, bias_a, bias_b, qa, kva, qkv_b0, qkv_b1, ob2,
              g_pre, w_gate, b_gate, w_a, w_b, w_out, g_post):
    b, s, _ = x.shape
    t = ROW_TILE
    dil1 = B_PATTERNS[1][1]
    nslab = B_GROUP_COLS // LANES
    row_map = lambda bi, i: (bi, i, 0)
    slab_spec = pl.BlockSpec((1, nslab, t, LANES), lambda bi, i: (bi, 0, i, 0))
    in_specs = ([pl.BlockSpec(memory_space=pltpu.SMEM), _const_spec(bias_a.shape), _const_spec(bias_b.shape),
                 pl.BlockSpec((1, t, D_MODEL), row_map),
                 pl.BlockSpec((1, t, A_Q_COLS), row_map)]
                + _halo_specs((), t, A_Q_BLOCK, s, A_KVA_COLS)
                + _halo_specs((1,), t, B_HALF_WINDOW, s, B_QKV_COLS)
                + _halo_specs((dil1,), t // dil1, B_HALF_WINDOW, s // dil1, B_QKV_COLS)
                + [slab_spec, slab_spec,
                   _const_spec((1, D_MODEL)),
                   _const_spec((D_MODEL, 2 * D_MODEL)),
                   _const_spec((1, 2 * D_MODEL)),
                   _const_spec((A_Q_COLS, D_MODEL)),
                   _const_spec((B_GROUP_COLS, D_MODEL)),
                   _const_spec((D_MODEL, D_MODEL)),
                   _const_spec((1, D_MODEL))])
    return pl.pallas_call(
        _mix_attn_kernel,
        grid=(b, s // t),
        in_specs=in_specs,
        out_specs=pl.BlockSpec((1, t, D_MODEL), row_map),
        out_shape=jax.ShapeDtypeStruct((b, s, D_MODEL), F32),
        scratch_shapes=[
            pltpu.VMEM((t, A_Q_COLS), BF16),
            pltpu.VMEM((2, 2, nslab, t, LANES), F32),
        ],
        compiler_params=pltpu.CompilerParams(
            dimension_semantics=("parallel", "parallel"), vmem_limit_bytes=VMEM_LIMIT_BYTES),
        name="mix_attn",
    )(sink, bias_a, bias_b, x, qa, kva, kva, kva, qkv_b0, qkv_b0, qkv_b0, qkv_b1, qkv_b1, qkv_b1,
      ob2[0], ob2[1], g_pre, w_gate, b_gate, w_a, w_b, w_out, g_post)


def _gelu_tanh(x):
    c = -2.0 * math.sqrt(2.0 / math.pi)
    return x / (1.0 + jnp.exp(x * (c + (c * 0.044715) * (x * x))))


def _conv_ffn_kernel(x_ref, prev_ref, next_ref, g_pre_ref, w_up_ref, conv_w_ref, conv_b_ref,
                     w_down_ref, g_post_ref, out_ref, u_ref, acc_ref, perm_ref):
    t = ROW_TILE
    half = t // 2
    i = pl.program_id(1)
    last = pl.num_programs(1) - 1
    row = lax.broadcasted_iota(jnp.int32, (t + 2 * HALO, 1), 0)
    starts = [sum(FF_CHUNKS[:c]) for c in range(len(FF_CHUNKS))]

    def col_starts(c):
        return (starts[c], D_FF + starts[c])

    def project(h, c, slot):
        for part, col0 in enumerate(col_starts(c)):
            up = _dot(h, w_up_ref[:, col0:col0 + FF_CHUNKS[c]])
            for j in range(FF_CHUNKS[c] // LANES):
                u_ref[2 * slot + part, j] = up[:, j * LANES:(j + 1) * LANES]

    def conv(c, slot, part):
        buf, col0 = 2 * slot + part, col_starts(c)[part]
        pieces = []
        for j in range(FF_CHUNKS[c] // LANES):
            cols = slice(col0 + j * LANES, col0 + (j + 1) * LANES)
            w0, w1, w2 = conv_w_ref[0:1, cols], conv_w_ref[1:2, cols], conv_w_ref[2:3, cols]
            bias = conv_b_ref[:, cols]
            r = [u_ref[buf, j, pl.ds(HALO - 1 + k, half, stride=2), :] for k in range(4)]
            even = r[0] * w0 + r[1] * w1 + r[2] * w2 + bias
            odd = r[1] * w0 + r[2] * w1 + r[3] * w2 + bias
            pieces.append(jnp.concatenate([even, odd], axis=0))
        return jnp.concatenate(pieces, axis=-1)

    def tile(sub, carry):
        lo = pl.multiple_of(sub * t, t)
        x = x_ref[0, pl.ds(lo, t), :]
        first, final = sub == 0, sub == FFN_SUB_TILES - 1
        inner_before = x_ref[0, pl.ds(pl.multiple_of(jnp.maximum(lo - HALO, 0), HALO), HALO), :]
        inner_after = x_ref[0, pl.ds(pl.multiple_of(jnp.minimum(lo + t, (FFN_SUB_TILES - 1) * t), HALO), HALO), :]
        before = jnp.where(first, prev_ref[0], inner_before)
        after = jnp.where(final, next_ref[0], inner_after)
        hn = _rms_norm(jnp.concatenate([before, x, after], axis=0), g_pre_ref[...])
        hn = jnp.where(jnp.logical_or(row >= HALO, jnp.logical_or(i > 0, sub > 0)), hn, 0.0)
        hn = jnp.where(jnp.logical_or(row < HALO + t, jnp.logical_or(i < last, sub < FFN_SUB_TILES - 1)), hn, 0.0)
        h = hn.astype(BF16)

        n_chunks = len(FF_CHUNKS)
        for c in range(min(FF_LOOKAHEAD, n_chunks)):
            project(h, c, c % (FF_LOOKAHEAD + 1))
        for c in range(n_chunks):
            if c + FF_LOOKAHEAD < n_chunks:
                project(h, c + FF_LOOKAHEAD, (c + FF_LOOKAHEAD) % (FF_LOOKAHEAD + 1))
            slot = c % (FF_LOOKAHEAD + 1)
            act = (_gelu_tanh(conv(c, slot, 0)) * conv(c, slot, 1)).astype(BF16)
            part = _dot(act, w_down_ref[starts[c]:starts[c] + FF_CHUNKS[c], :])
            if c == 0:
                acc_ref[...] = part
            else:
                acc_ref[...] += part
        y = _rms_norm(acc_ref[...], g_post_ref[...])
        for j in range(D_MODEL // LANES):
            perm_ref[j, pl.ds(0, half, stride=2), :] = y[:half, j * LANES:(j + 1) * LANES]
            perm_ref[j, pl.ds(1, half, stride=2), :] = y[half:, j * LANES:(j + 1) * LANES]
        y_nat = jnp.concatenate([perm_ref[j] for j in range(D_MODEL // LANES)], axis=-1)
        out_ref[0, pl.ds(lo, t), :] = x + y_nat
        return carry

    lax.fori_loop(0, FFN_SUB_TILES, tile, 0)


def _conv_ffn(x, g_pre, w_up, conv_w, conv_b, w_down, g_post):
    b, s, _ = x.shape
    t = FFN_SUB_TILES * ROW_TILE
    tiles = s // t
    halo_blocks_per_tile = t // HALO
    n_halo_blocks = s // HALO
    row_map = lambda bi, i: (bi, i, 0)
    prev_map = lambda bi, i: (bi, jnp.maximum(i * halo_blocks_per_tile - 1, 0), 0)
    next_map = lambda bi, i: (bi, jnp.minimum((i + 1) * halo_blocks_per_tile, n_halo_blocks - 1), 0)
    return pl.pallas_call(
        _conv_ffn_kernel,
        grid=(b, tiles),
        in_specs=[
            pl.BlockSpec((1, t, D_MODEL), row_map),
            pl.BlockSpec((1, HALO, D_MODEL), prev_map),
            pl.BlockSpec((1, HALO, D_MODEL), next_map),
            _const_spec((1, D_MODEL)),
            _const_spec((D_MODEL, 2 * D_FF)),
            _const_spec((3, 2 * D_FF)),
            _const_spec((1, 2 * D_FF)),
            _const_spec((D_FF, D_MODEL)),
            _const_spec((1, D_MODEL)),
        ],
        out_specs=pl.BlockSpec((1, t, D_MODEL), row_map),
        out_shape=jax.ShapeDtypeStruct((b, s, D_MODEL), F32),
        scratch_shapes=[
            pltpu.VMEM((2 * (FF_LOOKAHEAD + 1), FF_CHUNK // LANES, ROW_TILE + 2 * HALO, LANES), F32),
            pltpu.VMEM((ROW_TILE, D_MODEL), F32),
            pltpu.VMEM((D_MODEL // LANES, ROW_TILE, LANES), F32),
        ],
        compiler_params=pltpu.CompilerParams(
            dimension_semantics=("parallel", "parallel"), vmem_limit_bytes=VMEM_LIMIT_BYTES),
        name="conv_ffn",
    )(x, x, x, g_pre, w_up, conv_w, conv_b, w_down, g_post)


def _rope_tables(seq_len):
    half = HEAD_DIM // 2
    inv = ROPE_THETA ** (-jnp.arange(half, dtype=F32) / half)
    ang = jnp.arange(seq_len, dtype=F32)[:, None] * inv[None, :]
    cos, sin = jnp.cos(ang), jnp.sin(ang)
    cos_head = jnp.concatenate([cos, cos], axis=-1)
    sin_head = jnp.concatenate([-sin, sin], axis=-1)
    reps = LANES // HEAD_DIM
    return jnp.tile(cos_head, (1, reps)), jnp.tile(sin_head, (1, reps))


def _band_bias(q_rows, stacked, keys, half_window, lead):
    i = (jnp.arange(stacked * q_rows, dtype=jnp.int32) % q_rows)[None, :, None]
    j = jnp.arange(keys, dtype=jnp.int32)[None, None, :]
    e = jnp.arange(3, dtype=jnp.int32)[:, None, None]
    band = jnp.abs(j - lead - i) <= half_window
    inside = jnp.logical_and(jnp.logical_or(e != 0, j >= lead), jnp.logical_or(e != 2, j < lead + q_rows))
    return jnp.where(jnp.logical_and(band, inside), 0.0, NEG_INF).astype(F32)


def _far_band_bias():
    off = jnp.arange(3, dtype=jnp.int32)[:, None, None] * B_HALF_WINDOW
    i = (jnp.arange(2 * B_Q_BLOCK, dtype=jnp.int32) % B_Q_BLOCK)[None, :, None]
    j = jnp.arange(B_KEY_WINDOW, dtype=jnp.int32)[None, None, :]
    return jnp.where(jnp.abs(j - off - i) <= B_HALF_WINDOW, 0.0, NEG_INF).astype(F32)


def kernel(x, norm_mix_pre, w_in, sink, w_branch_a, w_branch_b, w_gate, b_gate, w_out,
           norm_mix_post, norm_ffn_pre, w_up, conv_w, conv_b, w_down, norm_ffn_post):
    b, s, d = x.shape
    assert d == D_MODEL and s % B_SPAN == 0 and s % (SUB_TILES * ROW_TILE) == 0
    cos, sin_signed = _rope_tables(s)
    bias_a = _band_bias(A_Q_BLOCK, 1, A_KEY_WINDOW, A_HALF_WINDOW, A_Q_BLOCK)
    bias_b = _band_bias(B_Q_BLOCK, 2, B_KEY_WINDOW, B_HALF_WINDOW, B_HALF_WINDOW)
    bias_b_far = _far_band_bias()
    for layer in range(norm_mix_pre.shape[0]):
        proj = _in_proj(x, norm_mix_pre[layer][None], w_in[layer].astype(BF16), cos, sin_signed)
        qa, kva, qkv_b0, qkv_b1, qkv_b2 = proj
        o2, l2, w_gate_h, w_a_h, w_b_h, w_out_h, w_up_h, w_down_h = _attn_b(
            bias_b_far, qkv_b2, B_PATTERNS[2][1],
            (w_gate[layer], w_branch_a[layer], w_branch_b[layer], w_out[layer], w_up[layer], w_down[layer]))
        ob2 = (o2, l2)
        x = _mix_attn(x, sink[layer], bias_a, bias_b, qa, kva, qkv_b0, qkv_b1, ob2,
                      norm_mix_pre[layer][None],
                      w_gate_h, b_gate[layer][None], w_a_h, w_b_h, w_out_h, norm_mix_post[layer][None])
        x = _conv_ffn(x, norm_ffn_pre[layer][None],
                      w_up_h, conv_w[layer], conv_b[layer][None], w_down_h,
                      norm_ffn_post[layer][None])
    return x
```

```python
import functools
import math

import jax
import jax.numpy as jnp
from jax import lax
from jax.experimental import pallas as pl
from jax.experimental.pallas import tpu as pltpu

D_MODEL = 1024
HEAD_DIM = 64
A_Q_HEADS = 8
A_KV_HEADS = 2
A_GROUP = A_Q_HEADS // A_KV_HEADS
A_HALF_WINDOW = 128
B_PATTERNS = ((128, 1), (512, 4), (2048, 16))
B_N_GROUPS = len(B_PATTERNS)
B_HEADS = 4
HEAD_ORDER = (0, 2, 1, 3)
B_HALF_WINDOW = 64
ROPE_THETA = 10000.0
D_FF = 3 * D_MODEL
RMS_EPS = 1e-6
NEG_INF = -1e30
LOG2_E = math.log2(math.e)

A_Q_COLS = A_Q_HEADS * HEAD_DIM
A_KV_COLS = A_KV_HEADS * HEAD_DIM
A_COLS = A_Q_COLS + 2 * A_KV_COLS
B_GROUP_COLS = B_HEADS * HEAD_DIM
B_PROJ_COLS = B_N_GROUPS * B_GROUP_COLS
IN_COLS = A_COLS + 3 * B_PROJ_COLS
A_KVA_COLS = 4 * A_KV_COLS
B_QKV_COLS = 3 * B_GROUP_COLS

LANES = 128
VMEM_LIMIT_BYTES = 56 * 1024 * 1024

ROW_TILE = 512
SUB_TILES = 2
FFN_SUB_TILES = 2
A_Q_BLOCK = 128
A_KEY_WINDOW = 3 * A_Q_BLOCK
B_Q_BLOCK = 128
B_KEY_WINDOW = B_Q_BLOCK + 2 * B_HALF_WINDOW
B_SPAN = 2048
B_UNROLL = 8
FF_CHUNK = 512
FF_CHUNKS = (FF_CHUNK,) * (D_FF // FF_CHUNK)
FF_LOOKAHEAD = 2
HALO = 8

BF16 = jnp.bfloat16
F32 = jnp.float32


def _dot(a, b):
    return jnp.dot(a, b, preferred_element_type=F32)


def _dot_nt(a, b):
    return lax.dot_general(a, b, (((1,), (1,)), ((), ())), preferred_element_type=F32)


def _rms_norm(x, gain):
    ms = jnp.mean(x * x, axis=-1, keepdims=True)
    return x * lax.rsqrt(ms + RMS_EPS) * gain


def _rope_chunk(p, cos, sin_signed, first_half):
    partner = jnp.where(first_half, pltpu.roll(p, 96, 1), pltpu.roll(p, 32, 1))
    return p * cos + partner * sin_signed


def _const_spec(shape):
    nd = len(shape)
    return pl.BlockSpec(shape, lambda *_: (0,) * nd, pipeline_mode=pl.Buffered(1))


def _row_chunk_specs(weights, steps_per_seq, n_steps):
    return [pl.BlockSpec((w.shape[0] // n_steps, w.shape[1]), lambda bi, i: (bi * steps_per_seq + i, 0))
            for w in weights]


def _cast_chunks(src_refs, dst_refs):
    for src_ref, dst_ref in zip(src_refs, dst_refs):
        dst_ref[...] = src_ref[...].astype(BF16)


def _in_proj_kernel(x_ref, gain_ref, w_ref, cos_ref, sin_ref,
                    qa_ref, kva_ref, qkv0_ref, qkv1_ref, qkv2_ref, slab_ref):
    t = ROW_TILE
    lane = lax.broadcasted_iota(jnp.int32, (t, LANES), 1)
    first_half = (lane % HEAD_DIM) < (HEAD_DIM // 2)
    low_head = lane < HEAD_DIM
    scale = HEAD_DIM ** -0.5 * LOG2_E
    out_refs = (qkv0_ref, qkv1_ref, qkv2_ref)

    def tile(sub):
        rows = slice(sub * t, (sub + 1) * t)
        h = _rms_norm(x_ref[0, rows, :], gain_ref[...]).astype(BF16)
        cos = cos_ref[rows, :]
        sin_signed = sin_ref[rows, :]

        def proj(col0, ncols):
            return _dot(h, w_ref[:, col0:col0 + ncols])

        def rope(p, mult=None):
            chunks = []
            for c in range(p.shape[1] // LANES):
                r = _rope_chunk(p[:, c * LANES:(c + 1) * LANES], cos, sin_signed, first_half)
                chunks.append(r if mult is None else r * mult)
            return chunks

        def dup_heads(p):
            swapped = pltpu.roll(p, HEAD_DIM, 1)
            return [jnp.where(low_head, p, swapped), jnp.where(low_head, swapped, p)]

        def windowed():
            for c, chunk in enumerate(rope(proj(0, A_Q_COLS), scale)):
                qa_ref[0, rows, c * LANES:(c + 1) * LANES] = chunk.astype(BF16)
            kv = proj(A_Q_COLS, 2 * A_KV_COLS)
            ka = rope(kv[:, :A_KV_COLS])[0]
            for c, chunk in enumerate(dup_heads(ka) + dup_heads(kv[:, A_KV_COLS:])):
                kva_ref[0, rows, c * LANES:(c + 1) * LANES] = chunk.astype(BF16)

        def dilated(g):
            dil = B_PATTERNS[g][1]
            for kind in range(3):
                col0 = A_COLS + kind * B_PROJ_COLS + g * B_GROUP_COLS
                p = proj(col0, B_GROUP_COLS)
                if kind == 0:
                    chunks = rope(p, scale)
                elif kind == 1:
                    chunks = rope(p)
                else:
                    chunks = [p[:, c * LANES:(c + 1) * LANES] for c in range(B_GROUP_COLS // LANES)]
                o_ref = out_refs[g]
                sub_rows = slice(sub * t // dil, (sub + 1) * t // dil)
                col = lambda c: slice(kind * B_GROUP_COLS + c * LANES, kind * B_GROUP_COLS + (c + 1) * LANES)
                if dil == 1:
                    for c, chunk in enumerate(chunks):
                        o_ref[0, 0, sub_rows, col(c)] = chunk.astype(BF16)
                else:
                    for c, chunk in enumerate(chunks):
                        slab_ref[sub, kind, c] = chunk
                    for c in range(len(chunks)):
                        for res in range(dil):
                            picked = slab_ref[sub, kind, c, pl.ds(res, t // dil, stride=dil), :]
                            o_ref[0, res, sub_rows, col(c)] = picked.astype(BF16)

        dilated(2)
        dilated(1)
        windowed()
        dilated(0)

    for sub in range(SUB_TILES):
        tile(sub)


def _in_proj(x, gain, w_in, cos, sin_signed):
    b, s, _ = x.shape
    t = SUB_TILES * ROW_TILE
    grid = (b, s // t)
    row_map = lambda bi, i: (bi, i, 0)
    out_shape = [
        jax.ShapeDtypeStruct((b, s, A_Q_COLS), BF16),
        jax.ShapeDtypeStruct((b, s, A_KVA_COLS), BF16),
    ]
    out_specs = [
        pl.BlockSpec((1, t, A_Q_COLS), row_map),
        pl.BlockSpec((1, t, A_KVA_COLS), row_map),
    ]
    for _, dil in B_PATTERNS:
        out_shape.append(jax.ShapeDtypeStruct((b, dil, s // dil, B_QKV_COLS), BF16))
        out_specs.append(pl.BlockSpec((1, dil, t // dil, B_QKV_COLS), lambda bi, i: (bi, 0, i, 0)))
    return pl.pallas_call(
        _in_proj_kernel,
        grid=grid,
        in_specs=[
            pl.BlockSpec((1, t, D_MODEL), row_map),
            _const_spec((1, D_MODEL)),
            _const_spec((D_MODEL, IN_COLS)),
            pl.BlockSpec((t, LANES), lambda bi, i: (i, 0)),
            pl.BlockSpec((t, LANES), lambda bi, i: (i, 0)),
        ],
        out_specs=out_specs,
        out_shape=out_shape,
        scratch_shapes=[pltpu.VMEM((SUB_TILES, 3, B_GROUP_COLS // LANES, ROW_TILE, LANES), F32)],
        compiler_params=pltpu.CompilerParams(
            dimension_semantics=("parallel", "parallel"), vmem_limit_bytes=VMEM_LIMIT_BYTES),
        name="in_proj",
    )(x, gain, w_in, cos, sin_signed)


def _attn_b_kernel(bias_ref, q_ref, kv_ref, *rest, dil, n_cast):
    o_ref, lse_ref = rest[n_cast:n_cast + 2]
    _cast_chunks(rest[:n_cast], rest[n_cast + 2:])
    sub_len = kv_ref.shape[2]
    sub_rows = q_ref.shape[2]
    blocks = sub_rows // B_Q_BLOCK
    step = pl.program_id(1)
    low_head = lax.broadcasted_iota(jnp.int32, (B_Q_BLOCK, LANES), 1) < HEAD_DIM
    low_key = lax.broadcasted_iota(jnp.int32, (B_KEY_WINDOW, LANES), 1) < HEAD_DIM

    def units(it, carry):
        work = []
        for uu in range(B_UNROLL):
            u = it * B_UNROLL + uu
            res = u // blocks
            r0 = (u % blocks) * B_Q_BLOCK
            q_pos = (step * sub_rows + r0) // B_HALF_WINDOW
            start_pos = jnp.clip(q_pos - 1, 0, (sub_len - B_KEY_WINDOW) // B_HALF_WINDOW)
            start = start_pos * B_HALF_WINDOW
            bias = bias_ref[q_pos - start_pos]
            q_blk = q_ref[0, res, pl.ds(r0, B_Q_BLOCK), 0:B_GROUP_COLS]
            k_win = kv_ref[0, res, pl.ds(start, B_KEY_WINDOW), B_GROUP_COLS:2 * B_GROUP_COLS]
            v_win = kv_ref[0, res, pl.ds(start, B_KEY_WINDOW), 2 * B_GROUP_COLS:3 * B_GROUP_COLS]
            if dil == 1:
                rows = pl.ds(r0, B_Q_BLOCK)
            else:
                rows = pl.ds(r0 * dil + res, B_Q_BLOCK, stride=dil)
            for pr in range(B_HEADS // 2):
                qp = q_blk[:, pr * LANES:(pr + 1) * LANES]
                zero = jnp.zeros_like(qp)
                lhs = jnp.concatenate([jnp.where(low_head, qp, zero), jnp.where(low_head, zero, qp)], axis=0)
                sc = _dot_nt(lhs, k_win[:, pr * LANES:(pr + 1) * LANES])
                work.append((rows, pr, bias, v_win[:, pr * LANES:(pr + 1) * LANES], sc))

        soft = []
        for rows, pr, bias, vp, sc in work:
            sc = sc + bias
            m = jnp.max(sc, axis=-1, keepdims=True)
            soft.append((rows, pr, vp, m, jnp.exp2(sc - m).astype(BF16)))

        for rows, pr, vp, m, p in soft:
            one = jnp.ones_like(vp)
            top = _dot(p[:B_Q_BLOCK], jnp.where(low_key, vp, one))
            bot = _dot(p[B_Q_BLOCK:], jnp.where(low_key, one, vp))
            denom = pltpu.roll(jnp.where(low_head, bot, top), HEAD_DIM, 1)
            o_ref[0, pr, rows, :] = jnp.where(low_head, top, bot) / denom
            lse_ref[0, pr, rows, :] = jnp.where(low_head, m[:B_Q_BLOCK], m[B_Q_BLOCK:]) + jnp.log2(denom)
        return carry

    lax.fori_loop(0, dil * blocks // B_UNROLL, units, 0)


def _attn_b(bias, qkv, dil, cast_weights=()):
    b, _, sub_len, _ = qkv.shape
    s = sub_len * dil
    sub_rows = B_SPAN // dil
    nslab = B_GROUP_COLS // LANES
    out_sds = jax.ShapeDtypeStruct((b, nslab, s, LANES), F32)
    out_spec = pl.BlockSpec((1, nslab, B_SPAN, LANES), lambda bi, i: (bi, 0, i, 0))
    steps_per_seq = s // B_SPAN
    chunk_specs = _row_chunk_specs(cast_weights, steps_per_seq, b * steps_per_seq)
    return pl.pallas_call(
        functools.partial(_attn_b_kernel, dil=dil, n_cast=len(cast_weights)),
        grid=(b, steps_per_seq),
        in_specs=[
            _const_spec(bias.shape),
            pl.BlockSpec((1, dil, sub_rows, B_QKV_COLS), lambda bi, i: (bi, 0, i, 0)),
            pl.BlockSpec((1, dil, sub_len, B_QKV_COLS), lambda bi, i: (bi, 0, 0, 0)),
        ] + chunk_specs,
        out_specs=[out_spec, out_spec] + chunk_specs,
        out_shape=[out_sds, out_sds] + [jax.ShapeDtypeStruct(w.shape, BF16) for w in cast_weights],
        compiler_params=pltpu.CompilerParams(
            dimension_semantics=("parallel", "parallel"), vmem_limit_bytes=VMEM_LIMIT_BYTES),
        name=f"attn_b_d{dil}",
    )(bias, qkv, qkv, *cast_weights)


def _window(main_ref, prev_ref, next_ref, lead, lo, hi, length, cols):
    parts = []
    if lo < 0:
        halo = prev_ref.shape[-2]
        parts.append(prev_ref[lead + (slice(halo + lo, halo), cols)])
        lo = 0
    parts.append(main_ref[lead + (slice(lo, min(hi, length)), cols)])
    if hi > length:
        parts.append(next_ref[lead + (slice(0, hi - length), cols)])
    return parts[0] if len(parts) == 1 else jnp.concatenate(parts, axis=0)


def _mix_attn_kernel(sink_ref, bias_a_ref, bias_b_ref, x_ref,
                     qa_ref, kva_ref, kva_p_ref, kva_n_ref,
                     b0_ref, b0_p_ref, b0_n_ref, b1_ref, b1_p_ref, b1_n_ref,
                     o2_ref, l2_ref,
                     g_pre_ref, w_gate_ref, b_gate_ref, w_a_ref, w_b_ref, w_out_ref, g_post_ref,
                     *rest, n_cast):
    out_ref = rest[n_cast]
    ya_ref, ob_ref = rest[2 * n_cast + 1:]
    _cast_chunks(rest[:n_cast], rest[n_cast + 1:2 * n_cast + 1])
    t = ROW_TILE
    step_rows = t
    i = pl.program_id(1)
    last = pl.num_programs(1) - 1
    low_head = lax.broadcasted_iota(jnp.int32, (A_Q_BLOCK, LANES), 1) < HEAD_DIM
    low_key_a = lax.broadcasted_iota(jnp.int32, (A_KEY_WINDOW, LANES), 1) < HEAD_DIM
    low_key_b = lax.broadcasted_iota(jnp.int32, (B_KEY_WINDOW, LANES), 1) < HEAD_DIM
    dil1 = B_PATTERNS[1][1]
    sub_rows = step_rows // dil1

    def edge_index(first, final):
        idx = 1
        if first:
            idx = jnp.where(i == 0, 0, idx)
        if final:
            idx = jnp.where(i == last, 2, idx)
        return idx

    def scores(grp_idx):
        work_a, work_b = [], []
        n = grp_idx
        lo, hi = (n - 1) * A_Q_BLOCK, (n + 2) * A_Q_BLOCK
        bias = bias_a_ref[edge_index(n == 0, hi > step_rows)]
        q_blk = qa_ref[0, n * A_Q_BLOCK:(n + 1) * A_Q_BLOCK, :]
        for j in range(A_KV_HEADS):
            kcols = slice(j * LANES, (j + 1) * LANES)
            vcols = slice(2 * A_KV_COLS + j * LANES, 2 * A_KV_COLS + (j + 1) * LANES)
            kd = _window(kva_ref, kva_p_ref, kva_n_ref, (0,), lo, hi, step_rows, kcols)
            vd = _window(kva_ref, kva_p_ref, kva_n_ref, (0,), lo, hi, step_rows, vcols)
            lhs = []
            for g in HEAD_ORDER:
                c0 = j * A_GROUP * HEAD_DIM + (g // 2) * LANES
                pair = q_blk[:, c0:c0 + LANES]
                keep = low_head if g % 2 == 0 else jnp.logical_not(low_head)
                lhs.append(jnp.where(keep, pair, jnp.zeros_like(pair)))
            sc = _dot_nt(jnp.concatenate(lhs, axis=0), kd)
            work_a.append((n, j, bias, vd, sc))
        units = [(0, (0, 0), grp_idx, step_rows), (1, (0, grp_idx), 0, sub_rows)]
        for grp, lead, n, length in units:
            m_ref, p_ref, n_ref = (b0_ref, b0_p_ref, b0_n_ref) if grp == 0 else (b1_ref, b1_p_ref, b1_n_ref)
            lo, hi = n * B_Q_BLOCK - B_HALF_WINDOW, (n + 1) * B_Q_BLOCK + B_HALF_WINDOW
            bias = bias_b_ref[edge_index(lo < 0, hi > length)]
            if grp == 0:
                rows = pl.ds(n * B_Q_BLOCK, B_Q_BLOCK)
            else:
                rows = pl.ds(n * B_Q_BLOCK * dil1 + lead[1], B_Q_BLOCK, stride=dil1)
            for pr in range(B_HEADS // 2):
                cols = [slice(kind * B_GROUP_COLS + pr * LANES, kind * B_GROUP_COLS + (pr + 1) * LANES)
                        for kind in range(3)]
                qp = m_ref[lead + (slice(n * B_Q_BLOCK, (n + 1) * B_Q_BLOCK), cols[0])]
                kp = _window(m_ref, p_ref, n_ref, lead, lo, hi, length, cols[1])
                vp = _window(m_ref, p_ref, n_ref, lead, lo, hi, length, cols[2])
                zero = jnp.zeros_like(qp)
                lhs = jnp.concatenate([jnp.where(low_head, qp, zero), jnp.where(low_head, zero, qp)], axis=0)
                work_b.append((grp, rows, pr, bias, vp, _dot_nt(lhs, kp)))
        return work_a, work_b

    def softmaxes(work):
        work_a, work_b = work
        soft_a, soft_b = [], []
        for n, j, bias, vd, sc in work_a:
            probs, sink_terms = {}, {}
            for slot, g in enumerate(HEAD_ORDER):
                sg = sc[slot * A_Q_BLOCK:(slot + 1) * A_Q_BLOCK] + bias
                sink = sink_ref[j * A_GROUP + g] * LOG2_E
                m = jnp.maximum(jnp.max(sg, axis=-1, keepdims=True), sink)
                probs[g] = jnp.exp2(sg - m).astype(BF16)
                sink_terms[g] = jnp.exp2(sink - m)
            soft_a.append((n, j, vd, probs, sink_terms))
        for grp, rows, pr, bias, vp, sc in work_b:
            sc = sc + bias
            m = jnp.max(sc, axis=-1, keepdims=True)
            soft_b.append((grp, rows, pr, vp, m, jnp.exp2(sc - m).astype(BF16)))
        return soft_a, soft_b

    def values(soft):
        soft_a, soft_b = soft
        pairs = range(A_GROUP // 2)
        for n, j, vd, probs, sink_terms in soft_a:
            one = jnp.ones_like(vd)
            r_even = _dot(jnp.concatenate([probs[2 * pr] for pr in pairs], axis=0),
                          jnp.where(low_key_a, vd, one))
            r_odd = _dot(jnp.concatenate([probs[2 * pr + 1] for pr in pairs], axis=0),
                         jnp.where(low_key_a, one, vd))
            for pr in pairs:
                even = r_even[pr * A_Q_BLOCK:(pr + 1) * A_Q_BLOCK]
                odd = r_odd[pr * A_Q_BLOCK:(pr + 1) * A_Q_BLOCK]
                num = jnp.where(low_head, even, odd)
                den = jnp.where(low_head, odd + sink_terms[2 * pr + 1], even + sink_terms[2 * pr])
                c0 = j * A_GROUP * HEAD_DIM + pr * LANES
                ya_ref[n * A_Q_BLOCK:(n + 1) * A_Q_BLOCK, c0:c0 + LANES] = (
                    num / pltpu.roll(den, HEAD_DIM, 1)).astype(BF16)
        for grp, rows, pr, vp, m, p in soft_b:
            one = jnp.ones_like(vp)
            top = _dot(p[:B_Q_BLOCK], jnp.where(low_key_b, vp, one))
            bot = _dot(p[B_Q_BLOCK:], jnp.where(low_key_b, one, vp))
            denom = pltpu.roll(jnp.where(low_head, bot, top), HEAD_DIM, 1)
            ob_ref[grp, 0, pr, rows, :] = jnp.where(low_head, top, bot) / denom
            ob_ref[grp, 1, pr, rows, :] = jnp.where(low_head, m[:B_Q_BLOCK], m[B_Q_BLOCK:]) + jnp.log2(denom)

    n_groups = t // A_Q_BLOCK
    assert step_rows == t and n_groups == dil1 == t // B_Q_BLOCK
    gate_chunk = 2 * D_MODEL // n_groups
    x = x_ref[0]
    h = _rms_norm(x, g_pre_ref[...]).astype(BF16)

    def gate_logits(j):
        cols = slice(j * gate_chunk, (j + 1) * gate_chunk)
        return _dot(h, w_gate_ref[:, cols]) + b_gate_ref[:, cols]

    z = [None] * n_groups
    work = scores(0)
    z[0] = gate_logits(0)
    soft = softmaxes(work)
    work = scores(1)
    values(soft)
    z[1] = gate_logits(1)
    soft = softmaxes(work)
    work = scores(2)
    values(soft)
    soft = softmaxes(work)
    work = scores(3)
    values(soft)
    z[2] = gate_logits(2)
    z[3] = gate_logits(3)
    values(softmaxes(work))

    yb = []
    for c in range(B_GROUP_COLS // LANES):
        l0, l1, l2 = ob_ref[0, 1, c], ob_ref[1, 1, c], l2_ref[0, c]
        m = jnp.maximum(jnp.maximum(l0, l1), l2)
        e0, e1, e2 = jnp.exp2(l0 - m), jnp.exp2(l1 - m), jnp.exp2(l2 - m)
        num = e0 * ob_ref[0, 0, c] + e1 * ob_ref[1, 0, c] + e2 * o2_ref[0, c]
        yb.append((num / (e0 + e1 + e2)).astype(BF16))
    yb = jnp.concatenate(yb, axis=-1)
    gates = jax.nn.sigmoid(jnp.concatenate(z, axis=-1))
    merged = (gates[:, :D_MODEL] * _dot(ya_ref[...], w_a_ref[...])
              + gates[:, D_MODEL:] * _dot(yb, w_b_ref[...]))
    mix = _dot(merged.astype(BF16), w_out_ref[...])
    out_ref[0] = x + _rms_norm(mix, g_post_ref[...])


def _halo_specs(lead_blocks, rows, halo, total_rows, cols):
    nlead = len(lead_blocks)
    per = rows // halo
    nhalo = total_rows // halo
    zeros = (0,) * nlead
    main = pl.BlockSpec((1,) + lead_blocks + (rows, cols), lambda bi, i: (bi,) + zeros + (i, 0))
    prev = pl.BlockSpec((1,) + lead_blocks + (halo, cols),
                        lambda bi, i: (bi,) + zeros + (jnp.maximum(i * per - 1, 0), 0))
    nxt = pl.BlockSpec((1,) + lead_blocks + (halo, cols),
                       lambda bi, i: (bi,) + zeros + (jnp.minimum((i + 1) * per, nhalo - 1), 0))
    return [main, prev, nxt]


def _mix_attn(x, sink, bias_a, bias_b, qa, kva, qkv_b0, qkv_b1, ob2,
              g_pre, w_gate, b_gate, w_a, w_b, w_out, g_post, cast_weights=()):
    b, s, _ = x.shape
    t = ROW_TILE
    dil1 = B_PATTERNS[1][1]
    nslab = B_GROUP_COLS // LANES
    row_map = lambda bi, i: (bi, i, 0)
    slab_spec = pl.BlockSpec((1, nslab, t, LANES), lambda bi, i: (bi, 0, i, 0))
    chunk_specs = _row_chunk_specs(cast_weights, s // t, b * s // t)
    in_specs = ([pl.BlockSpec(memory_space=pltpu.SMEM), _const_spec(bias_a.shape), _const_spec(bias_b.shape),
                 pl.BlockSpec((1, t, D_MODEL), row_map),
                 pl.BlockSpec((1, t, A_Q_COLS), row_map)]
                + _halo_specs((), t, A_Q_BLOCK, s, A_KVA_COLS)
                + _halo_specs((1,), t, B_HALF_WINDOW, s, B_QKV_COLS)
                + _halo_specs((dil1,), t // dil1, B_HALF_WINDOW, s // dil1, B_QKV_COLS)
                + [slab_spec, slab_spec,
                   _const_spec((1, D_MODEL)),
                   _const_spec((D_MODEL, 2 * D_MODEL)),
                   _const_spec((1, 2 * D_MODEL)),
                   _const_spec((A_Q_COLS, D_MODEL)),
                   _const_spec((B_GROUP_COLS, D_MODEL)),
                   _const_spec((D_MODEL, D_MODEL)),
                   _const_spec((1, D_MODEL))])
    return pl.pallas_call(
        functools.partial(_mix_attn_kernel, n_cast=len(cast_weights)),
        grid=(b, s // t),
        in_specs=in_specs + chunk_specs,
        out_specs=[pl.BlockSpec((1, t, D_MODEL), row_map)] + chunk_specs,
        out_shape=[jax.ShapeDtypeStruct((b, s, D_MODEL), F32)]
        + [jax.ShapeDtypeStruct(w.shape, BF16) for w in cast_weights],
        scratch_shapes=[
            pltpu.VMEM((t, A_Q_COLS), BF16),
            pltpu.VMEM((2, 2, nslab, t, LANES), F32),
        ],
        compiler_params=pltpu.CompilerParams(
            dimension_semantics=("parallel", "parallel"), vmem_limit_bytes=VMEM_LIMIT_BYTES),
        name="mix_attn",
    )(sink, bias_a, bias_b, x, qa, kva, kva, kva, qkv_b0, qkv_b0, qkv_b0, qkv_b1, qkv_b1, qkv_b1,
      ob2[0], ob2[1], g_pre, w_gate, b_gate, w_a, w_b, w_out, g_post, *cast_weights)


def _gelu_tanh(x):
    c = -2.0 * math.sqrt(2.0 / math.pi)
    return x / (1.0 + jnp.exp(x * (c + (c * 0.044715) * (x * x))))


def _conv_ffn_kernel(x_ref, prev_ref, next_ref, g_pre_ref, w_up_ref, conv_w_ref, conv_b_ref,
                     w_down_ref, g_post_ref, out_ref, u_ref, acc_ref, perm_ref):
    t = ROW_TILE
    half = t // 2
    i = pl.program_id(1)
    last = pl.num_programs(1) - 1
    row = lax.broadcasted_iota(jnp.int32, (t + 2 * HALO, 1), 0)
    starts = [sum(FF_CHUNKS[:c]) for c in range(len(FF_CHUNKS))]

    def col_starts(c):
        return (starts[c], D_FF + starts[c])

    def project(h, c, slot):
        for part, col0 in enumerate(col_starts(c)):
            up = _dot(h, w_up_ref[:, col0:col0 + FF_CHUNKS[c]])
            for j in range(FF_CHUNKS[c] // LANES):
                u_ref[2 * slot + part, j] = up[:, j * LANES:(j + 1) * LANES]

    def conv(c, slot, part):
        buf, col0 = 2 * slot + part, col_starts(c)[part]
        pieces = []
        for j in range(FF_CHUNKS[c] // LANES):
            cols = slice(col0 + j * LANES, col0 + (j + 1) * LANES)
            w0, w1, w2 = conv_w_ref[0:1, cols], conv_w_ref[1:2, cols], conv_w_ref[2:3, cols]
            bias = conv_b_ref[:, cols]
            r = [u_ref[buf, j, pl.ds(HALO - 1 + k, half, stride=2), :] for k in range(4)]
            even = r[0] * w0 + r[1] * w1 + r[2] * w2 + bias
            odd = r[1] * w0 + r[2] * w1 + r[3] * w2 + bias
            pieces.append(jnp.concatenate([even, odd], axis=0))
        return jnp.concatenate(pieces, axis=-1)

    def tile(sub, carry):
        lo = pl.multiple_of(sub * t, t)
        x = x_ref[0, pl.ds(lo, t), :]
        first, final = sub == 0, sub == FFN_SUB_TILES - 1
        inner_before = x_ref[0, pl.ds(pl.multiple_of(jnp.maximum(lo - HALO, 0), HALO), HALO), :]
        inner_after = x_ref[0, pl.ds(pl.multiple_of(jnp.minimum(lo + t, (FFN_SUB_TILES - 1) * t), HALO), HALO), :]
        before = jnp.where(first, prev_ref[0], inner_before)
        after = jnp.where(final, next_ref[0], inner_after)
        hn = _rms_norm(jnp.concatenate([before, x, after], axis=0), g_pre_ref[...])
        hn = jnp.where(jnp.logical_or(row >= HALO, jnp.logical_or(i > 0, sub > 0)), hn, 0.0)
        hn = jnp.where(jnp.logical_or(row < HALO + t, jnp.logical_or(i < last, sub < FFN_SUB_TILES - 1)), hn, 0.0)
        h = hn.astype(BF16)

        n_chunks = len(FF_CHUNKS)
        for c in range(min(FF_LOOKAHEAD, n_chunks)):
            project(h, c, c % (FF_LOOKAHEAD + 1))
        for c in range(n_chunks):
            if c + FF_LOOKAHEAD < n_chunks:
                project(h, c + FF_LOOKAHEAD, (c + FF_LOOKAHEAD) % (FF_LOOKAHEAD + 1))
            slot = c % (FF_LOOKAHEAD + 1)
            act = (_gelu_tanh(conv(c, slot, 0)) * conv(c, slot, 1)).astype(BF16)
            part = _dot(act, w_down_ref[starts[c]:starts[c] + FF_CHUNKS[c], :])
            if c == 0:
                acc_ref[...] = part
            else:
                acc_ref[...] += part
        y = _rms_norm(acc_ref[...], g_post_ref[...])
        for j in range(D_MODEL // LANES):
            perm_ref[j, pl.ds(0, half, stride=2), :] = y[:half, j * LANES:(j + 1) * LANES]
            perm_ref[j, pl.ds(1, half, stride=2), :] = y[half:, j * LANES:(j + 1) * LANES]
        y_nat = jnp.concatenate([perm_ref[j] for j in range(D_MODEL // LANES)], axis=-1)
        out_ref[0, pl.ds(lo, t), :] = x + y_nat
        return carry

    lax.fori_loop(0, FFN_SUB_TILES, tile, 0)


def _conv_ffn(x, g_pre, w_up, conv_w, conv_b, w_down, g_post):
    b, s, _ = x.shape
    t = FFN_SUB_TILES * ROW_TILE
    tiles = s // t
    halo_blocks_per_tile = t // HALO
    n_halo_blocks = s // HALO
    row_map = lambda bi, i: (bi, i, 0)
    prev_map = lambda bi, i: (bi, jnp.maximum(i * halo_blocks_per_tile - 1, 0), 0)
    next_map = lambda bi, i: (bi, jnp.minimum((i + 1) * halo_blocks_per_tile, n_halo_blocks - 1), 0)
    return pl.pallas_call(
        _conv_ffn_kernel,
        grid=(b, tiles),
        in_specs=[
            pl.BlockSpec((1, t, D_MODEL), row_map),
            pl.BlockSpec((1, HALO, D_MODEL), prev_map),
            pl.BlockSpec((1, HALO, D_MODEL), next_map),
            _const_spec((1, D_MODEL)),
            _const_spec((D_MODEL, 2 * D_FF)),
            _const_spec((3, 2 * D_FF)),
            _const_spec((1, 2 * D_FF)),
            _const_spec((D_FF, D_MODEL)),
            _const_spec((1, D_MODEL)),
        ],
        out_specs=pl.BlockSpec((1, t, D_MODEL), row_map),
        out_shape=jax.ShapeDtypeStruct((b, s, D_MODEL), F32),
        scratch_shapes=[
            pltpu.VMEM((2 * (FF_LOOKAHEAD + 1), FF_CHUNK // LANES, ROW_TILE + 2 * HALO, LANES), F32),
            pltpu.VMEM((ROW_TILE, D_MODEL), F32),
            pltpu.VMEM((D_MODEL // LANES, ROW_TILE, LANES), F32),
        ],
        compiler_params=pltpu.CompilerParams(
            dimension_semantics=("parallel", "parallel"), vmem_limit_bytes=VMEM_LIMIT_BYTES),
        name="conv_ffn",
    )(x, x, x, g_pre, w_up, conv_w, conv_b, w_down, g_post)


def _rope_tables(seq_len):
    half = HEAD_DIM // 2
    inv = ROPE_THETA ** (-jnp.arange(half, dtype=F32) / half)
    ang = jnp.arange(seq_len, dtype=F32)[:, None] * inv[None, :]
    cos, sin = jnp.cos(ang), jnp.sin(ang)
    cos_head = jnp.concatenate([cos, cos], axis=-1)
    sin_head = jnp.concatenate([-sin, sin], axis=-1)
    reps = LANES // HEAD_DIM
    return jnp.tile(cos_head, (1, reps)), jnp.tile(sin_head, (1, reps))


def _band_bias(q_rows, stacked, keys, half_window, lead):
    i = (jnp.arange(stacked * q_rows, dtype=jnp.int32) % q_rows)[None, :, None]
    j = jnp.arange(keys, dtype=jnp.int32)[None, None, :]
    e = jnp.arange(3, dtype=jnp.int32)[:, None, None]
    band = jnp.abs(j - lead - i) <= half_window
    inside = jnp.logical_and(jnp.logical_or(e != 0, j >= lead), jnp.logical_or(e != 2, j < lead + q_rows))
    return jnp.where(jnp.logical_and(band, inside), 0.0, NEG_INF).astype(F32)


def _far_band_bias():
    off = jnp.arange(3, dtype=jnp.int32)[:, None, None] * B_HALF_WINDOW
    i = (jnp.arange(2 * B_Q_BLOCK, dtype=jnp.int32) % B_Q_BLOCK)[None, :, None]
    j = jnp.arange(B_KEY_WINDOW, dtype=jnp.int32)[None, None, :]
    return jnp.where(jnp.abs(j - off - i) <= B_HALF_WINDOW, 0.0, NEG_INF).astype(F32)


def kernel(x, norm_mix_pre, w_in, sink, w_branch_a, w_branch_b, w_gate, b_gate, w_out,
           norm_mix_post, norm_ffn_pre, w_up, conv_w, conv_b, w_down, norm_ffn_post):
    b, s, d = x.shape
    assert d == D_MODEL and s % B_SPAN == 0 and s % (SUB_TILES * ROW_TILE) == 0
    cos, sin_signed = _rope_tables(s)
    bias_a = _band_bias(A_Q_BLOCK, 1, A_KEY_WINDOW, A_HALF_WINDOW, A_Q_BLOCK)
    bias_b = _band_bias(B_Q_BLOCK, 2, B_KEY_WINDOW, B_HALF_WINDOW, B_HALF_WINDOW)
    bias_b_far = _far_band_bias()
    for layer in range(norm_mix_pre.shape[0]):
        proj = _in_proj(x, norm_mix_pre[layer][None], w_in[layer].astype(BF16), cos, sin_signed)
        qa, kva, qkv_b0, qkv_b1, qkv_b2 = proj
        o2, l2, w_gate_h, w_a_h, w_b_h, w_out_h = _attn_b(
            bias_b_far, qkv_b2, B_PATTERNS[2][1],
            (w_gate[layer], w_branch_a[layer], w_branch_b[layer], w_out[layer]))
        ob2 = (o2, l2)
        x, w_up_h, w_down_h = _mix_attn(
                      x, sink[layer], bias_a, bias_b, qa, kva, qkv_b0, qkv_b1, ob2,
                      norm_mix_pre[layer][None],
                      w_gate_h, b_gate[layer][None], w_a_h, w_b_h, w_out_h, norm_mix_post[layer][None],
                      (w_up[layer], w_down[layer]))
        x = _conv_ffn(x, norm_ffn_pre[layer][None],
                      w_up_h, conv_w[layer], conv_b[layer][None], w_down_h,
                      norm_ffn_post[layer][None])
    return x
```

```python
import functools
import math

import jax
import jax.numpy as jnp
import numpy as np
from jax import lax
from jax.experimental import pallas as pl
from jax.experimental.pallas import tpu as pltpu

D_MODEL = 1024
HEAD_DIM = 64
A_Q_HEADS = 8
A_KV_HEADS = 2
A_GROUP = A_Q_HEADS // A_KV_HEADS
A_HALF_WINDOW = 128
B_PATTERNS = ((128, 1), (512, 4), (2048, 16))
B_N_GROUPS = len(B_PATTERNS)
B_HEADS = 4
HEAD_ORDER = (0, 2, 1, 3)
B_HALF_WINDOW = 64
ROPE_THETA = 10000.0
D_FF = 3 * D_MODEL
RMS_EPS = 1e-6
NEG_INF = -1e30
LOG2_E = math.log2(math.e)

A_Q_COLS = A_Q_HEADS * HEAD_DIM
A_KV_COLS = A_KV_HEADS * HEAD_DIM
A_COLS = A_Q_COLS + 2 * A_KV_COLS
B_GROUP_COLS = B_HEADS * HEAD_DIM
B_PROJ_COLS = B_N_GROUPS * B_GROUP_COLS
IN_COLS = A_COLS + 3 * B_PROJ_COLS
A_KVA_COLS = 4 * A_KV_COLS
B_QKV_COLS = 3 * B_GROUP_COLS

LANES = 128
VMEM_LIMIT_BYTES = 56 * 1024 * 1024

ROW_TILE = 512
SUB_TILES = 2
FFN_SUB_TILES = 2
A_Q_BLOCK = 128
A_KEY_WINDOW = 3 * A_Q_BLOCK
B_Q_BLOCK = 128
B_KEY_WINDOW = B_Q_BLOCK + 2 * B_HALF_WINDOW
B_SPAN = 2048
B_UNROLL = 8
FF_CHUNK = 512
FF_CHUNKS = (FF_CHUNK,) * (D_FF // FF_CHUNK)
FF_LOOKAHEAD = 2
HALO = 8

BF16 = jnp.bfloat16
F32 = jnp.float32


def _dot(a, b):
    return jnp.dot(a, b, preferred_element_type=F32)


def _dot_nt(a, b):
    return lax.dot_general(a, b, (((1,), (1,)), ((), ())), preferred_element_type=F32)


def _rms_norm(x, gain):
    ms = jnp.mean(x * x, axis=-1, keepdims=True)
    return x * lax.rsqrt(ms + RMS_EPS) * gain


def _rope_chunk(p, cos, sin_signed, first_half):
    partner = jnp.where(first_half, pltpu.roll(p, 96, 1), pltpu.roll(p, 32, 1))
    return p * cos + partner * sin_signed


def _const_spec(shape):
    nd = len(shape)
    return pl.BlockSpec(shape, lambda *_: (0,) * nd, pipeline_mode=pl.Buffered(1))


def _row_chunk_specs(weights, steps_per_seq, n_steps):
    return [pl.BlockSpec((w.shape[0] // n_steps, w.shape[1]), lambda bi, i: (bi * steps_per_seq + i, 0))
            for w in weights]


def _cast_chunks(src_refs, dst_refs):
    for src_ref, dst_ref in zip(src_refs, dst_refs):
        dst_ref[...] = src_ref[...].astype(BF16)


def _in_proj_kernel(x_ref, gain_ref, w_ref, cos_ref, sin_ref,
                    qa_ref, kva_ref, qkv0_ref, qkv1_ref, qkv2_ref, slab_ref):
    t = ROW_TILE
    lane = lax.broadcasted_iota(jnp.int32, (t, LANES), 1)
    first_half = (lane % HEAD_DIM) < (HEAD_DIM // 2)
    low_head = lane < HEAD_DIM
    scale = HEAD_DIM ** -0.5 * LOG2_E
    out_refs = (qkv0_ref, qkv1_ref, qkv2_ref)

    def tile(sub):
        rows = slice(sub * t, (sub + 1) * t)
        h = _rms_norm(x_ref[0, rows, :], gain_ref[...]).astype(BF16)
        cos = cos_ref[rows, :]
        sin_signed = sin_ref[rows, :]

        def proj(col0, ncols):
            return _dot(h, w_ref[:, col0:col0 + ncols])

        def rope(p, mult=None):
            chunks = []
            for c in range(p.shape[1] // LANES):
                r = _rope_chunk(p[:, c * LANES:(c + 1) * LANES], cos, sin_signed, first_half)
                chunks.append(r if mult is None else r * mult)
            return chunks

        def dup_heads(p):
            swapped = pltpu.roll(p, HEAD_DIM, 1)
            return [jnp.where(low_head, p, swapped), jnp.where(low_head, swapped, p)]

        def windowed():
            for c, chunk in enumerate(rope(proj(0, A_Q_COLS), scale)):
                qa_ref[0, rows, c * LANES:(c + 1) * LANES] = chunk.astype(BF16)
            kv = proj(A_Q_COLS, 2 * A_KV_COLS)
            ka = rope(kv[:, :A_KV_COLS])[0]
            for c, chunk in enumerate(dup_heads(ka) + dup_heads(kv[:, A_KV_COLS:])):
                kva_ref[0, rows, c * LANES:(c + 1) * LANES] = chunk.astype(BF16)

        def dilated(g):
            dil = B_PATTERNS[g][1]
            for kind in range(3):
                col0 = A_COLS + kind * B_PROJ_COLS + g * B_GROUP_COLS
                p = proj(col0, B_GROUP_COLS)
                if kind == 0:
                    chunks = rope(p, scale)
                elif kind == 1:
                    chunks = rope(p)
                else:
                    chunks = [p[:, c * LANES:(c + 1) * LANES] for c in range(B_GROUP_COLS // LANES)]
                o_ref = out_refs[g]
                sub_rows = slice(sub * t // dil, (sub + 1) * t // dil)
                col = lambda c: slice(kind * B_GROUP_COLS + c * LANES, kind * B_GROUP_COLS + (c + 1) * LANES)
                if dil == 1:
                    for c, chunk in enumerate(chunks):
                        o_ref[0, 0, sub_rows, col(c)] = chunk.astype(BF16)
                else:
                    for c, chunk in enumerate(chunks):
                        slab_ref[sub, kind, c] = chunk
                    for c in range(len(chunks)):
                        for res in range(dil):
                            picked = slab_ref[sub, kind, c, pl.ds(res, t // dil, stride=dil), :]
                            o_ref[0, res, sub_rows, col(c)] = picked.astype(BF16)

        dilated(2)
        dilated(1)
        windowed()
        dilated(0)

    for sub in range(SUB_TILES):
        tile(sub)


def _in_proj(x, gain, w_in, cos, sin_signed):
    b, s, _ = x.shape
    t = SUB_TILES * ROW_TILE
    grid = (b, s // t)
    row_map = lambda bi, i: (bi, i, 0)
    out_shape = [
        jax.ShapeDtypeStruct((b, s, A_Q_COLS), BF16),
        jax.ShapeDtypeStruct((b, s, A_KVA_COLS), BF16),
    ]
    out_specs = [
        pl.BlockSpec((1, t, A_Q_COLS), row_map),
        pl.BlockSpec((1, t, A_KVA_COLS), row_map),
    ]
    for _, dil in B_PATTERNS:
        out_shape.append(jax.ShapeDtypeStruct((b, dil, s // dil, B_QKV_COLS), BF16))
        out_specs.append(pl.BlockSpec((1, dil, t // dil, B_QKV_COLS), lambda bi, i: (bi, 0, i, 0)))
    return pl.pallas_call(
        _in_proj_kernel,
        grid=grid,
        in_specs=[
            pl.BlockSpec((1, t, D_MODEL), row_map),
            _const_spec((1, D_MODEL)),
            _const_spec((D_MODEL, IN_COLS)),
            pl.BlockSpec((t, LANES), lambda bi, i: (i, 0)),
            pl.BlockSpec((t, LANES), lambda bi, i: (i, 0)),
        ],
        out_specs=out_specs,
        out_shape=out_shape,
        scratch_shapes=[pltpu.VMEM((SUB_TILES, 3, B_GROUP_COLS // LANES, ROW_TILE, LANES), F32)],
        compiler_params=pltpu.CompilerParams(
            dimension_semantics=("parallel", "parallel"), vmem_limit_bytes=VMEM_LIMIT_BYTES),
        name="in_proj",
    )(x, gain, w_in, cos, sin_signed)


def _attn_b_kernel(bias_ref, q_ref, kv_ref, *rest, dil, n_cast):
    o_ref, lse_ref = rest[n_cast:n_cast + 2]
    _cast_chunks(rest[:n_cast], rest[n_cast + 2:])
    sub_len = kv_ref.shape[2]
    sub_rows = q_ref.shape[2]
    blocks = sub_rows // B_Q_BLOCK
    step = pl.program_id(1)
    low_head = lax.broadcasted_iota(jnp.int32, (B_Q_BLOCK, LANES), 1) < HEAD_DIM
    low_key = lax.broadcasted_iota(jnp.int32, (B_KEY_WINDOW, LANES), 1) < HEAD_DIM

    def units(it, carry):
        work = []
        for uu in range(B_UNROLL):
            u = it * B_UNROLL + uu
            res = u // blocks
            r0 = (u % blocks) * B_Q_BLOCK
            q_pos = (step * sub_rows + r0) // B_HALF_WINDOW
            start_pos = jnp.clip(q_pos - 1, 0, (sub_len - B_KEY_WINDOW) // B_HALF_WINDOW)
            start = start_pos * B_HALF_WINDOW
            bias = bias_ref[q_pos - start_pos]
            q_blk = q_ref[0, res, pl.ds(r0, B_Q_BLOCK), 0:B_GROUP_COLS]
            k_win = kv_ref[0, res, pl.ds(start, B_KEY_WINDOW), B_GROUP_COLS:2 * B_GROUP_COLS]
            v_win = kv_ref[0, res, pl.ds(start, B_KEY_WINDOW), 2 * B_GROUP_COLS:3 * B_GROUP_COLS]
            if dil == 1:
                rows = pl.ds(r0, B_Q_BLOCK)
            else:
                rows = pl.ds(r0 * dil + res, B_Q_BLOCK, stride=dil)
            for pr in range(B_HEADS // 2):
                qp = q_blk[:, pr * LANES:(pr + 1) * LANES]
                zero = jnp.zeros_like(qp)
                lhs = jnp.concatenate([jnp.where(low_head, qp, zero), jnp.where(low_head, zero, qp)], axis=0)
                sc = _dot_nt(lhs, k_win[:, pr * LANES:(pr + 1) * LANES])
                work.append((rows, pr, bias, v_win[:, pr * LANES:(pr + 1) * LANES], sc))

        soft = []
        for rows, pr, bias, vp, sc in work:
            sc = sc + bias
            m = jnp.max(sc, axis=-1, keepdims=True)
            soft.append((rows, pr, vp, m, jnp.exp2(sc - m).astype(BF16)))

        for rows, pr, vp, m, p in soft:
            one = jnp.ones_like(vp)
            top = _dot(p[:B_Q_BLOCK], jnp.where(low_key, vp, one))
            bot = _dot(p[B_Q_BLOCK:], jnp.where(low_key, one, vp))
            denom = pltpu.roll(jnp.where(low_head, bot, top), HEAD_DIM, 1)
            o_ref[0, pr, rows, :] = jnp.where(low_head, top, bot) / denom
            lse_ref[0, pr, rows, :] = jnp.where(low_head, m[:B_Q_BLOCK], m[B_Q_BLOCK:]) + jnp.log2(denom)
        return carry

    lax.fori_loop(0, dil * blocks // B_UNROLL, units, 0)


def _attn_b(bias, qkv, dil, cast_weights=()):
    b, _, sub_len, _ = qkv.shape
    s = sub_len * dil
    sub_rows = B_SPAN // dil
    nslab = B_GROUP_COLS // LANES
    out_sds = jax.ShapeDtypeStruct((b, nslab, s, LANES), F32)
    out_spec = pl.BlockSpec((1, nslab, B_SPAN, LANES), lambda bi, i: (bi, 0, i, 0))
    steps_per_seq = s // B_SPAN
    chunk_specs = _row_chunk_specs(cast_weights, steps_per_seq, b * steps_per_seq)
    return pl.pallas_call(
        functools.partial(_attn_b_kernel, dil=dil, n_cast=len(cast_weights)),
        grid=(b, steps_per_seq),
        in_specs=[
            _const_spec(bias.shape),
            pl.BlockSpec((1, dil, sub_rows, B_QKV_COLS), lambda bi, i: (bi, 0, i, 0)),
            pl.BlockSpec((1, dil, sub_len, B_QKV_COLS), lambda bi, i: (bi, 0, 0, 0)),
        ] + chunk_specs,
        out_specs=[out_spec, out_spec] + chunk_specs,
        out_shape=[out_sds, out_sds] + [jax.ShapeDtypeStruct(w.shape, BF16) for w in cast_weights],
        compiler_params=pltpu.CompilerParams(
            dimension_semantics=("parallel", "parallel"), vmem_limit_bytes=VMEM_LIMIT_BYTES),
        name=f"attn_b_d{dil}",
    )(bias, qkv, qkv, *cast_weights)


def _window(main_ref, prev_ref, next_ref, lead, lo, hi, length, cols):
    parts = []
    if lo < 0:
        halo = prev_ref.shape[-2]
        parts.append(prev_ref[lead + (slice(halo + lo, halo), cols)])
        lo = 0
    parts.append(main_ref[lead + (slice(lo, min(hi, length)), cols)])
    if hi > length:
        parts.append(next_ref[lead + (slice(0, hi - length), cols)])
    return parts[0] if len(parts) == 1 else jnp.concatenate(parts, axis=0)


def _mix_attn_kernel(sink_ref, bias_a_ref, bias_b_ref, x_ref,
                     qa_ref, kva_ref, kva_p_ref, kva_n_ref,
                     b0_ref, b0_p_ref, b0_n_ref, b1_ref, b1_p_ref, b1_n_ref,
                     o2_ref, l2_ref,
                     g_pre_ref, w_gate_ref, b_gate_ref, w_a_ref, w_b_ref, w_out_ref, g_post_ref,
                     *rest, n_cast):
    out_ref = rest[n_cast]
    ya_ref, ob_ref = rest[2 * n_cast + 1:]
    _cast_chunks(rest[:n_cast], rest[n_cast + 1:2 * n_cast + 1])
    t = ROW_TILE
    step_rows = t
    i = pl.program_id(1)
    last = pl.num_programs(1) - 1
    low_head = lax.broadcasted_iota(jnp.int32, (A_Q_BLOCK, LANES), 1) < HEAD_DIM
    low_key_a = lax.broadcasted_iota(jnp.int32, (A_KEY_WINDOW, LANES), 1) < HEAD_DIM
    low_key_b = lax.broadcasted_iota(jnp.int32, (B_KEY_WINDOW, LANES), 1) < HEAD_DIM
    dil1 = B_PATTERNS[1][1]
    sub_rows = step_rows // dil1

    def edge_index(first, final):
        idx = 1
        if first:
            idx = jnp.where(i == 0, 0, idx)
        if final:
            idx = jnp.where(i == last, 2, idx)
        return idx

    def scores(grp_idx):
        work_a, work_b = [], []
        n = grp_idx
        lo, hi = (n - 1) * A_Q_BLOCK, (n + 2) * A_Q_BLOCK
        bias = bias_a_ref[edge_index(n == 0, hi > step_rows)]
        q_blk = qa_ref[0, n * A_Q_BLOCK:(n + 1) * A_Q_BLOCK, :]
        for j in range(A_KV_HEADS):
            kcols = slice(j * LANES, (j + 1) * LANES)
            vcols = slice(2 * A_KV_COLS + j * LANES, 2 * A_KV_COLS + (j + 1) * LANES)
            kd = _window(kva_ref, kva_p_ref, kva_n_ref, (0,), lo, hi, step_rows, kcols)
            vd = _window(kva_ref, kva_p_ref, kva_n_ref, (0,), lo, hi, step_rows, vcols)
            lhs = []
            for g in HEAD_ORDER:
                c0 = j * A_GROUP * HEAD_DIM + (g // 2) * LANES
                pair = q_blk[:, c0:c0 + LANES]
                keep = low_head if g % 2 == 0 else jnp.logical_not(low_head)
                lhs.append(jnp.where(keep, pair, jnp.zeros_like(pair)))
            sc = _dot_nt(jnp.concatenate(lhs, axis=0), kd)
            work_a.append((n, j, bias, vd, sc))
        units = [(0, (0, 0), grp_idx, step_rows), (1, (0, grp_idx), 0, sub_rows)]
        for grp, lead, n, length in units:
            m_ref, p_ref, n_ref = (b0_ref, b0_p_ref, b0_n_ref) if grp == 0 else (b1_ref, b1_p_ref, b1_n_ref)
            lo, hi = n * B_Q_BLOCK - B_HALF_WINDOW, (n + 1) * B_Q_BLOCK + B_HALF_WINDOW
            bias = bias_b_ref[edge_index(lo < 0, hi > length)]
            if grp == 0:
                rows = pl.ds(n * B_Q_BLOCK, B_Q_BLOCK)
            else:
                rows = pl.ds(n * B_Q_BLOCK * dil1 + lead[1], B_Q_BLOCK, stride=dil1)
            for pr in range(B_HEADS // 2):
                cols = [slice(kind * B_GROUP_COLS + pr * LANES, kind * B_GROUP_COLS + (pr + 1) * LANES)
                        for kind in range(3)]
                qp = m_ref[lead + (slice(n * B_Q_BLOCK, (n + 1) * B_Q_BLOCK), cols[0])]
                kp = _window(m_ref, p_ref, n_ref, lead, lo, hi, length, cols[1])
                vp = _window(m_ref, p_ref, n_ref, lead, lo, hi, length, cols[2])
                zero = jnp.zeros_like(qp)
                lhs = jnp.concatenate([jnp.where(low_head, qp, zero), jnp.where(low_head, zero, qp)], axis=0)
                work_b.append((grp, rows, pr, bias, vp, _dot_nt(lhs, kp)))
        return work_a, work_b

    def softmaxes(work):
        work_a, work_b = work
        soft_a, soft_b = [], []
        for n, j, bias, vd, sc in work_a:
            probs, sink_terms = {}, {}
            for slot, g in enumerate(HEAD_ORDER):
                sg = sc[slot * A_Q_BLOCK:(slot + 1) * A_Q_BLOCK] + bias
                sink = sink_ref[j * A_GROUP + g] * LOG2_E
                m = jnp.maximum(jnp.max(sg, axis=-1, keepdims=True), sink)
                probs[g] = jnp.exp2(sg - m).astype(BF16)
                sink_terms[g] = jnp.exp2(sink - m)
            soft_a.append((n, j, vd, probs, sink_terms))
        for grp, rows, pr, bias, vp, sc in work_b:
            sc = sc + bias
            m = jnp.max(sc, axis=-1, keepdims=True)
            soft_b.append((grp, rows, pr, vp, m, jnp.exp2(sc - m).astype(BF16)))
        return soft_a, soft_b

    def values(soft):
        soft_a, soft_b = soft
        pairs = range(A_GROUP // 2)
        for n, j, vd, probs, sink_terms in soft_a:
            one = jnp.ones_like(vd)
            r_even = _dot(jnp.concatenate([probs[2 * pr] for pr in pairs], axis=0),
                          jnp.where(low_key_a, vd, one))
            r_odd = _dot(jnp.concatenate([probs[2 * pr + 1] for pr in pairs], axis=0),
                         jnp.where(low_key_a, one, vd))
            for pr in pairs:
                even = r_even[pr * A_Q_BLOCK:(pr + 1) * A_Q_BLOCK]
                odd = r_odd[pr * A_Q_BLOCK:(pr + 1) * A_Q_BLOCK]
                num = jnp.where(low_head, even, odd)
                den = jnp.where(low_head, odd + sink_terms[2 * pr + 1], even + sink_terms[2 * pr])
                c0 = j * A_GROUP * HEAD_DIM + pr * LANES
                ya_ref[n * A_Q_BLOCK:(n + 1) * A_Q_BLOCK, c0:c0 + LANES] = (
                    num / pltpu.roll(den, HEAD_DIM, 1)).astype(BF16)
        for grp, rows, pr, vp, m, p in soft_b:
            one = jnp.ones_like(vp)
            top = _dot(p[:B_Q_BLOCK], jnp.where(low_key_b, vp, one))
            bot = _dot(p[B_Q_BLOCK:], jnp.where(low_key_b, one, vp))
            denom = pltpu.roll(jnp.where(low_head, bot, top), HEAD_DIM, 1)
            ob_ref[grp, 0, pr, rows, :] = jnp.where(low_head, top, bot) / denom
            ob_ref[grp, 1, pr, rows, :] = jnp.where(low_head, m[:B_Q_BLOCK], m[B_Q_BLOCK:]) + jnp.log2(denom)

    n_groups = t // A_Q_BLOCK
    assert step_rows == t and n_groups == dil1 == t // B_Q_BLOCK
    gate_chunk = 2 * D_MODEL // n_groups
    x = x_ref[0]
    h = _rms_norm(x, g_pre_ref[...]).astype(BF16)

    def gate_logits(j):
        cols = slice(j * gate_chunk, (j + 1) * gate_chunk)
        return _dot(h, w_gate_ref[:, cols]) + b_gate_ref[:, cols]

    z = [None] * n_groups
    work = scores(0)
    z[0] = gate_logits(0)
    soft = softmaxes(work)
    work = scores(1)
    values(soft)
    z[1] = gate_logits(1)
    soft = softmaxes(work)
    work = scores(2)
    values(soft)
    soft = softmaxes(work)
    work = scores(3)
    values(soft)
    z[2] = gate_logits(2)
    z[3] = gate_logits(3)
    values(softmaxes(work))

    yb = []
    for c in range(B_GROUP_COLS // LANES):
        l0, l1, l2 = ob_ref[0, 1, c], ob_ref[1, 1, c], l2_ref[0, c]
        m = jnp.maximum(jnp.maximum(l0, l1), l2)
        e0, e1, e2 = jnp.exp2(l0 - m), jnp.exp2(l1 - m), jnp.exp2(l2 - m)
        num = e0 * ob_ref[0, 0, c] + e1 * ob_ref[1, 0, c] + e2 * o2_ref[0, c]
        yb.append((num / (e0 + e1 + e2)).astype(BF16))
    yb = jnp.concatenate(yb, axis=-1)
    gates = jax.nn.sigmoid(jnp.concatenate(z, axis=-1))
    merged = (gates[:, :D_MODEL] * _dot(ya_ref[...], w_a_ref[...])
              + gates[:, D_MODEL:] * _dot(yb, w_b_ref[...]))
    mix = _dot(merged.astype(BF16), w_out_ref[...])
    out_ref[0] = x + _rms_norm(mix, g_post_ref[...])


def _halo_specs(lead_blocks, rows, halo, total_rows, cols):
    nlead = len(lead_blocks)
    per = rows // halo
    nhalo = total_rows // halo
    zeros = (0,) * nlead
    main = pl.BlockSpec((1,) + lead_blocks + (rows, cols), lambda bi, i: (bi,) + zeros + (i, 0))
    prev = pl.BlockSpec((1,) + lead_blocks + (halo, cols),
                        lambda bi, i: (bi,) + zeros + (jnp.maximum(i * per - 1, 0), 0))
    nxt = pl.BlockSpec((1,) + lead_blocks + (halo, cols),
                       lambda bi, i: (bi,) + zeros + (jnp.minimum((i + 1) * per, nhalo - 1), 0))
    return [main, prev, nxt]


def _mix_attn(x, sink, bias_a, bias_b, qa, kva, qkv_b0, qkv_b1, ob2,
              g_pre, w_gate, b_gate, w_a, w_b, w_out, g_post, cast_weights=()):
    b, s, _ = x.shape
    t = ROW_TILE
    dil1 = B_PATTERNS[1][1]
    nslab = B_GROUP_COLS // LANES
    row_map = lambda bi, i: (bi, i, 0)
    slab_spec = pl.BlockSpec((1, nslab, t, LANES), lambda bi, i: (bi, 0, i, 0))
    chunk_specs = _row_chunk_specs(cast_weights, s // t, b * s // t)
    in_specs = ([pl.BlockSpec(memory_space=pltpu.SMEM), _const_spec(bias_a.shape), _const_spec(bias_b.shape),
                 pl.BlockSpec((1, t, D_MODEL), row_map),
                 pl.BlockSpec((1, t, A_Q_COLS), row_map)]
                + _halo_specs((), t, A_Q_BLOCK, s, A_KVA_COLS)
                + _halo_specs((1,), t, B_HALF_WINDOW, s, B_QKV_COLS)
                + _halo_specs((dil1,), t // dil1, B_HALF_WINDOW, s // dil1, B_QKV_COLS)
                + [slab_spec, slab_spec,
                   _const_spec((1, D_MODEL)),
                   _const_spec((D_MODEL, 2 * D_MODEL)),
                   _const_spec((1, 2 * D_MODEL)),
                   _const_spec((A_Q_COLS, D_MODEL)),
                   _const_spec((B_GROUP_COLS, D_MODEL)),
                   _const_spec((D_MODEL, D_MODEL)),
                   _const_spec((1, D_MODEL))])
    return pl.pallas_call(
        functools.partial(_mix_attn_kernel, n_cast=len(cast_weights)),
        grid=(b, s // t),
        in_specs=in_specs + chunk_specs,
        out_specs=[pl.BlockSpec((1, t, D_MODEL), row_map)] + chunk_specs,
        out_shape=[jax.ShapeDtypeStruct((b, s, D_MODEL), F32)]
        + [jax.ShapeDtypeStruct(w.shape, BF16) for w in cast_weights],
        scratch_shapes=[
            pltpu.VMEM((t, A_Q_COLS), BF16),
            pltpu.VMEM((2, 2, nslab, t, LANES), F32),
        ],
        compiler_params=pltpu.CompilerParams(
            dimension_semantics=("parallel", "parallel"), vmem_limit_bytes=VMEM_LIMIT_BYTES),
        name="mix_attn",
    )(sink, bias_a, bias_b, x, qa, kva, kva, kva, qkv_b0, qkv_b0, qkv_b0, qkv_b1, qkv_b1, qkv_b1,
      ob2[0], ob2[1], g_pre, w_gate, b_gate, w_a, w_b, w_out, g_post, *cast_weights)


def _gelu_tanh(x):
    c = -2.0 * math.sqrt(2.0 / math.pi)
    return x / (1.0 + jnp.exp(x * (c + (c * 0.044715) * (x * x))))


def _conv_ffn_kernel(x_ref, prev_ref, next_ref, g_pre_ref, w_up_ref, conv_w_ref, conv_b_ref,
                     w_down_ref, g_post_ref, out_ref, u_ref, acc_ref, perm_ref):
    t = ROW_TILE
    half = t // 2
    i = pl.program_id(1)
    last = pl.num_programs(1) - 1
    row = lax.broadcasted_iota(jnp.int32, (t + 2 * HALO, 1), 0)
    starts = [sum(FF_CHUNKS[:c]) for c in range(len(FF_CHUNKS))]

    def col_starts(c):
        return (starts[c], D_FF + starts[c])

    def project(h, c, slot):
        for part, col0 in enumerate(col_starts(c)):
            up = _dot(h, w_up_ref[:, col0:col0 + FF_CHUNKS[c]])
            for j in range(FF_CHUNKS[c] // LANES):
                u_ref[2 * slot + part, j] = up[:, j * LANES:(j + 1) * LANES]

    def conv(c, slot, part):
        buf, col0 = 2 * slot + part, col_starts(c)[part]
        pieces = []
        for j in range(FF_CHUNKS[c] // LANES):
            cols = slice(col0 + j * LANES, col0 + (j + 1) * LANES)
            w0, w1, w2 = conv_w_ref[0:1, cols], conv_w_ref[1:2, cols], conv_w_ref[2:3, cols]
            bias = conv_b_ref[:, cols]
            r = [u_ref[buf, j, pl.ds(HALO - 1 + k, half, stride=2), :] for k in range(4)]
            even = r[0] * w0 + r[1] * w1 + r[2] * w2 + bias
            odd = r[1] * w0 + r[2] * w1 + r[3] * w2 + bias
            pieces.append(jnp.concatenate([even, odd], axis=0))
        return jnp.concatenate(pieces, axis=-1)

    def tile(sub, carry):
        lo = pl.multiple_of(sub * t, t)
        x = x_ref[0, pl.ds(lo, t), :]
        first, final = sub == 0, sub == FFN_SUB_TILES - 1
        inner_before = x_ref[0, pl.ds(pl.multiple_of(jnp.maximum(lo - HALO, 0), HALO), HALO), :]
        inner_after = x_ref[0, pl.ds(pl.multiple_of(jnp.minimum(lo + t, (FFN_SUB_TILES - 1) * t), HALO), HALO), :]
        before = jnp.where(first, prev_ref[0], inner_before)
        after = jnp.where(final, next_ref[0], inner_after)
        hn = _rms_norm(jnp.concatenate([before, x, after], axis=0), g_pre_ref[...])
        hn = jnp.where(jnp.logical_or(row >= HALO, jnp.logical_or(i > 0, sub > 0)), hn, 0.0)
        hn = jnp.where(jnp.logical_or(row < HALO + t, jnp.logical_or(i < last, sub < FFN_SUB_TILES - 1)), hn, 0.0)
        h = hn.astype(BF16)

        n_chunks = len(FF_CHUNKS)
        for c in range(min(FF_LOOKAHEAD, n_chunks)):
            project(h, c, c % (FF_LOOKAHEAD + 1))
        for c in range(n_chunks):
            if c + FF_LOOKAHEAD < n_chunks:
                project(h, c + FF_LOOKAHEAD, (c + FF_LOOKAHEAD) % (FF_LOOKAHEAD + 1))
            slot = c % (FF_LOOKAHEAD + 1)
            act = (_gelu_tanh(conv(c, slot, 0)) * conv(c, slot, 1)).astype(BF16)
            part = _dot(act, w_down_ref[starts[c]:starts[c] + FF_CHUNKS[c], :])
            if c == 0:
                acc_ref[...] = part
            else:
                acc_ref[...] += part
        y = _rms_norm(acc_ref[...], g_post_ref[...])
        for j in range(D_MODEL // LANES):
            perm_ref[j, pl.ds(0, half, stride=2), :] = y[:half, j * LANES:(j + 1) * LANES]
            perm_ref[j, pl.ds(1, half, stride=2), :] = y[half:, j * LANES:(j + 1) * LANES]
        y_nat = jnp.concatenate([perm_ref[j] for j in range(D_MODEL // LANES)], axis=-1)
        out_ref[0, pl.ds(lo, t), :] = x + y_nat
        return carry

    lax.fori_loop(0, FFN_SUB_TILES, tile, 0)


def _conv_ffn(x, g_pre, w_up, conv_w, conv_b, w_down, g_post):
    b, s, _ = x.shape
    t = FFN_SUB_TILES * ROW_TILE
    tiles = s // t
    halo_blocks_per_tile = t // HALO
    n_halo_blocks = s // HALO
    row_map = lambda bi, i: (bi, i, 0)
    prev_map = lambda bi, i: (bi, jnp.maximum(i * halo_blocks_per_tile - 1, 0), 0)
    next_map = lambda bi, i: (bi, jnp.minimum((i + 1) * halo_blocks_per_tile, n_halo_blocks - 1), 0)
    return pl.pallas_call(
        _conv_ffn_kernel,
        grid=(b, tiles),
        in_specs=[
            pl.BlockSpec((1, t, D_MODEL), row_map),
            pl.BlockSpec((1, HALO, D_MODEL), prev_map),
            pl.BlockSpec((1, HALO, D_MODEL), next_map),
            _const_spec((1, D_MODEL)),
            _const_spec((D_MODEL, 2 * D_FF)),
            _const_spec((3, 2 * D_FF)),
            _const_spec((1, 2 * D_FF)),
            _const_spec((D_FF, D_MODEL)),
            _const_spec((1, D_MODEL)),
        ],
        out_specs=pl.BlockSpec((1, t, D_MODEL), row_map),
        out_shape=jax.ShapeDtypeStruct((b, s, D_MODEL), F32),
        scratch_shapes=[
            pltpu.VMEM((2 * (FF_LOOKAHEAD + 1), FF_CHUNK // LANES, ROW_TILE + 2 * HALO, LANES), F32),
            pltpu.VMEM((ROW_TILE, D_MODEL), F32),
            pltpu.VMEM((D_MODEL // LANES, ROW_TILE, LANES), F32),
        ],
        compiler_params=pltpu.CompilerParams(
            dimension_semantics=("parallel", "parallel"), vmem_limit_bytes=VMEM_LIMIT_BYTES),
        name="conv_ffn",
    )(x, x, x, g_pre, w_up, conv_w, conv_b, w_down, g_post)


def _rope_tables(seq_len):
    half = HEAD_DIM // 2
    inv = ROPE_THETA ** (-np.arange(half, dtype=np.float64) / half)
    ang = np.arange(seq_len, dtype=np.float64)[:, None] * inv[None, :]
    cos, sin = np.cos(ang), np.sin(ang)
    cos_head = np.concatenate([cos, cos], axis=-1)
    sin_head = np.concatenate([-sin, sin], axis=-1)
    reps = LANES // HEAD_DIM
    return (jnp.asarray(np.tile(cos_head, (1, reps)), dtype=F32),
            jnp.asarray(np.tile(sin_head, (1, reps)), dtype=F32))


def _band_bias(q_rows, stacked, keys, half_window, lead):
    i = (np.arange(stacked * q_rows) % q_rows)[None, :, None]
    j = np.arange(keys)[None, None, :]
    e = np.arange(3)[:, None, None]
    band = np.abs(j - lead - i) <= half_window
    inside = np.logical_and(np.logical_or(e != 0, j >= lead), np.logical_or(e != 2, j < lead + q_rows))
    return jnp.asarray(np.where(np.logical_and(band, inside), 0.0, NEG_INF), dtype=F32)


def _far_band_bias():
    off = np.arange(3)[:, None, None] * B_HALF_WINDOW
    i = (np.arange(2 * B_Q_BLOCK) % B_Q_BLOCK)[None, :, None]
    j = np.arange(B_KEY_WINDOW)[None, None, :]
    return jnp.asarray(np.where(np.abs(j - off - i) <= B_HALF_WINDOW, 0.0, NEG_INF), dtype=F32)


def kernel(x, norm_mix_pre, w_in, sink, w_branch_a, w_branch_b, w_gate, b_gate, w_out,
           norm_mix_post, norm_ffn_pre, w_up, conv_w, conv_b, w_down, norm_ffn_post):
    b, s, d = x.shape
    assert d == D_MODEL and s % B_SPAN == 0 and s % (SUB_TILES * ROW_TILE) == 0
    cos, sin_signed = _rope_tables(s)
    bias_a = _band_bias(A_Q_BLOCK, 1, A_KEY_WINDOW, A_HALF_WINDOW, A_Q_BLOCK)
    bias_b = _band_bias(B_Q_BLOCK, 2, B_KEY_WINDOW, B_HALF_WINDOW, B_HALF_WINDOW)
    bias_b_far = _far_band_bias()
    for layer in range(norm_mix_pre.shape[0]):
        proj = _in_proj(x, norm_mix_pre[layer][None], w_in[layer].astype(BF16), cos, sin_signed)
        qa, kva, qkv_b0, qkv_b1, qkv_b2 = proj
        o2, l2, w_gate_h, w_a_h, w_b_h, w_out_h = _attn_b(
            bias_b_far, qkv_b2, B_PATTERNS[2][1],
            (w_gate[layer], w_branch_a[layer], w_branch_b[layer], w_out[layer]))
        ob2 = (o2, l2)
        x, w_up_h, w_down_h = _mix_attn(
                      x, sink[layer], bias_a, bias_b, qa, kva, qkv_b0, qkv_b1, ob2,
                      norm_mix_pre[layer][None],
                      w_gate_h, b_gate[layer][None], w_a_h, w_b_h, w_out_h, norm_mix_post[layer][None],
                      (w_up[layer], w_down[layer]))
        x = _conv_ffn(x, norm_ffn_pre[layer][None],
                      w_up_h, conv_w[layer], conv_b[layer][None], w_down_h,
                      norm_ffn_post[layer][None])
    return x
```

```python
import functools
import math

import jax
import jax.numpy as jnp
import numpy as np
from jax import lax
from jax.experimental import pallas as pl
from jax.experimental.pallas import tpu as pltpu

D_MODEL = 1024
HEAD_DIM = 64
A_Q_HEADS = 8
A_KV_HEADS = 2
A_GROUP = A_Q_HEADS // A_KV_HEADS
A_HALF_WINDOW = 128
B_PATTERNS = ((128, 1), (512, 4), (2048, 16))
B_N_GROUPS = len(B_PATTERNS)
B_HEADS = 4
HEAD_ORDER = (0, 2, 1, 3)
B_HALF_WINDOW = 64
ROPE_THETA = 10000.0
D_FF = 3 * D_MODEL
RMS_EPS = 1e-6
NEG_INF = -1e30
LOG2_E = math.log2(math.e)

A_Q_COLS = A_Q_HEADS * HEAD_DIM
A_KV_COLS = A_KV_HEADS * HEAD_DIM
A_COLS = A_Q_COLS + 2 * A_KV_COLS
B_GROUP_COLS = B_HEADS * HEAD_DIM
B_PROJ_COLS = B_N_GROUPS * B_GROUP_COLS
IN_COLS = A_COLS + 3 * B_PROJ_COLS
A_KVA_COLS = 4 * A_KV_COLS
B_QKV_COLS = 3 * B_GROUP_COLS

LANES = 128
VMEM_LIMIT_BYTES = 56 * 1024 * 1024

ROW_TILE = 512
SUB_TILES = 2
FFN_SUB_TILES = 2
A_Q_BLOCK = 128
A_KEY_WINDOW = 3 * A_Q_BLOCK
B_Q_BLOCK = 128
B_KEY_WINDOW = B_Q_BLOCK + 2 * B_HALF_WINDOW
B_SPAN = 2048
B_UNROLL = 8
FF_CHUNK = 512
FF_CHUNKS = (FF_CHUNK,) * (D_FF // FF_CHUNK)
FF_LOOKAHEAD = 2
HALO = 8
DEINTERLEAVE_STRIDE = 4

BF16 = jnp.bfloat16
F32 = jnp.float32


def _dot(a, b):
    return jnp.dot(a, b, preferred_element_type=F32)


def _dot_nt(a, b):
    return lax.dot_general(a, b, (((1,), (1,)), ((), ())), preferred_element_type=F32)


def _rms_norm(x, gain):
    ms = jnp.mean(x * x, axis=-1, keepdims=True)
    return x * lax.rsqrt(ms + RMS_EPS) * gain


def _rope_chunk(p, cos, sin_signed, first_half):
    partner = jnp.where(first_half, pltpu.roll(p, 96, 1), pltpu.roll(p, 32, 1))
    return p * cos + partner * sin_signed


def _const_spec(shape):
    nd = len(shape)
    return pl.BlockSpec(shape, lambda *_: (0,) * nd, pipeline_mode=pl.Buffered(1))


def _row_chunk_specs(weights, steps_per_seq, n_steps):
    return [pl.BlockSpec((w.shape[0] // n_steps, w.shape[1]), lambda bi, i: (bi * steps_per_seq + i, 0))
            for w in weights]


def _cast_chunks(src_refs, dst_refs):
    for src_ref, dst_ref in zip(src_refs, dst_refs):
        dst_ref[...] = src_ref[...].astype(BF16)


def _in_proj_kernel(x_ref, gain_ref, w_ref, cos_ref, sin_ref,
                    qa_ref, kva_ref, qkv0_ref, qkv1_ref, qkv2_ref, slab_ref, stage_ref):
    t = ROW_TILE
    lane = lax.broadcasted_iota(jnp.int32, (t, LANES), 1)
    first_half = (lane % HEAD_DIM) < (HEAD_DIM // 2)
    low_head = lane < HEAD_DIM
    scale = HEAD_DIM ** -0.5 * LOG2_E
    out_refs = (qkv0_ref, qkv1_ref, qkv2_ref)

    def tile(sub):
        rows = slice(sub * t, (sub + 1) * t)
        h = _rms_norm(x_ref[0, rows, :], gain_ref[...]).astype(BF16)
        cos = cos_ref[rows, :]
        sin_signed = sin_ref[rows, :]

        def proj(col0, ncols):
            return _dot(h, w_ref[:, col0:col0 + ncols])

        def rope(p, mult=None):
            chunks = []
            for c in range(p.shape[1] // LANES):
                r = _rope_chunk(p[:, c * LANES:(c + 1) * LANES], cos, sin_signed, first_half)
                chunks.append(r if mult is None else r * mult)
            return chunks

        def dup_heads(p):
            swapped = pltpu.roll(p, HEAD_DIM, 1)
            return [jnp.where(low_head, p, swapped), jnp.where(low_head, swapped, p)]

        def windowed():
            for c, chunk in enumerate(rope(proj(0, A_Q_COLS), scale)):
                qa_ref[0, rows, c * LANES:(c + 1) * LANES] = chunk.astype(BF16)
            kv = proj(A_Q_COLS, 2 * A_KV_COLS)
            ka = rope(kv[:, :A_KV_COLS])[0]
            for c, chunk in enumerate(dup_heads(ka) + dup_heads(kv[:, A_KV_COLS:])):
                kva_ref[0, rows, c * LANES:(c + 1) * LANES] = chunk.astype(BF16)

        def dilated(g):
            dil = B_PATTERNS[g][1]
            for kind in range(3):
                col0 = A_COLS + kind * B_PROJ_COLS + g * B_GROUP_COLS
                p = proj(col0, B_GROUP_COLS)
                if kind == 0:
                    chunks = rope(p, scale)
                elif kind == 1:
                    chunks = rope(p)
                else:
                    chunks = [p[:, c * LANES:(c + 1) * LANES] for c in range(B_GROUP_COLS // LANES)]
                o_ref = out_refs[g]
                sub_rows = slice(sub * t // dil, (sub + 1) * t // dil)
                col = lambda c: slice(kind * B_GROUP_COLS + c * LANES, kind * B_GROUP_COLS + (c + 1) * LANES)
                if dil == 1:
                    for c, chunk in enumerate(chunks):
                        o_ref[0, 0, sub_rows, col(c)] = chunk.astype(BF16)
                else:
                    for c, chunk in enumerate(chunks):
                        slab_ref[sub, kind, c] = chunk
                    for c in range(len(chunks)):
                        src = slab_ref.at[sub, kind, c]
                        if dil > DEINTERLEAVE_STRIDE:
                            assert dil == DEINTERLEAVE_STRIDE ** 2
                            for a in range(DEINTERLEAVE_STRIDE):
                                stage_ref[sub, kind, c, a] = src[pl.ds(a, t // DEINTERLEAVE_STRIDE,
                                                                       stride=DEINTERLEAVE_STRIDE), :]
                        for res in range(dil):
                            if dil > DEINTERLEAVE_STRIDE:
                                picked = stage_ref[sub, kind, c, res % DEINTERLEAVE_STRIDE,
                                                   pl.ds(res // DEINTERLEAVE_STRIDE, t // dil,
                                                         stride=DEINTERLEAVE_STRIDE), :]
                            else:
                                picked = src[pl.ds(res, t // dil, stride=dil), :]
                            o_ref[0, res, sub_rows, col(c)] = picked.astype(BF16)

        dilated(2)
        dilated(1)
        windowed()
        dilated(0)

    for sub in range(SUB_TILES):
        tile(sub)


def _in_proj(x, gain, w_in, cos, sin_signed):
    b, s, _ = x.shape
    t = SUB_TILES * ROW_TILE
    grid = (b, s // t)
    row_map = lambda bi, i: (bi, i, 0)
    out_shape = [
        jax.ShapeDtypeStruct((b, s, A_Q_COLS), BF16),
        jax.ShapeDtypeStruct((b, s, A_KVA_COLS), BF16),
    ]
    out_specs = [
        pl.BlockSpec((1, t, A_Q_COLS), row_map),
        pl.BlockSpec((1, t, A_KVA_COLS), row_map),
    ]
    for _, dil in B_PATTERNS:
        out_shape.append(jax.ShapeDtypeStruct((b, dil, s // dil, B_QKV_COLS), BF16))
        out_specs.append(pl.BlockSpec((1, dil, t // dil, B_QKV_COLS), lambda bi, i: (bi, 0, i, 0)))
    return pl.pallas_call(
        _in_proj_kernel,
        grid=grid,
        in_specs=[
            pl.BlockSpec((1, t, D_MODEL), row_map),
            _const_spec((1, D_MODEL)),
            _const_spec((D_MODEL, IN_COLS)),
            pl.BlockSpec((t, LANES), lambda bi, i: (i, 0)),
            pl.BlockSpec((t, LANES), lambda bi, i: (i, 0)),
        ],
        out_specs=out_specs,
        out_shape=out_shape,
        scratch_shapes=[
            pltpu.VMEM((SUB_TILES, 3, B_GROUP_COLS // LANES, ROW_TILE, LANES), F32),
            pltpu.VMEM((SUB_TILES, 3, B_GROUP_COLS // LANES, DEINTERLEAVE_STRIDE,
                        ROW_TILE // DEINTERLEAVE_STRIDE, LANES), F32),
        ],
        compiler_params=pltpu.CompilerParams(
            dimension_semantics=("parallel", "parallel"), vmem_limit_bytes=VMEM_LIMIT_BYTES),
        name="in_proj",
    )(x, gain, w_in, cos, sin_signed)


def _attn_b_kernel(bias_ref, q_ref, kv_ref, *rest, dil, n_cast):
    o_ref, lse_ref = rest[n_cast:n_cast + 2]
    _cast_chunks(rest[:n_cast], rest[n_cast + 2:2 * n_cast + 2])
    two_pass = dil > DEINTERLEAVE_STRIDE
    if two_pass:
        assert dil == DEINTERLEAVE_STRIDE ** 2
        stage_ref = rest[2 * n_cast + 2]
    sub_len = kv_ref.shape[2]
    sub_rows = q_ref.shape[2]
    blocks = sub_rows // B_Q_BLOCK
    step = pl.program_id(1)
    low_head = lax.broadcasted_iota(jnp.int32, (B_Q_BLOCK, LANES), 1) < HEAD_DIM
    low_key = lax.broadcasted_iota(jnp.int32, (B_KEY_WINDOW, LANES), 1) < HEAD_DIM

    def units(it, carry):
        work = []
        for uu in range(B_UNROLL):
            u = it * B_UNROLL + uu
            res = u // blocks
            r0 = (u % blocks) * B_Q_BLOCK
            q_pos = (step * sub_rows + r0) // B_HALF_WINDOW
            start_pos = jnp.clip(q_pos - 1, 0, (sub_len - B_KEY_WINDOW) // B_HALF_WINDOW)
            start = start_pos * B_HALF_WINDOW
            bias = bias_ref[q_pos - start_pos]
            q_blk = q_ref[0, res, pl.ds(r0, B_Q_BLOCK), 0:B_GROUP_COLS]
            k_win = kv_ref[0, res, pl.ds(start, B_KEY_WINDOW), B_GROUP_COLS:2 * B_GROUP_COLS]
            v_win = kv_ref[0, res, pl.ds(start, B_KEY_WINDOW), 2 * B_GROUP_COLS:3 * B_GROUP_COLS]
            if dil == 1:
                dst = (None, pl.ds(r0, B_Q_BLOCK))
            elif two_pass:
                dst = (res % DEINTERLEAVE_STRIDE,
                       pl.ds(r0 * DEINTERLEAVE_STRIDE + res // DEINTERLEAVE_STRIDE, B_Q_BLOCK,
                             stride=DEINTERLEAVE_STRIDE))
            else:
                dst = (None, pl.ds(r0 * dil + res, B_Q_BLOCK, stride=dil))
            rows = dst
            for pr in range(B_HEADS // 2):
                qp = q_blk[:, pr * LANES:(pr + 1) * LANES]
                zero = jnp.zeros_like(qp)
                lhs = jnp.concatenate([jnp.where(low_head, qp, zero), jnp.where(low_head, zero, qp)], axis=0)
                sc = _dot_nt(lhs, k_win[:, pr * LANES:(pr + 1) * LANES])
                work.append((rows, pr, bias, v_win[:, pr * LANES:(pr + 1) * LANES], sc))

        soft = []
        for rows, pr, bias, vp, sc in work:
            sc = sc + bias
            m = jnp.max(sc, axis=-1, keepdims=True)
            soft.append((rows, pr, vp, m, jnp.exp2(sc - m).astype(BF16)))

        for rows, pr, vp, m, p in soft:
            one = jnp.ones_like(vp)
            top = _dot(p[:B_Q_BLOCK], jnp.where(low_key, vp, one))
            bot = _dot(p[B_Q_BLOCK:], jnp.where(low_key, one, vp))
            denom = pltpu.roll(jnp.where(low_head, bot, top), HEAD_DIM, 1)
            o_val = jnp.where(low_head, top, bot) / denom
            lse_val = jnp.where(low_head, m[:B_Q_BLOCK], m[B_Q_BLOCK:]) + jnp.log2(denom)
            cls, where = rows
            if two_pass:
                stage_ref[0, pr, cls, where, :] = o_val
                stage_ref[1, pr, cls, where, :] = lse_val
            else:
                o_ref[0, pr, where, :] = o_val
                lse_ref[0, pr, where, :] = lse_val
        return carry

    lax.fori_loop(0, dil * blocks // B_UNROLL, units, 0)
    if two_pass:
        for which, out in enumerate((o_ref, lse_ref)):
            for pr in range(B_HEADS // 2):
                for a in range(DEINTERLEAVE_STRIDE):
                    out[0, pr, pl.ds(a, DEINTERLEAVE_STRIDE * sub_rows, stride=DEINTERLEAVE_STRIDE), :] = (
                        stage_ref[which, pr, a])


def _attn_b(bias, qkv, dil, cast_weights=()):
    b, _, sub_len, _ = qkv.shape
    s = sub_len * dil
    sub_rows = B_SPAN // dil
    nslab = B_GROUP_COLS // LANES
    out_sds = jax.ShapeDtypeStruct((b, nslab, s, LANES), F32)
    out_spec = pl.BlockSpec((1, nslab, B_SPAN, LANES), lambda bi, i: (bi, 0, i, 0))
    steps_per_seq = s // B_SPAN
    chunk_specs = _row_chunk_specs(cast_weights, steps_per_seq, b * steps_per_seq)
    return pl.pallas_call(
        functools.partial(_attn_b_kernel, dil=dil, n_cast=len(cast_weights)),
        grid=(b, steps_per_seq),
        in_specs=[
            _const_spec(bias.shape),
            pl.BlockSpec((1, dil, sub_rows, B_QKV_COLS), lambda bi, i: (bi, 0, i, 0)),
            pl.BlockSpec((1, dil, sub_len, B_QKV_COLS), lambda bi, i: (bi, 0, 0, 0)),
        ] + chunk_specs,
        out_specs=[out_spec, out_spec] + chunk_specs,
        out_shape=[out_sds, out_sds] + [jax.ShapeDtypeStruct(w.shape, BF16) for w in cast_weights],
        scratch_shapes=([pltpu.VMEM((2, nslab, DEINTERLEAVE_STRIDE, DEINTERLEAVE_STRIDE * sub_rows, LANES), F32)]
                        if dil > DEINTERLEAVE_STRIDE else []),
        compiler_params=pltpu.CompilerParams(
            dimension_semantics=("parallel", "parallel"), vmem_limit_bytes=VMEM_LIMIT_BYTES),
        name=f"attn_b_d{dil}",
    )(bias, qkv, qkv, *cast_weights)


def _window(main_ref, prev_ref, next_ref, lead, lo, hi, length, cols):
    parts = []
    if lo < 0:
        halo = prev_ref.shape[-2]
        parts.append(prev_ref[lead + (slice(halo + lo, halo), cols)])
        lo = 0
    parts.append(main_ref[lead + (slice(lo, min(hi, length)), cols)])
    if hi > length:
        parts.append(next_ref[lead + (slice(0, hi - length), cols)])
    return parts[0] if len(parts) == 1 else jnp.concatenate(parts, axis=0)


def _mix_attn_kernel(sink_ref, bias_a_ref, bias_b_ref, x_ref,
                     qa_ref, kva_ref, kva_p_ref, kva_n_ref,
                     b0_ref, b0_p_ref, b0_n_ref, b1_ref, b1_p_ref, b1_n_ref,
                     o2_ref, l2_ref,
                     g_pre_ref, w_gate_ref, b_gate_ref, w_a_ref, w_b_ref, w_out_ref, g_post_ref,
                     *rest, n_cast):
    out_ref = rest[n_cast]
    ya_ref, ob_ref = rest[2 * n_cast + 1:]
    _cast_chunks(rest[:n_cast], rest[n_cast + 1:2 * n_cast + 1])
    t = ROW_TILE
    step_rows = t
    i = pl.program_id(1)
    last = pl.num_programs(1) - 1
    low_head = lax.broadcasted_iota(jnp.int32, (A_Q_BLOCK, LANES), 1) < HEAD_DIM
    low_key_a = lax.broadcasted_iota(jnp.int32, (A_KEY_WINDOW, LANES), 1) < HEAD_DIM
    low_key_b = lax.broadcasted_iota(jnp.int32, (B_KEY_WINDOW, LANES), 1) < HEAD_DIM
    dil1 = B_PATTERNS[1][1]
    sub_rows = step_rows // dil1

    def edge_index(first, final):
        idx = 1
        if first:
            idx = jnp.where(i == 0, 0, idx)
        if final:
            idx = jnp.where(i == last, 2, idx)
        return idx

    def scores(grp_idx):
        work_a, work_b = [], []
        n = grp_idx
        lo, hi = (n - 1) * A_Q_BLOCK, (n + 2) * A_Q_BLOCK
        bias = bias_a_ref[edge_index(n == 0, hi > step_rows)]
        q_blk = qa_ref[0, n * A_Q_BLOCK:(n + 1) * A_Q_BLOCK, :]
        for j in range(A_KV_HEADS):
            kcols = slice(j * LANES, (j + 1) * LANES)
            vcols = slice(2 * A_KV_COLS + j * LANES, 2 * A_KV_COLS + (j + 1) * LANES)
            kd = _window(kva_ref, kva_p_ref, kva_n_ref, (0,), lo, hi, step_rows, kcols)
            vd = _window(kva_ref, kva_p_ref, kva_n_ref, (0,), lo, hi, step_rows, vcols)
            lhs = []
            for g in HEAD_ORDER:
                c0 = j * A_GROUP * HEAD_DIM + (g // 2) * LANES
                pair = q_blk[:, c0:c0 + LANES]
                keep = low_head if g % 2 == 0 else jnp.logical_not(low_head)
                lhs.append(jnp.where(keep, pair, jnp.zeros_like(pair)))
            sc = _dot_nt(jnp.concatenate(lhs, axis=0), kd)
            work_a.append((n, j, bias, vd, sc))
        units = [(0, (0, 0), grp_idx, step_rows), (1, (0, grp_idx), 0, sub_rows)]
        for grp, lead, n, length in units:
            m_ref, p_ref, n_ref = (b0_ref, b0_p_ref, b0_n_ref) if grp == 0 else (b1_ref, b1_p_ref, b1_n_ref)
            lo, hi = n * B_Q_BLOCK - B_HALF_WINDOW, (n + 1) * B_Q_BLOCK + B_HALF_WINDOW
            bias = bias_b_ref[edge_index(lo < 0, hi > length)]
            if grp == 0:
                rows = pl.ds(n * B_Q_BLOCK, B_Q_BLOCK)
            else:
                rows = pl.ds(n * B_Q_BLOCK * dil1 + lead[1], B_Q_BLOCK, stride=dil1)
            for pr in range(B_HEADS // 2):
                cols = [slice(kind * B_GROUP_COLS + pr * LANES, kind * B_GROUP_COLS + (pr + 1) * LANES)
                        for kind in range(3)]
                qp = m_ref[lead + (slice(n * B_Q_BLOCK, (n + 1) * B_Q_BLOCK), cols[0])]
                kp = _window(m_ref, p_ref, n_ref, lead, lo, hi, length, cols[1])
                vp = _window(m_ref, p_ref, n_ref, lead, lo, hi, length, cols[2])
                zero = jnp.zeros_like(qp)
                lhs = jnp.concatenate([jnp.where(low_head, qp, zero), jnp.where(low_head, zero, qp)], axis=0)
                work_b.append((grp, rows, pr, bias, vp, _dot_nt(lhs, kp)))
        return work_a, work_b

    def softmaxes(work):
        work_a, work_b = work
        soft_a, soft_b = [], []
        for n, j, bias, vd, sc in work_a:
            probs, sink_terms = {}, {}
            for slot, g in enumerate(HEAD_ORDER):
                sg = sc[slot * A_Q_BLOCK:(slot + 1) * A_Q_BLOCK] + bias
                sink = sink_ref[j * A_GROUP + g] * LOG2_E
                m = jnp.maximum(jnp.max(sg, axis=-1, keepdims=True), sink)
                probs[g] = jnp.exp2(sg - m).astype(BF16)
                sink_terms[g] = jnp.exp2(sink - m)
            soft_a.append((n, j, vd, probs, sink_terms))
        for grp, rows, pr, bias, vp, sc in work_b:
            sc = sc + bias
            m = jnp.max(sc, axis=-1, keepdims=True)
            soft_b.append((grp, rows, pr, vp, m, jnp.exp2(sc - m).astype(BF16)))
        return soft_a, soft_b

    def values(soft):
        soft_a, soft_b = soft
        pairs = range(A_GROUP // 2)
        for n, j, vd, probs, sink_terms in soft_a:
            one = jnp.ones_like(vd)
            r_even = _dot(jnp.concatenate([probs[2 * pr] for pr in pairs], axis=0),
                          jnp.where(low_key_a, vd, one))
            r_odd = _dot(jnp.concatenate([probs[2 * pr + 1] for pr in pairs], axis=0),
                         jnp.where(low_key_a, one, vd))
            for pr in pairs:
                even = r_even[pr * A_Q_BLOCK:(pr + 1) * A_Q_BLOCK]
                odd = r_odd[pr * A_Q_BLOCK:(pr + 1) * A_Q_BLOCK]
                num = jnp.where(low_head, even, odd)
                den = jnp.where(low_head, odd + sink_terms[2 * pr + 1], even + sink_terms[2 * pr])
                c0 = j * A_GROUP * HEAD_DIM + pr * LANES
                ya_ref[n * A_Q_BLOCK:(n + 1) * A_Q_BLOCK, c0:c0 + LANES] = (
                    num / pltpu.roll(den, HEAD_DIM, 1)).astype(BF16)
        for grp, rows, pr, vp, m, p in soft_b:
            one = jnp.ones_like(vp)
            top = _dot(p[:B_Q_BLOCK], jnp.where(low_key_b, vp, one))
            bot = _dot(p[B_Q_BLOCK:], jnp.where(low_key_b, one, vp))
            denom = pltpu.roll(jnp.where(low_head, bot, top), HEAD_DIM, 1)
            ob_ref[grp, 0, pr, rows, :] = jnp.where(low_head, top, bot) / denom
            ob_ref[grp, 1, pr, rows, :] = jnp.where(low_head, m[:B_Q_BLOCK], m[B_Q_BLOCK:]) + jnp.log2(denom)

    n_groups = t // A_Q_BLOCK
    assert step_rows == t and n_groups == dil1 == t // B_Q_BLOCK
    gate_chunk = 2 * D_MODEL // n_groups
    x = x_ref[0]
    h = _rms_norm(x, g_pre_ref[...]).astype(BF16)

    def gate_logits(j):
        cols = slice(j * gate_chunk, (j + 1) * gate_chunk)
        return _dot(h, w_gate_ref[:, cols]) + b_gate_ref[:, cols]

    z = [None] * n_groups
    work = scores(0)
    z[0] = gate_logits(0)
    soft = softmaxes(work)
    work = scores(1)
    values(soft)
    z[1] = gate_logits(1)
    soft = softmaxes(work)
    work = scores(2)
    values(soft)
    soft = softmaxes(work)
    work = scores(3)
    values(soft)
    z[2] = gate_logits(2)
    z[3] = gate_logits(3)
    values(softmaxes(work))

    yb = []
    for c in range(B_GROUP_COLS // LANES):
        l0, l1, l2 = ob_ref[0, 1, c], ob_ref[1, 1, c], l2_ref[0, c]
        m = jnp.maximum(jnp.maximum(l0, l1), l2)
        e0, e1, e2 = jnp.exp2(l0 - m), jnp.exp2(l1 - m), jnp.exp2(l2 - m)
        num = e0 * ob_ref[0, 0, c] + e1 * ob_ref[1, 0, c] + e2 * o2_ref[0, c]
        yb.append((num / (e0 + e1 + e2)).astype(BF16))
    yb = jnp.concatenate(yb, axis=-1)
    gates = jax.nn.sigmoid(jnp.concatenate(z, axis=-1))
    merged = (gates[:, :D_MODEL] * _dot(ya_ref[...], w_a_ref[...])
              + gates[:, D_MODEL:] * _dot(yb, w_b_ref[...]))
    mix = _dot(merged.astype(BF16), w_out_ref[...])
    out_ref[0] = x + _rms_norm(mix, g_post_ref[...])


def _halo_specs(lead_blocks, rows, halo, total_rows, cols):
    nlead = len(lead_blocks)
    per = rows // halo
    nhalo = total_rows // halo
    zeros = (0,) * nlead
    main = pl.BlockSpec((1,) + lead_blocks + (rows, cols), lambda bi, i: (bi,) + zeros + (i, 0))
    prev = pl.BlockSpec((1,) + lead_blocks + (halo, cols),
                        lambda bi, i: (bi,) + zeros + (jnp.maximum(i * per - 1, 0), 0))
    nxt = pl.BlockSpec((1,) + lead_blocks + (halo, cols),
                       lambda bi, i: (bi,) + zeros + (jnp.minimum((i + 1) * per, nhalo - 1), 0))
    return [main, prev, nxt]


def _mix_attn(x, sink, bias_a, bias_b, qa, kva, qkv_b0, qkv_b1, ob2,
              g_pre, w_gate, b_gate, w_a, w_b, w_out, g_post, cast_weights=()):
    b, s, _ = x.shape
    t = ROW_TILE
    dil1 = B_PATTERNS[1][1]
    nslab = B_GROUP_COLS // LANES
    row_map = lambda bi, i: (bi, i, 0)
    slab_spec = pl.BlockSpec((1, nslab, t, LANES), lambda bi, i: (bi, 0, i, 0))
    chunk_specs = _row_chunk_specs(cast_weights, s // t, b * s // t)
    in_specs = ([pl.BlockSpec(memory_space=pltpu.SMEM), _const_spec(bias_a.shape), _const_spec(bias_b.shape),
                 pl.BlockSpec((1, t, D_MODEL), row_map),
                 pl.BlockSpec((1, t, A_Q_COLS), row_map)]
                + _halo_specs((), t, A_Q_BLOCK, s, A_KVA_COLS)
                + _halo_specs((1,), t, B_HALF_WINDOW, s, B_QKV_COLS)
                + _halo_specs((dil1,), t // dil1, B_HALF_WINDOW, s // dil1, B_QKV_COLS)
                + [slab_spec, slab_spec,
                   _const_spec((1, D_MODEL)),
                   _const_spec((D_MODEL, 2 * D_MODEL)),
                   _const_spec((1, 2 * D_MODEL)),
                   _const_spec((A_Q_COLS, D_MODEL)),
                   _const_spec((B_GROUP_COLS, D_MODEL)),
                   _const_spec((D_MODEL, D_MODEL)),
                   _const_spec((1, D_MODEL))])
    return pl.pallas_call(
        functools.partial(_mix_attn_kernel, n_cast=len(cast_weights)),
        grid=(b, s // t),
        in_specs=in_specs + chunk_specs,
        out_specs=[pl.BlockSpec((1, t, D_MODEL), row_map)] + chunk_specs,
        out_shape=[jax.ShapeDtypeStruct((b, s, D_MODEL), F32)]
        + [jax.ShapeDtypeStruct(w.shape, BF16) for w in cast_weights],
        scratch_shapes=[
            pltpu.VMEM((t, A_Q_COLS), BF16),
            pltpu.VMEM((2, 2, nslab, t, LANES), F32),
        ],
        compiler_params=pltpu.CompilerParams(
            dimension_semantics=("parallel", "parallel"), vmem_limit_bytes=VMEM_LIMIT_BYTES),
        name="mix_attn",
    )(sink, bias_a, bias_b, x, qa, kva, kva, kva, qkv_b0, qkv_b0, qkv_b0, qkv_b1, qkv_b1, qkv_b1,
      ob2[0], ob2[1], g_pre, w_gate, b_gate, w_a, w_b, w_out, g_post, *cast_weights)


def _gelu_tanh(x):
    c = -2.0 * math.sqrt(2.0 / math.pi)
    return x / (1.0 + jnp.exp(x * (c + (c * 0.044715) * (x * x))))


def _conv_ffn_kernel(x_ref, prev_ref, next_ref, g_pre_ref, w_up_ref, conv_w_ref, conv_b_ref,
                     w_down_ref, g_post_ref, out_ref, u_ref, acc_ref, perm_ref):
    t = ROW_TILE
    half = t // 2
    i = pl.program_id(1)
    last = pl.num_programs(1) - 1
    row = lax.broadcasted_iota(jnp.int32, (t + 2 * HALO, 1), 0)
    starts = [sum(FF_CHUNKS[:c]) for c in range(len(FF_CHUNKS))]

    def col_starts(c):
        return (starts[c], D_FF + starts[c])

    def project(h, c, slot):
        for part, col0 in enumerate(col_starts(c)):
            up = _dot(h, w_up_ref[:, col0:col0 + FF_CHUNKS[c]])
            for j in range(FF_CHUNKS[c] // LANES):
                u_ref[2 * slot + part, j] = up[:, j * LANES:(j + 1) * LANES]

    def conv(c, slot, part):
        buf, col0 = 2 * slot + part, col_starts(c)[part]
        pieces = []
        for j in range(FF_CHUNKS[c] // LANES):
            cols = slice(col0 + j * LANES, col0 + (j + 1) * LANES)
            w0, w1, w2 = conv_w_ref[0:1, cols], conv_w_ref[1:2, cols], conv_w_ref[2:3, cols]
            bias = conv_b_ref[:, cols]
            r = [u_ref[buf, j, pl.ds(HALO - 1 + k, half, stride=2), :] for k in range(4)]
            even = r[0] * w0 + r[1] * w1 + r[2] * w2 + bias
            odd = r[1] * w0 + r[2] * w1 + r[3] * w2 + bias
            pieces.append(jnp.concatenate([even, odd], axis=0))
        return jnp.concatenate(pieces, axis=-1)

    def tile(sub, carry):
        lo = pl.multiple_of(sub * t, t)
        x = x_ref[0, pl.ds(lo, t), :]
        first, final = sub == 0, sub == FFN_SUB_TILES - 1
        inner_before = x_ref[0, pl.ds(pl.multiple_of(jnp.maximum(lo - HALO, 0), HALO), HALO), :]
        inner_after = x_ref[0, pl.ds(pl.multiple_of(jnp.minimum(lo + t, (FFN_SUB_TILES - 1) * t), HALO), HALO), :]
        before = jnp.where(first, prev_ref[0], inner_before)
        after = jnp.where(final, next_ref[0], inner_after)
        hn = _rms_norm(jnp.concatenate([before, x, after], axis=0), g_pre_ref[...])
        hn = jnp.where(jnp.logical_or(row >= HALO, jnp.logical_or(i > 0, sub > 0)), hn, 0.0)
        hn = jnp.where(jnp.logical_or(row < HALO + t, jnp.logical_or(i < last, sub < FFN_SUB_TILES - 1)), hn, 0.0)
        h = hn.astype(BF16)

        n_chunks = len(FF_CHUNKS)
        for c in range(min(FF_LOOKAHEAD, n_chunks)):
            project(h, c, c % (FF_LOOKAHEAD + 1))
        for c in range(n_chunks):
            if c + FF_LOOKAHEAD < n_chunks:
                project(h, c + FF_LOOKAHEAD, (c + FF_LOOKAHEAD) % (FF_LOOKAHEAD + 1))
            slot = c % (FF_LOOKAHEAD + 1)
            act = (_gelu_tanh(conv(c, slot, 0)) * conv(c, slot, 1)).astype(BF16)
            part = _dot(act, w_down_ref[starts[c]:starts[c] + FF_CHUNKS[c], :])
            if c == 0:
                acc_ref[...] = part
            else:
                acc_ref[...] += part
        y = _rms_norm(acc_ref[...], g_post_ref[...])
        for j in range(D_MODEL // LANES):
            perm_ref[j, pl.ds(0, half, stride=2), :] = y[:half, j * LANES:(j + 1) * LANES]
            perm_ref[j, pl.ds(1, half, stride=2), :] = y[half:, j * LANES:(j + 1) * LANES]
        y_nat = jnp.concatenate([perm_ref[j] for j in range(D_MODEL // LANES)], axis=-1)
        out_ref[0, pl.ds(lo, t), :] = x + y_nat
        return carry

    lax.fori_loop(0, FFN_SUB_TILES, tile, 0)


def _conv_ffn(x, g_pre, w_up, conv_w, conv_b, w_down, g_post):
    b, s, _ = x.shape
    t = FFN_SUB_TILES * ROW_TILE
    tiles = s // t
    halo_blocks_per_tile = t // HALO
    n_halo_blocks = s // HALO
    row_map = lambda bi, i: (bi, i, 0)
    prev_map = lambda bi, i: (bi, jnp.maximum(i * halo_blocks_per_tile - 1, 0), 0)
    next_map = lambda bi, i: (bi, jnp.minimum((i + 1) * halo_blocks_per_tile, n_halo_blocks - 1), 0)
    return pl.pallas_call(
        _conv_ffn_kernel,
        grid=(b, tiles),
        in_specs=[
            pl.BlockSpec((1, t, D_MODEL), row_map),
            pl.BlockSpec((1, HALO, D_MODEL), prev_map),
            pl.BlockSpec((1, HALO, D_MODEL), next_map),
            _const_spec((1, D_MODEL)),
            _const_spec((D_MODEL, 2 * D_FF)),
            _const_spec((3, 2 * D_FF)),
            _const_spec((1, 2 * D_FF)),
            _const_spec((D_FF, D_MODEL)),
            _const_spec((1, D_MODEL)),
        ],
        out_specs=pl.BlockSpec((1, t, D_MODEL), row_map),
        out_shape=jax.ShapeDtypeStruct((b, s, D_MODEL), F32),
        scratch_shapes=[
            pltpu.VMEM((2 * (FF_LOOKAHEAD + 1), FF_CHUNK // LANES, ROW_TILE + 2 * HALO, LANES), F32),
            pltpu.VMEM((ROW_TILE, D_MODEL), F32),
            pltpu.VMEM((D_MODEL // LANES, ROW_TILE, LANES), F32),
        ],
        compiler_params=pltpu.CompilerParams(
            dimension_semantics=("parallel", "parallel"), vmem_limit_bytes=VMEM_LIMIT_BYTES),
        name="conv_ffn",
    )(x, x, x, g_pre, w_up, conv_w, conv_b, w_down, g_post)


def _rope_tables(seq_len):
    half = HEAD_DIM // 2
    inv = ROPE_THETA ** (-np.arange(half, dtype=np.float64) / half)
    ang = np.arange(seq_len, dtype=np.float64)[:, None] * inv[None, :]
    cos, sin = np.cos(ang), np.sin(ang)
    cos_head = np.concatenate([cos, cos], axis=-1)
    sin_head = np.concatenate([-sin, sin], axis=-1)
    reps = LANES // HEAD_DIM
    return (jnp.asarray(np.tile(cos_head, (1, reps)), dtype=F32),
            jnp.asarray(np.tile(sin_head, (1, reps)), dtype=F32))


def _band_bias(q_rows, stacked, keys, half_window, lead):
    i = (np.arange(stacked * q_rows) % q_rows)[None, :, None]
    j = np.arange(keys)[None, None, :]
    e = np.arange(3)[:, None, None]
    band = np.abs(j - lead - i) <= half_window
    inside = np.logical_and(np.logical_or(e != 0, j >= lead), np.logical_or(e != 2, j < lead + q_rows))
    return jnp.asarray(np.where(np.logical_and(band, inside), 0.0, NEG_INF), dtype=F32)


def _far_band_bias():
    off = np.arange(3)[:, None, None] * B_HALF_WINDOW
    i = (np.arange(2 * B_Q_BLOCK) % B_Q_BLOCK)[None, :, None]
    j = np.arange(B_KEY_WINDOW)[None, None, :]
    return jnp.asarray(np.where(np.abs(j - off - i) <= B_HALF_WINDOW, 0.0, NEG_INF), dtype=F32)


def kernel(x, norm_mix_pre, w_in, sink, w_branch_a, w_branch_b, w_gate, b_gate, w_out,
           norm_mix_post, norm_ffn_pre, w_up, conv_w, conv_b, w_down, norm_ffn_post):
    b, s, d = x.shape
    assert d == D_MODEL and s % B_SPAN == 0 and s % (SUB_TILES * ROW_TILE) == 0
    cos, sin_signed = _rope_tables(s)
    bias_a = _band_bias(A_Q_BLOCK, 1, A_KEY_WINDOW, A_HALF_WINDOW, A_Q_BLOCK)
    bias_b = _band_bias(B_Q_BLOCK, 2, B_KEY_WINDOW, B_HALF_WINDOW, B_HALF_WINDOW)
    bias_b_far = _far_band_bias()
    for layer in range(norm_mix_pre.shape[0]):
        proj = _in_proj(x, norm_mix_pre[layer][None], w_in[layer].astype(BF16), cos, sin_signed)
        qa, kva, qkv_b0, qkv_b1, qkv_b2 = proj
        o2, l2, w_gate_h, w_a_h, w_b_h, w_out_h = _attn_b(
            bias_b_far, qkv_b2, B_PATTERNS[2][1],
            (w_gate[layer], w_branch_a[layer], w_branch_b[layer], w_out[layer]))
        ob2 = (o2, l2)
        x, w_up_h, w_down_h = _mix_attn(
                      x, sink[layer], bias_a, bias_b, qa, kva, qkv_b0, qkv_b1, ob2,
                      norm_mix_pre[layer][None],
                      w_gate_h, b_gate[layer][None], w_a_h, w_b_h, w_out_h, norm_mix_post[layer][None],
                      (w_up[layer], w_down[layer]))
        x = _conv_ffn(x, norm_ffn_pre[layer][None],
                      w_up_h, conv_w[layer], conv_b[layer][None], w_down_h,
                      norm_ffn_post[layer][None])
    return x
```

```python
import functools
import math

import jax
import jax.numpy as jnp
import numpy as np
from jax import lax
from jax.experimental import pallas as pl
from jax.experimental.pallas import tpu as pltpu

D_MODEL = 1024
HEAD_DIM = 64
A_Q_HEADS = 8
A_KV_HEADS = 2
A_GROUP = A_Q_HEADS // A_KV_HEADS
A_HALF_WINDOW = 128
B_PATTERNS = ((128, 1), (512, 4), (2048, 16))
B_N_GROUPS = len(B_PATTERNS)
B_HEADS = 4
HEAD_ORDER = (0, 2, 1, 3)
B_HALF_WINDOW = 64
ROPE_THETA = 10000.0
D_FF = 3 * D_MODEL
RMS_EPS = 1e-6
NEG_INF = -1e30
LOG2_E = math.log2(math.e)

A_Q_COLS = A_Q_HEADS * HEAD_DIM
A_KV_COLS = A_KV_HEADS * HEAD_DIM
A_COLS = A_Q_COLS + 2 * A_KV_COLS
B_GROUP_COLS = B_HEADS * HEAD_DIM
B_PROJ_COLS = B_N_GROUPS * B_GROUP_COLS
IN_COLS = A_COLS + 3 * B_PROJ_COLS
A_KVA_COLS = 4 * A_KV_COLS
B_QKV_COLS = 3 * B_GROUP_COLS

LANES = 128
VMEM_LIMIT_BYTES = 56 * 1024 * 1024

ROW_TILE = 512
SUB_TILES = 2
FFN_SUB_TILES = 2
A_Q_BLOCK = 128
A_KEY_WINDOW = 3 * A_Q_BLOCK
B_Q_BLOCK = 128
B_KEY_WINDOW = B_Q_BLOCK + 2 * B_HALF_WINDOW
B_SPAN = 2048
B_UNROLL = 8
FF_CHUNK = 512
FF_CHUNKS = (FF_CHUNK,) * (D_FF // FF_CHUNK)
FF_LOOKAHEAD = 2
HALO = 8
DEINTERLEAVE_STRIDE = 4

BF16 = jnp.bfloat16
F32 = jnp.float32


def _dot(a, b):
    return jnp.dot(a, b, preferred_element_type=F32)


def _dot_nt(a, b):
    return lax.dot_general(a, b, (((1,), (1,)), ((), ())), preferred_element_type=F32)


def _rms_norm(x, gain):
    ms = jnp.mean(x * x, axis=-1, keepdims=True)
    return x * lax.rsqrt(ms + RMS_EPS) * gain


def _rope_chunk(p, cos, sin_signed, first_half):
    partner = jnp.where(first_half, pltpu.roll(p, 96, 1), pltpu.roll(p, 32, 1))
    return p * cos + partner * sin_signed


def _const_spec(shape):
    nd = len(shape)
    return pl.BlockSpec(shape, lambda *_: (0,) * nd, pipeline_mode=pl.Buffered(1))


def _row_chunk_specs(weights, steps_per_seq, n_steps):
    return [pl.BlockSpec((w.shape[0] // n_steps, w.shape[1]), lambda bi, i: (bi * steps_per_seq + i, 0))
            for w in weights]


def _cast_chunks(src_refs, dst_refs):
    for src_ref, dst_ref in zip(src_refs, dst_refs):
        dst_ref[...] = src_ref[...].astype(BF16)


def _in_proj_kernel(x_ref, gain_ref, w_ref, cos_ref, sin_ref,
                    qa_ref, kva_ref, qkv0_ref, qkv1_ref, qkv2_ref, slab_ref, stage_ref):
    t = ROW_TILE
    lane = lax.broadcasted_iota(jnp.int32, (t, LANES), 1)
    first_half = (lane % HEAD_DIM) < (HEAD_DIM // 2)
    low_head = lane < HEAD_DIM
    scale = HEAD_DIM ** -0.5 * LOG2_E
    out_refs = (qkv0_ref, qkv1_ref, qkv2_ref)

    def tile(sub):
        rows = slice(sub * t, (sub + 1) * t)
        h = _rms_norm(x_ref[0, rows, :], gain_ref[...]).astype(BF16)
        cos = cos_ref[rows, :]
        sin_signed = sin_ref[rows, :]

        def proj(col0, ncols):
            return _dot(h, w_ref[:, col0:col0 + ncols])

        def rope(p, mult=None):
            chunks = []
            for c in range(p.shape[1] // LANES):
                r = _rope_chunk(p[:, c * LANES:(c + 1) * LANES], cos, sin_signed, first_half)
                chunks.append(r if mult is None else r * mult)
            return chunks

        def dup_heads(p):
            swapped = pltpu.roll(p, HEAD_DIM, 1)
            return [jnp.where(low_head, p, swapped), jnp.where(low_head, swapped, p)]

        def windowed():
            for c, chunk in enumerate(rope(proj(0, A_Q_COLS), scale)):
                qa_ref[0, rows, c * LANES:(c + 1) * LANES] = chunk.astype(BF16)
            kv = proj(A_Q_COLS, 2 * A_KV_COLS)
            ka = rope(kv[:, :A_KV_COLS])[0]
            for c, chunk in enumerate(dup_heads(ka) + dup_heads(kv[:, A_KV_COLS:])):
                kva_ref[0, rows, c * LANES:(c + 1) * LANES] = chunk.astype(BF16)

        def dilated(g):
            dil = B_PATTERNS[g][1]
            for kind in range(3):
                col0 = A_COLS + kind * B_PROJ_COLS + g * B_GROUP_COLS
                p = proj(col0, B_GROUP_COLS)
                if kind == 0:
                    chunks = rope(p, scale)
                elif kind == 1:
                    chunks = rope(p)
                else:
                    chunks = [p[:, c * LANES:(c + 1) * LANES] for c in range(B_GROUP_COLS // LANES)]
                o_ref = out_refs[g]
                sub_rows = slice(sub * t // dil, (sub + 1) * t // dil)
                col = lambda c: slice(kind * B_GROUP_COLS + c * LANES, kind * B_GROUP_COLS + (c + 1) * LANES)
                if dil == 1:
                    for c, chunk in enumerate(chunks):
                        o_ref[0, 0, sub_rows, col(c)] = chunk.astype(BF16)
                else:
                    for c, chunk in enumerate(chunks):
                        slab_ref[sub, kind, c] = chunk
                    for c in range(len(chunks)):
                        src = slab_ref.at[sub, kind, c]
                        if dil > DEINTERLEAVE_STRIDE:
                            assert dil == DEINTERLEAVE_STRIDE ** 2
                            for a in range(DEINTERLEAVE_STRIDE):
                                stage_ref[sub, kind, c, a] = src[pl.ds(a, t // DEINTERLEAVE_STRIDE,
                                                                       stride=DEINTERLEAVE_STRIDE), :]
                        for res in range(dil):
                            if dil > DEINTERLEAVE_STRIDE:
                                picked = stage_ref[sub, kind, c, res % DEINTERLEAVE_STRIDE,
                                                   pl.ds(res // DEINTERLEAVE_STRIDE, t // dil,
                                                         stride=DEINTERLEAVE_STRIDE), :]
                            else:
                                picked = src[pl.ds(res, t // dil, stride=dil), :]
                            o_ref[0, res, sub_rows, col(c)] = picked.astype(BF16)

        dilated(2)
        dilated(1)
        windowed()
        dilated(0)

    for sub in range(SUB_TILES):
        tile(sub)


def _in_proj(x, gain, w_in, cos, sin_signed):
    b, s, _ = x.shape
    t = SUB_TILES * ROW_TILE
    grid = (b, s // t)
    row_map = lambda bi, i: (bi, i, 0)
    out_shape = [
        jax.ShapeDtypeStruct((b, s, A_Q_COLS), BF16),
        jax.ShapeDtypeStruct((b, s, A_KVA_COLS), BF16),
    ]
    out_specs = [
        pl.BlockSpec((1, t, A_Q_COLS), row_map),
        pl.BlockSpec((1, t, A_KVA_COLS), row_map),
    ]
    for _, dil in B_PATTERNS:
        out_shape.append(jax.ShapeDtypeStruct((b, dil, s // dil, B_QKV_COLS), BF16))
        out_specs.append(pl.BlockSpec((1, dil, t // dil, B_QKV_COLS), lambda bi, i: (bi, 0, i, 0)))
    return pl.pallas_call(
        _in_proj_kernel,
        grid=grid,
        in_specs=[
            pl.BlockSpec((1, t, D_MODEL), row_map),
            _const_spec((1, D_MODEL)),
            _const_spec((D_MODEL, IN_COLS)),
            pl.BlockSpec((t, LANES), lambda bi, i: (i, 0)),
            pl.BlockSpec((t, LANES), lambda bi, i: (i, 0)),
        ],
        out_specs=out_specs,
        out_shape=out_shape,
        scratch_shapes=[
            pltpu.VMEM((SUB_TILES, 3, B_GROUP_COLS // LANES, ROW_TILE, LANES), F32),
            pltpu.VMEM((SUB_TILES, 3, B_GROUP_COLS // LANES, DEINTERLEAVE_STRIDE,
                        ROW_TILE // DEINTERLEAVE_STRIDE, LANES), F32),
        ],
        compiler_params=pltpu.CompilerParams(
            dimension_semantics=("parallel", "parallel"), vmem_limit_bytes=VMEM_LIMIT_BYTES),
        name="in_proj",
    )(x, gain, w_in, cos, sin_signed)


def _attn_b_kernel(bias_ref, q_ref, kv_ref, *rest, dil, n_cast):
    o_ref, lse_ref = rest[n_cast:n_cast + 2]
    _cast_chunks(rest[:n_cast], rest[n_cast + 2:2 * n_cast + 2])
    two_pass = dil > DEINTERLEAVE_STRIDE
    if two_pass:
        assert dil == DEINTERLEAVE_STRIDE ** 2
        stage_ref = rest[2 * n_cast + 2]
    sub_len = kv_ref.shape[2]
    sub_rows = q_ref.shape[2]
    blocks = sub_rows // B_Q_BLOCK
    step = pl.program_id(1)
    low_head = lax.broadcasted_iota(jnp.int32, (B_Q_BLOCK, LANES), 1) < HEAD_DIM
    low_key = lax.broadcasted_iota(jnp.int32, (B_KEY_WINDOW, LANES), 1) < HEAD_DIM

    def units(it, carry):
        work = []
        for uu in range(B_UNROLL):
            u = it * B_UNROLL + uu
            res = u // blocks
            r0 = (u % blocks) * B_Q_BLOCK
            q_pos = (step * sub_rows + r0) // B_HALF_WINDOW
            start_pos = jnp.clip(q_pos - 1, 0, (sub_len - B_KEY_WINDOW) // B_HALF_WINDOW)
            start = start_pos * B_HALF_WINDOW
            bias = bias_ref[q_pos - start_pos]
            q_blk = q_ref[0, res, pl.ds(r0, B_Q_BLOCK), 0:B_GROUP_COLS]
            k_win = kv_ref[0, res, pl.ds(start, B_KEY_WINDOW), B_GROUP_COLS:2 * B_GROUP_COLS]
            v_win = kv_ref[0, res, pl.ds(start, B_KEY_WINDOW), 2 * B_GROUP_COLS:3 * B_GROUP_COLS]
            if dil == 1:
                dst = (None, pl.ds(r0, B_Q_BLOCK))
            elif two_pass:
                dst = (res % DEINTERLEAVE_STRIDE,
                       pl.ds(r0 * DEINTERLEAVE_STRIDE + res // DEINTERLEAVE_STRIDE, B_Q_BLOCK,
                             stride=DEINTERLEAVE_STRIDE))
            else:
                dst = (None, pl.ds(r0 * dil + res, B_Q_BLOCK, stride=dil))
            rows = dst
            for pr in range(B_HEADS // 2):
                qp = q_blk[:, pr * LANES:(pr + 1) * LANES]
                zero = jnp.zeros_like(qp)
                lhs = jnp.concatenate([jnp.where(low_head, qp, zero), jnp.where(low_head, zero, qp)], axis=0)
                sc = _dot_nt(lhs, k_win[:, pr * LANES:(pr + 1) * LANES])
                work.append((rows, pr, bias, v_win[:, pr * LANES:(pr + 1) * LANES], sc))

        soft = []
        for rows, pr, bias, vp, sc in work:
            sc = sc + bias
            m = jnp.max(sc, axis=-1, keepdims=True)
            soft.append((rows, pr, vp, m, jnp.exp2(sc - m).astype(BF16)))

        for rows, pr, vp, m, p in soft:
            one = jnp.ones_like(vp)
            top = _dot(p[:B_Q_BLOCK], jnp.where(low_key, vp, one))
            bot = _dot(p[B_Q_BLOCK:], jnp.where(low_key, one, vp))
            denom = pltpu.roll(jnp.where(low_head, bot, top), HEAD_DIM, 1)
            o_val = jnp.where(low_head, top, bot) / denom
            lse_val = jnp.where(low_head, m[:B_Q_BLOCK], m[B_Q_BLOCK:]) + jnp.log2(denom)
            cls, where = rows
            if two_pass:
                stage_ref[0, pr, cls, where, :] = o_val
                stage_ref[1, pr, cls, where, :] = lse_val
            else:
                o_ref[0, pr, where, :] = o_val
                lse_ref[0, pr, where, :] = lse_val
        return carry

    lax.fori_loop(0, dil * blocks // B_UNROLL, units, 0)
    if two_pass:
        for which, out in enumerate((o_ref, lse_ref)):
            for pr in range(B_HEADS // 2):
                for a in range(DEINTERLEAVE_STRIDE):
                    out[0, pr, pl.ds(a, DEINTERLEAVE_STRIDE * sub_rows, stride=DEINTERLEAVE_STRIDE), :] = (
                        stage_ref[which, pr, a])


def _attn_b(bias, qkv, dil, cast_weights=()):
    b, _, sub_len, _ = qkv.shape
    s = sub_len * dil
    sub_rows = B_SPAN // dil
    nslab = B_GROUP_COLS // LANES
    out_sds = jax.ShapeDtypeStruct((b, nslab, s, LANES), F32)
    out_spec = pl.BlockSpec((1, nslab, B_SPAN, LANES), lambda bi, i: (bi, 0, i, 0))
    steps_per_seq = s // B_SPAN
    chunk_specs = _row_chunk_specs(cast_weights, steps_per_seq, b * steps_per_seq)
    return pl.pallas_call(
        functools.partial(_attn_b_kernel, dil=dil, n_cast=len(cast_weights)),
        grid=(b, steps_per_seq),
        in_specs=[
            _const_spec(bias.shape),
            pl.BlockSpec((1, dil, sub_rows, B_QKV_COLS), lambda bi, i: (bi, 0, i, 0)),
            pl.BlockSpec((1, dil, sub_len, B_QKV_COLS), lambda bi, i: (bi, 0, 0, 0)),
        ] + chunk_specs,
        out_specs=[out_spec, out_spec] + chunk_specs,
        out_shape=[out_sds, out_sds] + [jax.ShapeDtypeStruct(w.shape, BF16) for w in cast_weights],
        scratch_shapes=([pltpu.VMEM((2, nslab, DEINTERLEAVE_STRIDE, DEINTERLEAVE_STRIDE * sub_rows, LANES), F32)]
                        if dil > DEINTERLEAVE_STRIDE else []),
        compiler_params=pltpu.CompilerParams(
            dimension_semantics=("parallel", "parallel"), vmem_limit_bytes=VMEM_LIMIT_BYTES),
        name=f"attn_b_d{dil}",
    )(bias, qkv, qkv, *cast_weights)


def _window(main_ref, prev_ref, next_ref, lead, lo, hi, length, cols):
    parts = []
    if lo < 0:
        halo = prev_ref.shape[-2]
        parts.append(prev_ref[lead + (slice(halo + lo, halo), cols)])
        lo = 0
    parts.append(main_ref[lead + (slice(lo, min(hi, length)), cols)])
    if hi > length:
        parts.append(next_ref[lead + (slice(0, hi - length), cols)])
    return parts[0] if len(parts) == 1 else jnp.concatenate(parts, axis=0)


def _mix_attn_kernel(sink_ref, bias_a_ref, bias_b_ref, x_ref,
                     qa_ref, kva_ref, kva_p_ref, kva_n_ref,
                     b0_ref, b0_p_ref, b0_n_ref, b1_ref, b1_p_ref, b1_n_ref,
                     o2_ref, l2_ref,
                     g_pre_ref, w_gate_ref, b_gate_ref, w_a_ref, w_b_ref, w_out_ref, g_post_ref,
                     *rest, n_cast):
    out_ref = rest[n_cast]
    ya_ref, ob_ref = rest[2 * n_cast + 1:]
    _cast_chunks(rest[:n_cast], rest[n_cast + 1:2 * n_cast + 1])
    t = ROW_TILE
    step_rows = t
    i = pl.program_id(1)
    last = pl.num_programs(1) - 1
    low_head = lax.broadcasted_iota(jnp.int32, (A_Q_BLOCK, LANES), 1) < HEAD_DIM
    dil1 = B_PATTERNS[1][1]
    sub_rows = step_rows // dil1

    def edge_index(first, final):
        idx = 1
        if first:
            idx = jnp.where(i == 0, 0, idx)
        if final:
            idx = jnp.where(i == last, 2, idx)
        return idx

    def scores(grp_idx):
        work_a, work_b = [], []
        n = grp_idx
        lo, hi = (n - 1) * A_Q_BLOCK, (n + 2) * A_Q_BLOCK
        bias = bias_a_ref[edge_index(n == 0, hi > step_rows)]
        q_blk = qa_ref[0, n * A_Q_BLOCK:(n + 1) * A_Q_BLOCK, :]
        for j in range(A_KV_HEADS):
            kcols = slice(j * LANES, (j + 1) * LANES)
            vcols = slice(2 * A_KV_COLS + j * LANES, 2 * A_KV_COLS + (j + 1) * LANES)
            kd = _window(kva_ref, kva_p_ref, kva_n_ref, (0,), lo, hi, step_rows, kcols)
            vd = _window(kva_ref, kva_p_ref, kva_n_ref, (0,), lo, hi, step_rows, vcols)
            lhs = []
            for g in HEAD_ORDER:
                c0 = j * A_GROUP * HEAD_DIM + (g // 2) * LANES
                pair = q_blk[:, c0:c0 + LANES]
                keep = low_head if g % 2 == 0 else jnp.logical_not(low_head)
                lhs.append(jnp.where(keep, pair, jnp.zeros_like(pair)))
            sc = _dot_nt(jnp.concatenate(lhs, axis=0), kd)
            work_a.append((n, j, bias, vd, sc))
        units = [(0, (0, 0), grp_idx, step_rows), (1, (0, grp_idx), 0, sub_rows)]
        for grp, lead, n, length in units:
            m_ref, p_ref, n_ref = (b0_ref, b0_p_ref, b0_n_ref) if grp == 0 else (b1_ref, b1_p_ref, b1_n_ref)
            lo, hi = n * B_Q_BLOCK - B_HALF_WINDOW, (n + 1) * B_Q_BLOCK + B_HALF_WINDOW
            bias = bias_b_ref[edge_index(lo < 0, hi > length)]
            if grp == 0:
                rows = pl.ds(n * B_Q_BLOCK, B_Q_BLOCK)
            else:
                rows = pl.ds(n * B_Q_BLOCK * dil1 + lead[1], B_Q_BLOCK, stride=dil1)
            for pr in range(B_HEADS // 2):
                cols = [slice(kind * B_GROUP_COLS + pr * LANES, kind * B_GROUP_COLS + (pr + 1) * LANES)
                        for kind in range(3)]
                qp = m_ref[lead + (slice(n * B_Q_BLOCK, (n + 1) * B_Q_BLOCK), cols[0])]
                kp = _window(m_ref, p_ref, n_ref, lead, lo, hi, length, cols[1])
                vp = _window(m_ref, p_ref, n_ref, lead, lo, hi, length, cols[2])
                zero = jnp.zeros_like(qp)
                lhs = jnp.concatenate([jnp.where(low_head, qp, zero), jnp.where(low_head, zero, qp)], axis=0)
                work_b.append((grp, rows, pr, bias, vp, _dot_nt(lhs, kp)))
        return work_a, work_b

    def softmaxes(work):
        work_a, work_b = work
        soft_a, soft_b = [], []
        for n, j, bias, vd, sc in work_a:
            probs, denoms = {}, {}
            for slot, g in enumerate(HEAD_ORDER):
                sg = sc[slot * A_Q_BLOCK:(slot + 1) * A_Q_BLOCK] + bias
                sink = sink_ref[j * A_GROUP + g] * LOG2_E
                m = jnp.maximum(jnp.max(sg, axis=-1, keepdims=True), sink)
                p = jnp.exp2(sg - m)
                probs[g] = p.astype(BF16)
                denoms[g] = jnp.sum(p, axis=-1, keepdims=True) + jnp.exp2(sink - m)
            soft_a.append((n, j, vd, probs, denoms))
        for grp, rows, pr, bias, vp, sc in work_b:
            sc = sc + bias
            m = jnp.max(sc, axis=-1, keepdims=True)
            p = jnp.exp2(sc - m)
            soft_b.append((grp, rows, pr, vp, (m, jnp.sum(p, axis=-1, keepdims=True)), p.astype(BF16)))
        return soft_a, soft_b

    def values(soft):
        soft_a, soft_b = soft
        pairs = range(A_GROUP // 2)
        for n, j, vd, probs, denoms in soft_a:
            r = _dot(jnp.concatenate([probs[g] for g in range(A_GROUP)], axis=0), vd)
            for pr in pairs:
                even = r[(2 * pr) * A_Q_BLOCK:(2 * pr + 1) * A_Q_BLOCK] / denoms[2 * pr]
                odd = r[(2 * pr + 1) * A_Q_BLOCK:(2 * pr + 2) * A_Q_BLOCK] / denoms[2 * pr + 1]
                c0 = j * A_GROUP * HEAD_DIM + pr * LANES
                ya_ref[n * A_Q_BLOCK:(n + 1) * A_Q_BLOCK, c0:c0 + LANES] = (
                    jnp.where(low_head, even, odd)).astype(BF16)
        for grp, rows, pr, vp, (m, l), p in soft_b:
            o = _dot(p, vp) / l
            lse = m + jnp.log2(l)
            ob_ref[grp, 0, pr, rows, :] = jnp.where(low_head, o[:B_Q_BLOCK], o[B_Q_BLOCK:])
            ob_ref[grp, 1, pr, rows, :] = jnp.where(low_head, lse[:B_Q_BLOCK], lse[B_Q_BLOCK:])

    n_groups = t // A_Q_BLOCK
    assert step_rows == t and n_groups == dil1 == t // B_Q_BLOCK
    gate_chunk = 2 * D_MODEL // n_groups
    x = x_ref[0]
    h = _rms_norm(x, g_pre_ref[...]).astype(BF16)

    def gate_logits(j):
        cols = slice(j * gate_chunk, (j + 1) * gate_chunk)
        return _dot(h, w_gate_ref[:, cols]) + b_gate_ref[:, cols]

    z = [None] * n_groups
    work = scores(0)
    z[0] = gate_logits(0)
    soft = softmaxes(work)
    work = scores(1)
    values(soft)
    z[1] = gate_logits(1)
    soft = softmaxes(work)
    work = scores(2)
    values(soft)
    soft = softmaxes(work)
    work = scores(3)
    values(soft)
    z[2] = gate_logits(2)
    z[3] = gate_logits(3)
    values(softmaxes(work))

    yb = []
    for c in range(B_GROUP_COLS // LANES):
        l0, l1, l2 = ob_ref[0, 1, c], ob_ref[1, 1, c], l2_ref[0, c]
        m = jnp.maximum(jnp.maximum(l0, l1), l2)
        e0, e1, e2 = jnp.exp2(l0 - m), jnp.exp2(l1 - m), jnp.exp2(l2 - m)
        num = e0 * ob_ref[0, 0, c] + e1 * ob_ref[1, 0, c] + e2 * o2_ref[0, c]
        yb.append((num / (e0 + e1 + e2)).astype(BF16))
    yb = jnp.concatenate(yb, axis=-1)
    gates = jax.nn.sigmoid(jnp.concatenate(z, axis=-1))
    merged = (gates[:, :D_MODEL] * _dot(ya_ref[...], w_a_ref[...])
              + gates[:, D_MODEL:] * _dot(yb, w_b_ref[...]))
    mix = _dot(merged.astype(BF16), w_out_ref[...])
    out_ref[0] = x + _rms_norm(mix, g_post_ref[...])


def _halo_specs(lead_blocks, rows, halo, total_rows, cols):
    nlead = len(lead_blocks)
    per = rows // halo
    nhalo = total_rows // halo
    zeros = (0,) * nlead
    main = pl.BlockSpec((1,) + lead_blocks + (rows, cols), lambda bi, i: (bi,) + zeros + (i, 0))
    prev = pl.BlockSpec((1,) + lead_blocks + (halo, cols),
                        lambda bi, i: (bi,) + zeros + (jnp.maximum(i * per - 1, 0), 0))
    nxt = pl.BlockSpec((1,) + lead_blocks + (halo, cols),
                       lambda bi, i: (bi,) + zeros + (jnp.minimum((i + 1) * per, nhalo - 1), 0))
    return [main, prev, nxt]


def _mix_attn(x, sink, bias_a, bias_b, qa, kva, qkv_b0, qkv_b1, ob2,
              g_pre, w_gate, b_gate, w_a, w_b, w_out, g_post, cast_weights=()):
    b, s, _ = x.shape
    t = ROW_TILE
    dil1 = B_PATTERNS[1][1]
    nslab = B_GROUP_COLS // LANES
    row_map = lambda bi, i: (bi, i, 0)
    slab_spec = pl.BlockSpec((1, nslab, t, LANES), lambda bi, i: (bi, 0, i, 0))
    chunk_specs = _row_chunk_specs(cast_weights, s // t, b * s // t)
    in_specs = ([pl.BlockSpec(memory_space=pltpu.SMEM), _const_spec(bias_a.shape), _const_spec(bias_b.shape),
                 pl.BlockSpec((1, t, D_MODEL), row_map),
                 pl.BlockSpec((1, t, A_Q_COLS), row_map)]
                + _halo_specs((), t, A_Q_BLOCK, s, A_KVA_COLS)
                + _halo_specs((1,), t, B_HALF_WINDOW, s, B_QKV_COLS)
                + _halo_specs((dil1,), t // dil1, B_HALF_WINDOW, s // dil1, B_QKV_COLS)
                + [slab_spec, slab_spec,
                   _const_spec((1, D_MODEL)),
                   _const_spec((D_MODEL, 2 * D_MODEL)),
                   _const_spec((1, 2 * D_MODEL)),
                   _const_spec((A_Q_COLS, D_MODEL)),
                   _const_spec((B_GROUP_COLS, D_MODEL)),
                   _const_spec((D_MODEL, D_MODEL)),
                   _const_spec((1, D_MODEL))])
    return pl.pallas_call(
        functools.partial(_mix_attn_kernel, n_cast=len(cast_weights)),
        grid=(b, s // t),
        in_specs=in_specs + chunk_specs,
        out_specs=[pl.BlockSpec((1, t, D_MODEL), row_map)] + chunk_specs,
        out_shape=[jax.ShapeDtypeStruct((b, s, D_MODEL), F32)]
        + [jax.ShapeDtypeStruct(w.shape, BF16) for w in cast_weights],
        scratch_shapes=[
            pltpu.VMEM((t, A_Q_COLS), BF16),
            pltpu.VMEM((2, 2, nslab, t, LANES), F32),
        ],
        compiler_params=pltpu.CompilerParams(
            dimension_semantics=("parallel", "parallel"), vmem_limit_bytes=VMEM_LIMIT_BYTES),
        name="mix_attn",
    )(sink, bias_a, bias_b, x, qa, kva, kva, kva, qkv_b0, qkv_b0, qkv_b0, qkv_b1, qkv_b1, qkv_b1,
      ob2[0], ob2[1], g_pre, w_gate, b_gate, w_a, w_b, w_out, g_post, *cast_weights)


def _gelu_tanh(x):
    c = -2.0 * math.sqrt(2.0 / math.pi)
    return x / (1.0 + jnp.exp(x * (c + (c * 0.044715) * (x * x))))


def _conv_ffn_kernel(x_ref, prev_ref, next_ref, g_pre_ref, w_up_ref, conv_w_ref, conv_b_ref,
                     w_down_ref, g_post_ref, out_ref, u_ref, acc_ref, perm_ref):
    t = ROW_TILE
    half = t // 2
    i = pl.program_id(1)
    last = pl.num_programs(1) - 1
    row = lax.broadcasted_iota(jnp.int32, (t + 2 * HALO, 1), 0)
    starts = [sum(FF_CHUNKS[:c]) for c in range(len(FF_CHUNKS))]

    def col_starts(c):
        return (starts[c], D_FF + starts[c])

    def project(h, c, slot):
        for part, col0 in enumerate(col_starts(c)):
            up = _dot(h, w_up_ref[:, col0:col0 + FF_CHUNKS[c]])
            for j in range(FF_CHUNKS[c] // LANES):
                u_ref[2 * slot + part, j] = up[:, j * LANES:(j + 1) * LANES]

    def conv(c, slot, part):
        buf, col0 = 2 * slot + part, col_starts(c)[part]
        pieces = []
        for j in range(FF_CHUNKS[c] // LANES):
            cols = slice(col0 + j * LANES, col0 + (j + 1) * LANES)
            w0, w1, w2 = conv_w_ref[0:1, cols], conv_w_ref[1:2, cols], conv_w_ref[2:3, cols]
            bias = conv_b_ref[:, cols]
            r = [u_ref[buf, j, pl.ds(HALO - 1 + k, half, stride=2), :] for k in range(4)]
            even = r[0] * w0 + r[1] * w1 + r[2] * w2 + bias
            odd = r[1] * w0 + r[2] * w1 + r[3] * w2 + bias
            pieces.append(jnp.concatenate([even, odd], axis=0))
        return jnp.concatenate(pieces, axis=-1)

    def tile(sub, carry):
        lo = pl.multiple_of(sub * t, t)
        x = x_ref[0, pl.ds(lo, t), :]
        first, final = sub == 0, sub == FFN_SUB_TILES - 1
        inner_before = x_ref[0, pl.ds(pl.multiple_of(jnp.maximum(lo - HALO, 0), HALO), HALO), :]
        inner_after = x_ref[0, pl.ds(pl.multiple_of(jnp.minimum(lo + t, (FFN_SUB_TILES - 1) * t), HALO), HALO), :]
        before = jnp.where(first, prev_ref[0], inner_before)
        after = jnp.where(final, next_ref[0], inner_after)
        hn = _rms_norm(jnp.concatenate([before, x, after], axis=0), g_pre_ref[...])
        hn = jnp.where(jnp.logical_or(row >= HALO, jnp.logical_or(i > 0, sub > 0)), hn, 0.0)
        hn = jnp.where(jnp.logical_or(row < HALO + t, jnp.logical_or(i < last, sub < FFN_SUB_TILES - 1)), hn, 0.0)
        h = hn.astype(BF16)

        n_chunks = len(FF_CHUNKS)
        for c in range(min(FF_LOOKAHEAD, n_chunks)):
            project(h, c, c % (FF_LOOKAHEAD + 1))
        for c in range(n_chunks):
            if c + FF_LOOKAHEAD < n_chunks:
                project(h, c + FF_LOOKAHEAD, (c + FF_LOOKAHEAD) % (FF_LOOKAHEAD + 1))
            slot = c % (FF_LOOKAHEAD + 1)
            act = (_gelu_tanh(conv(c, slot, 0)) * conv(c, slot, 1)).astype(BF16)
            part = _dot(act, w_down_ref[starts[c]:starts[c] + FF_CHUNKS[c], :])
            if c == 0:
                acc_ref[...] = part
            else:
                acc_ref[...] += part
        y = _rms_norm(acc_ref[...], g_post_ref[...])
        for j in range(D_MODEL // LANES):
            perm_ref[j, pl.ds(0, half, stride=2), :] = y[:half, j * LANES:(j + 1) * LANES]
            perm_ref[j, pl.ds(1, half, stride=2), :] = y[half:, j * LANES:(j + 1) * LANES]
        y_nat = jnp.concatenate([perm_ref[j] for j in range(D_MODEL // LANES)], axis=-1)
        out_ref[0, pl.ds(lo, t), :] = x + y_nat
        return carry

    lax.fori_loop(0, FFN_SUB_TILES, tile, 0)


def _conv_ffn(x, g_pre, w_up, conv_w, conv_b, w_down, g_post):
    b, s, _ = x.shape
    t = FFN_SUB_TILES * ROW_TILE
    tiles = s // t
    halo_blocks_per_tile = t // HALO
    n_halo_blocks = s // HALO
    row_map = lambda bi, i: (bi, i, 0)
    prev_map = lambda bi, i: (bi, jnp.maximum(i * halo_blocks_per_tile - 1, 0), 0)
    next_map = lambda bi, i: (bi, jnp.minimum((i + 1) * halo_blocks_per_tile, n_halo_blocks - 1), 0)
    return pl.pallas_call(
        _conv_ffn_kernel,
        grid=(b, tiles),
        in_specs=[
            pl.BlockSpec((1, t, D_MODEL), row_map),
            pl.BlockSpec((1, HALO, D_MODEL), prev_map),
            pl.BlockSpec((1, HALO, D_MODEL), next_map),
            _const_spec((1, D_MODEL)),
            _const_spec((D_MODEL, 2 * D_FF)),
            _const_spec((3, 2 * D_FF)),
            _const_spec((1, 2 * D_FF)),
            _const_spec((D_FF, D_MODEL)),
            _const_spec((1, D_MODEL)),
        ],
        out_specs=pl.BlockSpec((1, t, D_MODEL), row_map),
        out_shape=jax.ShapeDtypeStruct((b, s, D_MODEL), F32),
        scratch_shapes=[
            pltpu.VMEM((2 * (FF_LOOKAHEAD + 1), FF_CHUNK // LANES, ROW_TILE + 2 * HALO, LANES), F32),
            pltpu.VMEM((ROW_TILE, D_MODEL), F32),
            pltpu.VMEM((D_MODEL // LANES, ROW_TILE, LANES), F32),
        ],
        compiler_params=pltpu.CompilerParams(
            dimension_semantics=("parallel", "parallel"), vmem_limit_bytes=VMEM_LIMIT_BYTES),
        name="conv_ffn",
    )(x, x, x, g_pre, w_up, conv_w, conv_b, w_down, g_post)


def _rope_tables(seq_len):
    half = HEAD_DIM // 2
    inv = ROPE_THETA ** (-np.arange(half, dtype=np.float64) / half)
    ang = np.arange(seq_len, dtype=np.float64)[:, None] * inv[None, :]
    cos, sin = np.cos(ang), np.sin(ang)
    cos_head = np.concatenate([cos, cos], axis=-1)
    sin_head = np.concatenate([-sin, sin], axis=-1)
    reps = LANES // HEAD_DIM
    return (jnp.asarray(np.tile(cos_head, (1, reps)), dtype=F32),
            jnp.asarray(np.tile(sin_head, (1, reps)), dtype=F32))


def _band_bias(q_rows, stacked, keys, half_window, lead):
    i = (np.arange(stacked * q_rows) % q_rows)[None, :, None]
    j = np.arange(keys)[None, None, :]
    e = np.arange(3)[:, None, None]
    band = np.abs(j - lead - i) <= half_window
    inside = np.logical_and(np.logical_or(e != 0, j >= lead), np.logical_or(e != 2, j < lead + q_rows))
    return jnp.asarray(np.where(np.logical_and(band, inside), 0.0, NEG_INF), dtype=F32)


def _far_band_bias():
    off = np.arange(3)[:, None, None] * B_HALF_WINDOW
    i = (np.arange(2 * B_Q_BLOCK) % B_Q_BLOCK)[None, :, None]
    j = np.arange(B_KEY_WINDOW)[None, None, :]
    return jnp.asarray(np.where(np.abs(j - off - i) <= B_HALF_WINDOW, 0.0, NEG_INF), dtype=F32)


def kernel(x, norm_mix_pre, w_in, sink, w_branch_a, w_branch_b, w_gate, b_gate, w_out,
           norm_mix_post, norm_ffn_pre, w_up, conv_w, conv_b, w_down, norm_ffn_post):
    b, s, d = x.shape
    assert d == D_MODEL and s % B_SPAN == 0 and s % (SUB_TILES * ROW_TILE) == 0
    cos, sin_signed = _rope_tables(s)
    bias_a = _band_bias(A_Q_BLOCK, 1, A_KEY_WINDOW, A_HALF_WINDOW, A_Q_BLOCK)
    bias_b = _band_bias(B_Q_BLOCK, 2, B_KEY_WINDOW, B_HALF_WINDOW, B_HALF_WINDOW)
    bias_b_far = _far_band_bias()
    for layer in range(norm_mix_pre.shape[0]):
        proj = _in_proj(x, norm_mix_pre[layer][None], w_in[layer].astype(BF16), cos, sin_signed)
        qa, kva, qkv_b0, qkv_b1, qkv_b2 = proj
        o2, l2, w_gate_h, w_a_h, w_b_h, w_out_h = _attn_b(
            bias_b_far, qkv_b2, B_PATTERNS[2][1],
            (w_gate[layer], w_branch_a[layer], w_branch_b[layer], w_out[layer]))
        ob2 = (o2, l2)
        x, w_up_h, w_down_h = _mix_attn(
                      x, sink[layer], bias_a, bias_b, qa, kva, qkv_b0, qkv_b1, ob2,
                      norm_mix_pre[layer][None],
                      w_gate_h, b_gate[layer][None], w_a_h, w_b_h, w_out_h, norm_mix_post[layer][None],
                      (w_up[layer], w_down[layer]))
        x = _conv_ffn(x, norm_ffn_pre[layer][None],
                      w_up_h, conv_w[layer], conv_b[layer][None], w_down_h,
                      norm_ffn_post[layer][None])
    return x
```

```python
import functools
import math

import jax
import jax.numpy as jnp
import numpy as np
from jax import lax
from jax.experimental import pallas as pl
from jax.experimental.pallas import tpu as pltpu

D_MODEL = 1024
HEAD_DIM = 64
A_Q_HEADS = 8
A_KV_HEADS = 2
A_GROUP = A_Q_HEADS // A_KV_HEADS
A_HALF_WINDOW = 128
B_PATTERNS = ((128, 1), (512, 4), (2048, 16))
B_N_GROUPS = len(B_PATTERNS)
B_HEADS = 4
HEAD_ORDER = (0, 2, 1, 3)
B_HALF_WINDOW = 64
ROPE_THETA = 10000.0
D_FF = 3 * D_MODEL
RMS_EPS = 1e-6
NEG_INF = -1e30
LOG2_E = math.log2(math.e)

A_Q_COLS = A_Q_HEADS * HEAD_DIM
A_KV_COLS = A_KV_HEADS * HEAD_DIM
A_COLS = A_Q_COLS + 2 * A_KV_COLS
B_GROUP_COLS = B_HEADS * HEAD_DIM
B_PROJ_COLS = B_N_GROUPS * B_GROUP_COLS
IN_COLS = A_COLS + 3 * B_PROJ_COLS
A_KVA_COLS = 4 * A_KV_COLS
B_QKV_COLS = 3 * B_GROUP_COLS

LANES = 128
VMEM_LIMIT_BYTES = 56 * 1024 * 1024

ROW_TILE = 512
SUB_TILES = 2
FFN_SUB_TILES = 2
A_Q_BLOCK = 128
A_KEY_WINDOW = 3 * A_Q_BLOCK
B_Q_BLOCK = 128
B_KEY_WINDOW = B_Q_BLOCK + 2 * B_HALF_WINDOW
B_SPAN = 2048
B_UNROLL = 8
FF_CHUNK = 1024
FF_CHUNKS = (FF_CHUNK,) * (D_FF // FF_CHUNK)
FF_LOOKAHEAD = 2
HALO = 8
DEINTERLEAVE_STRIDE = 4

BF16 = jnp.bfloat16
F32 = jnp.float32


def _dot(a, b):
    return jnp.dot(a, b, preferred_element_type=F32)


def _dot_nt(a, b):
    return lax.dot_general(a, b, (((1,), (1,)), ((), ())), preferred_element_type=F32)


def _rms_norm(x, gain):
    ms = jnp.mean(x * x, axis=-1, keepdims=True)
    return x * lax.rsqrt(ms + RMS_EPS) * gain


def _rope_chunk(p, cos, sin_signed, first_half):
    partner = jnp.where(first_half, pltpu.roll(p, 96, 1), pltpu.roll(p, 32, 1))
    return p * cos + partner * sin_signed


def _const_spec(shape):
    nd = len(shape)
    return pl.BlockSpec(shape, lambda *_: (0,) * nd, pipeline_mode=pl.Buffered(1))


def _row_chunk_specs(weights, steps_per_seq, n_steps):
    return [pl.BlockSpec((w.shape[0] // n_steps, w.shape[1]), lambda bi, i: (bi * steps_per_seq + i, 0))
            for w in weights]


def _cast_chunks(src_refs, dst_refs):
    for src_ref, dst_ref in zip(src_refs, dst_refs):
        dst_ref[...] = src_ref[...].astype(BF16)


def _in_proj_kernel(x_ref, gain_ref, w_f32_ref, cos_ref, sin_ref,
                    qa_ref, kva_ref, qkv0_ref, qkv1_ref, qkv2_ref, slab_ref, stage_ref, w_ref):
    t = ROW_TILE

    @pl.when(jnp.logical_and(pl.program_id(0) == 0, pl.program_id(1) == 0))
    def _():
        w_ref[...] = w_f32_ref[...].astype(BF16)

    lane = lax.broadcasted_iota(jnp.int32, (t, LANES), 1)
    first_half = (lane % HEAD_DIM) < (HEAD_DIM // 2)
    low_head = lane < HEAD_DIM
    scale = HEAD_DIM ** -0.5 * LOG2_E
    out_refs = (qkv0_ref, qkv1_ref, qkv2_ref)

    def tile(sub):
        rows = slice(sub * t, (sub + 1) * t)
        h = _rms_norm(x_ref[0, rows, :], gain_ref[...]).astype(BF16)
        cos = cos_ref[rows, :]
        sin_signed = sin_ref[rows, :]

        def proj(col0, ncols):
            return _dot(h, w_ref[:, col0:col0 + ncols])

        def rope(p, mult=None):
            chunks = []
            for c in range(p.shape[1] // LANES):
                r = _rope_chunk(p[:, c * LANES:(c + 1) * LANES], cos, sin_signed, first_half)
                chunks.append(r if mult is None else r * mult)
            return chunks

        def dup_heads(p):
            swapped = pltpu.roll(p, HEAD_DIM, 1)
            return [jnp.where(low_head, p, swapped), jnp.where(low_head, swapped, p)]

        def windowed():
            for c, chunk in enumerate(rope(proj(0, A_Q_COLS), scale)):
                qa_ref[0, rows, c * LANES:(c + 1) * LANES] = chunk.astype(BF16)
            kv = proj(A_Q_COLS, 2 * A_KV_COLS)
            ka = rope(kv[:, :A_KV_COLS])[0]
            for c, chunk in enumerate(dup_heads(ka) + dup_heads(kv[:, A_KV_COLS:])):
                kva_ref[0, rows, c * LANES:(c + 1) * LANES] = chunk.astype(BF16)

        def dilated(g):
            dil = B_PATTERNS[g][1]
            for kind in range(3):
                col0 = A_COLS + kind * B_PROJ_COLS + g * B_GROUP_COLS
                p = proj(col0, B_GROUP_COLS)
                if kind == 0:
                    chunks = rope(p, scale)
                elif kind == 1:
                    chunks = rope(p)
                else:
                    chunks = [p[:, c * LANES:(c + 1) * LANES] for c in range(B_GROUP_COLS // LANES)]
                o_ref = out_refs[g]
                sub_rows = slice(sub * t // dil, (sub + 1) * t // dil)
                col = lambda c: slice(kind * B_GROUP_COLS + c * LANES, kind * B_GROUP_COLS + (c + 1) * LANES)
                if dil == 1:
                    for c, chunk in enumerate(chunks):
                        o_ref[0, 0, sub_rows, col(c)] = chunk.astype(BF16)
                else:
                    for c, chunk in enumerate(chunks):
                        slab_ref[sub, kind, c] = chunk
                    for c in range(len(chunks)):
                        src = slab_ref.at[sub, kind, c]
                        if dil > DEINTERLEAVE_STRIDE:
                            assert dil == DEINTERLEAVE_STRIDE ** 2
                            for a in range(DEINTERLEAVE_STRIDE):
                                stage_ref[sub, kind, c, a] = src[pl.ds(a, t // DEINTERLEAVE_STRIDE,
                                                                       stride=DEINTERLEAVE_STRIDE), :]
                        for res in range(dil):
                            if dil > DEINTERLEAVE_STRIDE:
                                picked = stage_ref[sub, kind, c, res % DEINTERLEAVE_STRIDE,
                                                   pl.ds(res // DEINTERLEAVE_STRIDE, t // dil,
                                                         stride=DEINTERLEAVE_STRIDE), :]
                            else:
                                picked = src[pl.ds(res, t // dil, stride=dil), :]
                            o_ref[0, res, sub_rows, col(c)] = picked.astype(BF16)

        dilated(2)
        dilated(1)
        windowed()
        dilated(0)

    for sub in range(SUB_TILES):
        tile(sub)


def _in_proj(x, gain, w_in, cos, sin_signed):
    b, s, _ = x.shape
    t = SUB_TILES * ROW_TILE
    grid = (b, s // t)
    row_map = lambda bi, i: (bi, i, 0)
    out_shape = [
        jax.ShapeDtypeStruct((b, s, A_Q_COLS), BF16),
        jax.ShapeDtypeStruct((b, s, A_KVA_COLS), BF16),
    ]
    out_specs = [
        pl.BlockSpec((1, t, A_Q_COLS), row_map),
        pl.BlockSpec((1, t, A_KVA_COLS), row_map),
    ]
    for _, dil in B_PATTERNS:
        out_shape.append(jax.ShapeDtypeStruct((b, dil, s // dil, B_QKV_COLS), BF16))
        out_specs.append(pl.BlockSpec((1, dil, t // dil, B_QKV_COLS), lambda bi, i: (bi, 0, i, 0)))
    return pl.pallas_call(
        _in_proj_kernel,
        grid=grid,
        in_specs=[
            pl.BlockSpec((1, t, D_MODEL), row_map),
            _const_spec((1, D_MODEL)),
            _const_spec((D_MODEL, IN_COLS)),
            pl.BlockSpec((t, LANES), lambda bi, i: (i, 0)),
            pl.BlockSpec((t, LANES), lambda bi, i: (i, 0)),
        ],
        out_specs=out_specs,
        out_shape=out_shape,
        scratch_shapes=[
            pltpu.VMEM((SUB_TILES, 3, B_GROUP_COLS // LANES, ROW_TILE, LANES), F32),
            pltpu.VMEM((SUB_TILES, 3, B_GROUP_COLS // LANES, DEINTERLEAVE_STRIDE,
                        ROW_TILE // DEINTERLEAVE_STRIDE, LANES), F32),
            pltpu.VMEM((D_MODEL, IN_COLS), BF16),
        ],
        compiler_params=pltpu.CompilerParams(
            dimension_semantics=("arbitrary", "arbitrary"), vmem_limit_bytes=VMEM_LIMIT_BYTES),
        name="in_proj",
    )(x, gain, w_in, cos, sin_signed)


def _attn_b_kernel(bias_ref, q_ref, kv_ref, *rest, dil, n_cast):
    o_ref, lse_ref = rest[n_cast:n_cast + 2]
    _cast_chunks(rest[:n_cast], rest[n_cast + 2:2 * n_cast + 2])
    two_pass = dil > DEINTERLEAVE_STRIDE
    if two_pass:
        assert dil == DEINTERLEAVE_STRIDE ** 2
        stage_ref = rest[2 * n_cast + 2]
    sub_len = kv_ref.shape[2]
    sub_rows = q_ref.shape[2]
    blocks = sub_rows // B_Q_BLOCK
    step = pl.program_id(1)
    low_head = lax.broadcasted_iota(jnp.int32, (B_Q_BLOCK, LANES), 1) < HEAD_DIM
    low_key = lax.broadcasted_iota(jnp.int32, (B_KEY_WINDOW, LANES), 1) < HEAD_DIM

    def units(it, carry):
        work = []
        for uu in range(B_UNROLL):
            u = it * B_UNROLL + uu
            res = u // blocks
            r0 = (u % blocks) * B_Q_BLOCK
            q_pos = (step * sub_rows + r0) // B_HALF_WINDOW
            start_pos = jnp.clip(q_pos - 1, 0, (sub_len - B_KEY_WINDOW) // B_HALF_WINDOW)
            start = start_pos * B_HALF_WINDOW
            bias = bias_ref[q_pos - start_pos]
            q_blk = q_ref[0, res, pl.ds(r0, B_Q_BLOCK), 0:B_GROUP_COLS]
            k_win = kv_ref[0, res, pl.ds(start, B_KEY_WINDOW), B_GROUP_COLS:2 * B_GROUP_COLS]
            v_win = kv_ref[0, res, pl.ds(start, B_KEY_WINDOW), 2 * B_GROUP_COLS:3 * B_GROUP_COLS]
            if dil == 1:
                dst = (None, pl.ds(r0, B_Q_BLOCK))
            elif two_pass:
                dst = (res % DEINTERLEAVE_STRIDE,
                       pl.ds(r0 * DEINTERLEAVE_STRIDE + res // DEINTERLEAVE_STRIDE, B_Q_BLOCK,
                             stride=DEINTERLEAVE_STRIDE))
            else:
                dst = (None, pl.ds(r0 * dil + res, B_Q_BLOCK, stride=dil))
            rows = dst
            for pr in range(B_HEADS // 2):
                qp = q_blk[:, pr * LANES:(pr + 1) * LANES]
                zero = jnp.zeros_like(qp)
                lhs = jnp.concatenate([jnp.where(low_head, qp, zero), jnp.where(low_head, zero, qp)], axis=0)
                sc = _dot_nt(lhs, k_win[:, pr * LANES:(pr + 1) * LANES])
                work.append((rows, pr, bias, v_win[:, pr * LANES:(pr + 1) * LANES], sc))

        soft = []
        for rows, pr, bias, vp, sc in work:
            sc = sc + bias
            m = jnp.max(sc, axis=-1, keepdims=True)
            soft.append((rows, pr, vp, m, jnp.exp2(sc - m).astype(BF16)))

        for rows, pr, vp, m, p in soft:
            one = jnp.ones_like(vp)
            top = _dot(p[:B_Q_BLOCK], jnp.where(low_key, vp, one))
            bot = _dot(p[B_Q_BLOCK:], jnp.where(low_key, one, vp))
            denom = pltpu.roll(jnp.where(low_head, bot, top), HEAD_DIM, 1)
            o_val = jnp.where(low_head, top, bot) / denom
            lse_val = jnp.where(low_head, m[:B_Q_BLOCK], m[B_Q_BLOCK:]) + jnp.log2(denom)
            cls, where = rows
            if two_pass:
                stage_ref[0, pr, cls, where, :] = o_val
                stage_ref[1, pr, cls, where, :] = lse_val
            else:
                o_ref[0, pr, where, :] = o_val
                lse_ref[0, pr, where, :] = lse_val
        return carry

    lax.fori_loop(0, dil * blocks // B_UNROLL, units, 0)
    if two_pass:
        for which, out in enumerate((o_ref, lse_ref)):
            for pr in range(B_HEADS // 2):
                for a in range(DEINTERLEAVE_STRIDE):
                    out[0, pr, pl.ds(a, DEINTERLEAVE_STRIDE * sub_rows, stride=DEINTERLEAVE_STRIDE), :] = (
                        stage_ref[which, pr, a])


def _attn_b(bias, qkv, dil, cast_weights=()):
    b, _, sub_len, _ = qkv.shape
    s = sub_len * dil
    sub_rows = B_SPAN // dil
    nslab = B_GROUP_COLS // LANES
    out_sds = jax.ShapeDtypeStruct((b, nslab, s, LANES), F32)
    out_spec = pl.BlockSpec((1, nslab, B_SPAN, LANES), lambda bi, i: (bi, 0, i, 0))
    steps_per_seq = s // B_SPAN
    chunk_specs = _row_chunk_specs(cast_weights, steps_per_seq, b * steps_per_seq)
    return pl.pallas_call(
        functools.partial(_attn_b_kernel, dil=dil, n_cast=len(cast_weights)),
        grid=(b, steps_per_seq),
        in_specs=[
            _const_spec(bias.shape),
            pl.BlockSpec((1, dil, sub_rows, B_QKV_COLS), lambda bi, i: (bi, 0, i, 0)),
            pl.BlockSpec((1, dil, sub_len, B_QKV_COLS), lambda bi, i: (bi, 0, 0, 0)),
        ] + chunk_specs,
        out_specs=[out_spec, out_spec] + chunk_specs,
        out_shape=[out_sds, out_sds] + [jax.ShapeDtypeStruct(w.shape, BF16) for w in cast_weights],
        scratch_shapes=([pltpu.VMEM((2, nslab, DEINTERLEAVE_STRIDE, DEINTERLEAVE_STRIDE * sub_rows, LANES), F32)]
                        if dil > DEINTERLEAVE_STRIDE else []),
        compiler_params=pltpu.CompilerParams(
            dimension_semantics=("parallel", "parallel"), vmem_limit_bytes=VMEM_LIMIT_BYTES),
        name=f"attn_b_d{dil}",
    )(bias, qkv, qkv, *cast_weights)


def _window(main_ref, prev_ref, next_ref, lead, lo, hi, length, cols):
    parts = []
    if lo < 0:
        halo = prev_ref.shape[-2]
        parts.append(prev_ref[lead + (slice(halo + lo, halo), cols)])
        lo = 0
    parts.append(main_ref[lead + (slice(lo, min(hi, length)), cols)])
    if hi > length:
        parts.append(next_ref[lead + (slice(0, hi - length), cols)])
    return parts[0] if len(parts) == 1 else jnp.concatenate(parts, axis=0)


def _mix_attn_kernel(sink_ref, bias_a_ref, bias_b_ref, x_ref,
                     qa_ref, kva_ref, kva_p_ref, kva_n_ref,
                     b0_ref, b0_p_ref, b0_n_ref, b1_ref, b1_p_ref, b1_n_ref,
                     o2_ref, l2_ref,
                     g_pre_ref, w_gate_ref, b_gate_ref, w_a_ref, w_b_ref, w_out_ref, g_post_ref,
                     *rest, n_cast):
    out_ref = rest[n_cast]
    ya_ref, ob_ref = rest[2 * n_cast + 1:]
    _cast_chunks(rest[:n_cast], rest[n_cast + 1:2 * n_cast + 1])
    t = ROW_TILE
    step_rows = t
    i = pl.program_id(1)
    last = pl.num_programs(1) - 1
    low_head = lax.broadcasted_iota(jnp.int32, (A_Q_BLOCK, LANES), 1) < HEAD_DIM
    dil1 = B_PATTERNS[1][1]
    sub_rows = step_rows // dil1

    def edge_index(first, final):
        idx = 1
        if first:
            idx = jnp.where(i == 0, 0, idx)
        if final:
            idx = jnp.where(i == last, 2, idx)
        return idx

    def scores(grp_idx):
        work_a, work_b = [], []
        n = grp_idx
        lo, hi = (n - 1) * A_Q_BLOCK, (n + 2) * A_Q_BLOCK
        bias = bias_a_ref[edge_index(n == 0, hi > step_rows)]
        q_blk = qa_ref[0, n * A_Q_BLOCK:(n + 1) * A_Q_BLOCK, :]
        for j in range(A_KV_HEADS):
            kcols = slice(j * LANES, (j + 1) * LANES)
            vcols = slice(2 * A_KV_COLS + j * LANES, 2 * A_KV_COLS + (j + 1) * LANES)
            kd = _window(kva_ref, kva_p_ref, kva_n_ref, (0,), lo, hi, step_rows, kcols)
            vd = _window(kva_ref, kva_p_ref, kva_n_ref, (0,), lo, hi, step_rows, vcols)
            lhs = []
            for g in HEAD_ORDER:
                c0 = j * A_GROUP * HEAD_DIM + (g // 2) * LANES
                pair = q_blk[:, c0:c0 + LANES]
                keep = low_head if g % 2 == 0 else jnp.logical_not(low_head)
                lhs.append(jnp.where(keep, pair, jnp.zeros_like(pair)))
            sc = _dot_nt(jnp.concatenate(lhs, axis=0), kd)
            work_a.append((n, j, bias, vd, sc))
        units = [(0, (0, 0), grp_idx, step_rows), (1, (0, grp_idx), 0, sub_rows)]
        for grp, lead, n, length in units:
            m_ref, p_ref, n_ref = (b0_ref, b0_p_ref, b0_n_ref) if grp == 0 else (b1_ref, b1_p_ref, b1_n_ref)
            lo, hi = n * B_Q_BLOCK - B_HALF_WINDOW, (n + 1) * B_Q_BLOCK + B_HALF_WINDOW
            bias = bias_b_ref[edge_index(lo < 0, hi > length)]
            if grp == 0:
                rows = pl.ds(n * B_Q_BLOCK, B_Q_BLOCK)
            else:
                rows = pl.ds(n * B_Q_BLOCK * dil1 + lead[1], B_Q_BLOCK, stride=dil1)
            for pr in range(B_HEADS // 2):
                cols = [slice(kind * B_GROUP_COLS + pr * LANES, kind * B_GROUP_COLS + (pr + 1) * LANES)
                        for kind in range(3)]
                qp = m_ref[lead + (slice(n * B_Q_BLOCK, (n + 1) * B_Q_BLOCK), cols[0])]
                kp = _window(m_ref, p_ref, n_ref, lead, lo, hi, length, cols[1])
                vp = _window(m_ref, p_ref, n_ref, lead, lo, hi, length, cols[2])
                zero = jnp.zeros_like(qp)
                lhs = jnp.concatenate([jnp.where(low_head, qp, zero), jnp.where(low_head, zero, qp)], axis=0)
                work_b.append((grp, rows, pr, bias, vp, _dot_nt(lhs, kp)))
        return work_a, work_b

    def softmaxes(work):
        work_a, work_b = work
        soft_a, soft_b = [], []
        for n, j, bias, vd, sc in work_a:
            probs, denoms = {}, {}
            for slot, g in enumerate(HEAD_ORDER):
                sg = sc[slot * A_Q_BLOCK:(slot + 1) * A_Q_BLOCK]
                sg = jnp.concatenate([sg[:, :A_Q_BLOCK] + bias[:, :A_Q_BLOCK],
                                      sg[:, A_Q_BLOCK:2 * A_Q_BLOCK],
                                      sg[:, 2 * A_Q_BLOCK:] + bias[:, 2 * A_Q_BLOCK:]], axis=-1)
                sink = sink_ref[j * A_GROUP + g] * LOG2_E
                m = jnp.maximum(jnp.max(sg, axis=-1, keepdims=True), sink)
                p = jnp.exp2(sg - m)
                probs[g] = p.astype(BF16)
                denoms[g] = jnp.sum(p, axis=-1, keepdims=True) + jnp.exp2(sink - m)
            soft_a.append((n, j, vd, probs, denoms))
        for grp, rows, pr, bias, vp, sc in work_b:
            sc = sc + bias
            m = jnp.max(sc, axis=-1, keepdims=True)
            p = jnp.exp2(sc - m)
            soft_b.append((grp, rows, pr, vp, (m, jnp.sum(p, axis=-1, keepdims=True)), p.astype(BF16)))
        return soft_a, soft_b

    def values(soft):
        soft_a, soft_b = soft
        pairs = range(A_GROUP // 2)
        for n, j, vd, probs, denoms in soft_a:
            r = _dot(jnp.concatenate([probs[g] for g in range(A_GROUP)], axis=0), vd)
            for pr in pairs:
                even = r[(2 * pr) * A_Q_BLOCK:(2 * pr + 1) * A_Q_BLOCK] / denoms[2 * pr]
                odd = r[(2 * pr + 1) * A_Q_BLOCK:(2 * pr + 2) * A_Q_BLOCK] / denoms[2 * pr + 1]
                c0 = j * A_GROUP * HEAD_DIM + pr * LANES
                ya_ref[n * A_Q_BLOCK:(n + 1) * A_Q_BLOCK, c0:c0 + LANES] = (
                    jnp.where(low_head, even, odd)).astype(BF16)
        for grp, rows, pr, vp, (m, l), p in soft_b:
            o = _dot(p, vp) / l
            lse = m + jnp.log2(l)
            ob_ref[grp, 0, pr, rows, :] = jnp.where(low_head, o[:B_Q_BLOCK], o[B_Q_BLOCK:])
            ob_ref[grp, 1, pr, rows, :] = jnp.where(low_head, lse[:B_Q_BLOCK], lse[B_Q_BLOCK:])

    n_groups = t // A_Q_BLOCK
    assert step_rows == t and n_groups == dil1 == t // B_Q_BLOCK
    gate_chunk = 2 * D_MODEL // n_groups
    x = x_ref[0]
    h = _rms_norm(x, g_pre_ref[...]).astype(BF16)

    def gate_logits(j):
        cols = slice(j * gate_chunk, (j + 1) * gate_chunk)
        return _dot(h, w_gate_ref[:, cols]) + b_gate_ref[:, cols]

    z = [None] * n_groups
    work = scores(0)
    z[0] = gate_logits(0)
    soft = softmaxes(work)
    work = scores(1)
    values(soft)
    z[1] = gate_logits(1)
    soft = softmaxes(work)
    work = scores(2)
    values(soft)
    soft = softmaxes(work)
    work = scores(3)
    values(soft)
    z[2] = gate_logits(2)
    z[3] = gate_logits(3)
    values(softmaxes(work))

    yb = []
    for c in range(B_GROUP_COLS // LANES):
        l0, l1, l2 = ob_ref[0, 1, c], ob_ref[1, 1, c], l2_ref[0, c]
        m = jnp.maximum(jnp.maximum(l0, l1), l2)
        e0, e1, e2 = jnp.exp2(l0 - m), jnp.exp2(l1 - m), jnp.exp2(l2 - m)
        num = e0 * ob_ref[0, 0, c] + e1 * ob_ref[1, 0, c] + e2 * o2_ref[0, c]
        yb.append((num / (e0 + e1 + e2)).astype(BF16))
    yb = jnp.concatenate(yb, axis=-1)
    gates = jax.nn.sigmoid(jnp.concatenate(z, axis=-1))
    merged = (gates[:, :D_MODEL] * _dot(ya_ref[...], w_a_ref[...])
              + gates[:, D_MODEL:] * _dot(yb, w_b_ref[...]))
    mix = _dot(merged.astype(BF16), w_out_ref[...])
    out_ref[0] = x + _rms_norm(mix, g_post_ref[...])


def _halo_specs(lead_blocks, rows, halo, total_rows, cols):
    nlead = len(lead_blocks)
    per = rows // halo
    nhalo = total_rows // halo
    zeros = (0,) * nlead
    main = pl.BlockSpec((1,) + lead_blocks + (rows, cols), lambda bi, i: (bi,) + zeros + (i, 0))
    prev = pl.BlockSpec((1,) + lead_blocks + (halo, cols),
                        lambda bi, i: (bi,) + zeros + (jnp.maximum(i * per - 1, 0), 0))
    nxt = pl.BlockSpec((1,) + lead_blocks + (halo, cols),
                       lambda bi, i: (bi,) + zeros + (jnp.minimum((i + 1) * per, nhalo - 1), 0))
    return [main, prev, nxt]


def _mix_attn(x, sink, bias_a, bias_b, qa, kva, qkv_b0, qkv_b1, ob2,
              g_pre, w_gate, b_gate, w_a, w_b, w_out, g_post, cast_weights=()):
    b, s, _ = x.shape
    t = ROW_TILE
    dil1 = B_PATTERNS[1][1]
    nslab = B_GROUP_COLS // LANES
    row_map = lambda bi, i: (bi, i, 0)
    slab_spec = pl.BlockSpec((1, nslab, t, LANES), lambda bi, i: (bi, 0, i, 0))
    chunk_specs = _row_chunk_specs(cast_weights, s // t, b * s // t)
    in_specs = ([pl.BlockSpec(memory_space=pltpu.SMEM), _const_spec(bias_a.shape), _const_spec(bias_b.shape),
                 pl.BlockSpec((1, t, D_MODEL), row_map),
                 pl.BlockSpec((1, t, A_Q_COLS), row_map)]
                + _halo_specs((), t, A_Q_BLOCK, s, A_KVA_COLS)
                + _halo_specs((1,), t, B_HALF_WINDOW, s, B_QKV_COLS)
                + _halo_specs((dil1,), t // dil1, B_HALF_WINDOW, s // dil1, B_QKV_COLS)
                + [slab_spec, slab_spec,
                   _const_spec((1, D_MODEL)),
                   _const_spec((D_MODEL, 2 * D_MODEL)),
                   _const_spec((1, 2 * D_MODEL)),
                   _const_spec((A_Q_COLS, D_MODEL)),
                   _const_spec((B_GROUP_COLS, D_MODEL)),
                   _const_spec((D_MODEL, D_MODEL)),
                   _const_spec((1, D_MODEL))])
    return pl.pallas_call(
        functools.partial(_mix_attn_kernel, n_cast=len(cast_weights)),
        grid=(b, s // t),
        in_specs=in_specs + chunk_specs,
        out_specs=[pl.BlockSpec((1, t, D_MODEL), row_map)] + chunk_specs,
        out_shape=[jax.ShapeDtypeStruct((b, s, D_MODEL), F32)]
        + [jax.ShapeDtypeStruct(w.shape, BF16) for w in cast_weights],
        scratch_shapes=[
            pltpu.VMEM((t, A_Q_COLS), BF16),
            pltpu.VMEM((2, 2, nslab, t, LANES), F32),
        ],
        compiler_params=pltpu.CompilerParams(
            dimension_semantics=("parallel", "parallel"), vmem_limit_bytes=VMEM_LIMIT_BYTES),
        name="mix_attn",
    )(sink, bias_a, bias_b, x, qa, kva, kva, kva, qkv_b0, qkv_b0, qkv_b0, qkv_b1, qkv_b1, qkv_b1,
      ob2[0], ob2[1], g_pre, w_gate, b_gate, w_a, w_b, w_out, g_post, *cast_weights)


def _gelu_tanh(x):
    c = -2.0 * math.sqrt(2.0 / math.pi) * LOG2_E
    return x / (1.0 + jnp.exp2(x * (c + (c * 0.044715) * (x * x))))


def _conv_ffn_kernel(x_ref, prev_ref, next_ref, g_pre_ref, w_up_ref, conv_w_ref, conv_b_ref,
                     w_down_ref, g_post_ref, out_ref, u_ref, acc_ref, perm_ref):
    t = ROW_TILE
    half = t // 2
    i = pl.program_id(1)
    last = pl.num_programs(1) - 1
    row = lax.broadcasted_iota(jnp.int32, (t + 2 * HALO, 1), 0)
    starts = [sum(FF_CHUNKS[:c]) for c in range(len(FF_CHUNKS))]

    def col_starts(c):
        return (starts[c], D_FF + starts[c])

    def project(h, c, slot):
        for part, col0 in enumerate(col_starts(c)):
            up = _dot(h, w_up_ref[:, col0:col0 + FF_CHUNKS[c]])
            for j in range(FF_CHUNKS[c] // LANES):
                u_ref[2 * slot + part, j] = up[:, j * LANES:(j + 1) * LANES]

    def conv(c, slot, part):
        buf, col0 = 2 * slot + part, col_starts(c)[part]
        pieces = []
        for j in range(FF_CHUNKS[c] // LANES):
            cols = slice(col0 + j * LANES, col0 + (j + 1) * LANES)
            w0, w1, w2 = conv_w_ref[0:1, cols], conv_w_ref[1:2, cols], conv_w_ref[2:3, cols]
            bias = conv_b_ref[:, cols]
            r = [u_ref[buf, j, pl.ds(HALO - 1 + k, half, stride=2), :] for k in range(4)]
            even = r[0] * w0 + r[1] * w1 + r[2] * w2 + bias
            odd = r[1] * w0 + r[2] * w1 + r[3] * w2 + bias
            pieces.append(jnp.concatenate([even, odd], axis=0))
        return jnp.concatenate(pieces, axis=-1)

    def tile(sub, carry):
        lo = pl.multiple_of(sub * t, t)
        x = x_ref[0, pl.ds(lo, t), :]
        first, final = sub == 0, sub == FFN_SUB_TILES - 1
        inner_before = x_ref[0, pl.ds(pl.multiple_of(jnp.maximum(lo - HALO, 0), HALO), HALO), :]
        inner_after = x_ref[0, pl.ds(pl.multiple_of(jnp.minimum(lo + t, (FFN_SUB_TILES - 1) * t), HALO), HALO), :]
        before = jnp.where(first, prev_ref[0], inner_before)
        after = jnp.where(final, next_ref[0], inner_after)
        hn = _rms_norm(jnp.concatenate([before, x, after], axis=0), g_pre_ref[...])
        hn = jnp.where(jnp.logical_or(row >= HALO, jnp.logical_or(i > 0, sub > 0)), hn, 0.0)
        hn = jnp.where(jnp.logical_or(row < HALO + t, jnp.logical_or(i < last, sub < FFN_SUB_TILES - 1)), hn, 0.0)
        h = hn.astype(BF16)

        n_chunks = len(FF_CHUNKS)
        for c in range(min(FF_LOOKAHEAD, n_chunks)):
            project(h, c, c % (FF_LOOKAHEAD + 1))
        for c in range(n_chunks):
            if c + FF_LOOKAHEAD < n_chunks:
                project(h, c + FF_LOOKAHEAD, (c + FF_LOOKAHEAD) % (FF_LOOKAHEAD + 1))
            slot = c % (FF_LOOKAHEAD + 1)
            act = (_gelu_tanh(conv(c, slot, 0)) * conv(c, slot, 1)).astype(BF16)
            part = _dot(act, w_down_ref[starts[c]:starts[c] + FF_CHUNKS[c], :])
            if c == 0:
                acc_ref[...] = part
            else:
                acc_ref[...] += part
        y = _rms_norm(acc_ref[...], g_post_ref[...])
        for j in range(D_MODEL // LANES):
            perm_ref[j, pl.ds(0, half, stride=2), :] = y[:half, j * LANES:(j + 1) * LANES]
            perm_ref[j, pl.ds(1, half, stride=2), :] = y[half:, j * LANES:(j + 1) * LANES]
        y_nat = jnp.concatenate([perm_ref[j] for j in range(D_MODEL // LANES)], axis=-1)
        out_ref[0, pl.ds(lo, t), :] = x + y_nat
        return carry

    lax.fori_loop(0, FFN_SUB_TILES, tile, 0)


def _conv_ffn(x, g_pre, w_up, conv_w, conv_b, w_down, g_post):
    b, s, _ = x.shape
    t = FFN_SUB_TILES * ROW_TILE
    tiles = s // t
    halo_blocks_per_tile = t // HALO
    n_halo_blocks = s // HALO
    row_map = lambda bi, i: (bi, i, 0)
    prev_map = lambda bi, i: (bi, jnp.maximum(i * halo_blocks_per_tile - 1, 0), 0)
    next_map = lambda bi, i: (bi, jnp.minimum((i + 1) * halo_blocks_per_tile, n_halo_blocks - 1), 0)
    return pl.pallas_call(
        _conv_ffn_kernel,
        grid=(b, tiles),
        in_specs=[
            pl.BlockSpec((1, t, D_MODEL), row_map),
            pl.BlockSpec((1, HALO, D_MODEL), prev_map),
            pl.BlockSpec((1, HALO, D_MODEL), next_map),
            _const_spec((1, D_MODEL)),
            _const_spec((D_MODEL, 2 * D_FF)),
            _const_spec((3, 2 * D_FF)),
            _const_spec((1, 2 * D_FF)),
            _const_spec((D_FF, D_MODEL)),
            _const_spec((1, D_MODEL)),
        ],
        out_specs=pl.BlockSpec((1, t, D_MODEL), row_map),
        out_shape=jax.ShapeDtypeStruct((b, s, D_MODEL), F32),
        scratch_shapes=[
            pltpu.VMEM((2 * (FF_LOOKAHEAD + 1), FF_CHUNK // LANES, ROW_TILE + 2 * HALO, LANES), F32),
            pltpu.VMEM((ROW_TILE, D_MODEL), F32),
            pltpu.VMEM((D_MODEL // LANES, ROW_TILE, LANES), F32),
        ],
        compiler_params=pltpu.CompilerParams(
            dimension_semantics=("parallel", "parallel"), vmem_limit_bytes=VMEM_LIMIT_BYTES),
        name="conv_ffn",
    )(x, x, x, g_pre, w_up, conv_w, conv_b, w_down, g_post)


def _rope_tables(seq_len):
    half = HEAD_DIM // 2
    inv = ROPE_THETA ** (-np.arange(half, dtype=np.float64) / half)
    ang = np.arange(seq_len, dtype=np.float64)[:, None] * inv[None, :]
    cos, sin = np.cos(ang), np.sin(ang)
    cos_head = np.concatenate([cos, cos], axis=-1)
    sin_head = np.concatenate([-sin, sin], axis=-1)
    reps = LANES // HEAD_DIM
    return (jnp.asarray(np.tile(cos_head, (1, reps)), dtype=F32),
            jnp.asarray(np.tile(sin_head, (1, reps)), dtype=F32))


def _band_bias(q_rows, stacked, keys, half_window, lead):
    i = (np.arange(stacked * q_rows) % q_rows)[None, :, None]
    j = np.arange(keys)[None, None, :]
    e = np.arange(3)[:, None, None]
    band = np.abs(j - lead - i) <= half_window
    inside = np.logical_and(np.logical_or(e != 0, j >= lead), np.logical_or(e != 2, j < lead + q_rows))
    return jnp.asarray(np.where(np.logical_and(band, inside), 0.0, NEG_INF), dtype=F32)


def _far_band_bias():
    off = np.arange(3)[:, None, None] * B_HALF_WINDOW
    i = (np.arange(2 * B_Q_BLOCK) % B_Q_BLOCK)[None, :, None]
    j = np.arange(B_KEY_WINDOW)[None, None, :]
    return jnp.asarray(np.where(np.abs(j - off - i) <= B_HALF_WINDOW, 0.0, NEG_INF), dtype=F32)


def kernel(x, norm_mix_pre, w_in, sink, w_branch_a, w_branch_b, w_gate, b_gate, w_out,
           norm_mix_post, norm_ffn_pre, w_up, conv_w, conv_b, w_down, norm_ffn_post):
    b, s, d = x.shape
    assert d == D_MODEL and s % B_SPAN == 0 and s % (SUB_TILES * ROW_TILE) == 0
    cos, sin_signed = _rope_tables(s)
    bias_a = _band_bias(A_Q_BLOCK, 1, A_KEY_WINDOW, A_HALF_WINDOW, A_Q_BLOCK)
    bias_b = _band_bias(B_Q_BLOCK, 2, B_KEY_WINDOW, B_HALF_WINDOW, B_HALF_WINDOW)
    bias_b_far = _far_band_bias()
    for layer in range(norm_mix_pre.shape[0]):
        proj = _in_proj(x, norm_mix_pre[layer][None], w_in[layer], cos, sin_signed)
        qa, kva, qkv_b0, qkv_b1, qkv_b2 = proj
        o2, l2, w_gate_h, w_a_h, w_b_h, w_out_h = _attn_b(
            bias_b_far, qkv_b2, B_PATTERNS[2][1],
            (w_gate[layer], w_branch_a[layer], w_branch_b[layer], w_out[layer]))
        ob2 = (o2, l2)
        x, w_up_h, w_down_h = _mix_attn(
                      x, sink[layer], bias_a, bias_b, qa, kva, qkv_b0, qkv_b1, ob2,
                      norm_mix_pre[layer][None],
                      w_gate_h, b_gate[layer][None], w_a_h, w_b_h, w_out_h, norm_mix_post[layer][None],
                      (w_up[layer], w_down[layer]))
        x = _conv_ffn(x, norm_ffn_pre[layer][None],
                      w_up_h, conv_w[layer], conv_b[layer][None], w_down_h,
                      norm_ffn_post[layer][None])
    return x
```

```python
import functools
import math

import jax
import jax.numpy as jnp
import numpy as np
from jax import lax
from jax.experimental import pallas as pl
from jax.experimental.pallas import tpu as pltpu

D_MODEL = 1024
HEAD_DIM = 64
A_Q_HEADS = 8
A_KV_HEADS = 2
A_GROUP = A_Q_HEADS // A_KV_HEADS
A_HALF_WINDOW = 128
B_PATTERNS = ((128, 1), (512, 4), (2048, 16))
B_N_GROUPS = len(B_PATTERNS)
B_HEADS = 4
HEAD_ORDER = (0, 2, 1, 3)
B_HALF_WINDOW = 64
ROPE_THETA = 10000.0
D_FF = 3 * D_MODEL
RMS_EPS = 1e-6
NEG_INF = -1e30
LOG2_E = math.log2(math.e)

A_Q_COLS = A_Q_HEADS * HEAD_DIM
A_KV_COLS = A_KV_HEADS * HEAD_DIM
A_COLS = A_Q_COLS + 2 * A_KV_COLS
B_GROUP_COLS = B_HEADS * HEAD_DIM
B_PROJ_COLS = B_N_GROUPS * B_GROUP_COLS
IN_COLS = A_COLS + 3 * B_PROJ_COLS
A_KVA_COLS = 4 * A_KV_COLS
B_QKV_COLS = 3 * B_GROUP_COLS

LANES = 128
VMEM_LIMIT_BYTES = 56 * 1024 * 1024

ROW_TILE = 512
SUB_TILES = 2
FFN_SUB_TILES = 2
A_Q_BLOCK = 128
A_KEY_WINDOW = 3 * A_Q_BLOCK
B_Q_BLOCK = 128
B_KEY_WINDOW = B_Q_BLOCK + 2 * B_HALF_WINDOW
B_SPAN = 2048
B_UNROLL = 8
FF_CHUNK = 1024
FF_CHUNKS = (FF_CHUNK,) * (D_FF // FF_CHUNK)
FF_LOOKAHEAD = 2
HALO = 8
DEINTERLEAVE_STRIDE = 4

BF16 = jnp.bfloat16
F32 = jnp.float32


def _dot(a, b):
    return jnp.dot(a, b, preferred_element_type=F32)


def _dot_nt(a, b):
    return lax.dot_general(a, b, (((1,), (1,)), ((), ())), preferred_element_type=F32)


def _rms_norm(x, gain):
    ms = jnp.mean(x * x, axis=-1, keepdims=True)
    return x * lax.rsqrt(ms + RMS_EPS) * gain


def _rope_chunk(p, cos, sin_signed, first_half):
    partner = jnp.where(first_half, pltpu.roll(p, 96, 1), pltpu.roll(p, 32, 1))
    return p * cos + partner * sin_signed


def _const_spec(shape):
    nd = len(shape)
    return pl.BlockSpec(shape, lambda *_: (0,) * nd, pipeline_mode=pl.Buffered(1))


def _row_chunk_specs(weights, steps_per_seq, n_steps):
    return [pl.BlockSpec((w.shape[0] // n_steps, w.shape[1]), lambda bi, i: (bi * steps_per_seq + i, 0))
            for w in weights]


def _cast_chunks(src_refs, dst_refs):
    for src_ref, dst_ref in zip(src_refs, dst_refs):
        dst_ref[...] = src_ref[...].astype(BF16)


def _in_proj_kernel(x_ref, gain_ref, w_f32_ref, cos_ref, sin_ref,
                    qa_ref, kva_ref, qkv0_ref, qkv1_ref, qkv2_ref, slab_ref, stage_ref, w_ref):
    t = ROW_TILE

    @pl.when(jnp.logical_and(pl.program_id(0) == 0, pl.program_id(1) == 0))
    def _():
        w_ref[...] = w_f32_ref[...].astype(BF16)

    lane = lax.broadcasted_iota(jnp.int32, (t, LANES), 1)
    first_half = (lane % HEAD_DIM) < (HEAD_DIM // 2)
    low_head = lane < HEAD_DIM
    scale = HEAD_DIM ** -0.5 * LOG2_E
    out_refs = (qkv0_ref, qkv1_ref, qkv2_ref)

    def tile(sub):
        rows = slice(sub * t, (sub + 1) * t)
        h = _rms_norm(x_ref[0, rows, :], gain_ref[...]).astype(BF16)
        cos = cos_ref[rows, :]
        sin_signed = sin_ref[rows, :]

        def proj(col0, ncols):
            return _dot(h, w_ref[:, col0:col0 + ncols])

        def rope(p, mult=None):
            chunks = []
            for c in range(p.shape[1] // LANES):
                r = _rope_chunk(p[:, c * LANES:(c + 1) * LANES], cos, sin_signed, first_half)
                chunks.append(r if mult is None else r * mult)
            return chunks

        def dup_heads(p):
            swapped = pltpu.roll(p, HEAD_DIM, 1)
            return [jnp.where(low_head, p, swapped), jnp.where(low_head, swapped, p)]

        def windowed():
            for c, chunk in enumerate(rope(proj(0, A_Q_COLS), scale)):
                qa_ref[0, rows, c * LANES:(c + 1) * LANES] = chunk.astype(BF16)
            kv = proj(A_Q_COLS, 2 * A_KV_COLS)
            ka = rope(kv[:, :A_KV_COLS])[0]
            for c, chunk in enumerate(dup_heads(ka) + dup_heads(kv[:, A_KV_COLS:])):
                kva_ref[0, rows, c * LANES:(c + 1) * LANES] = chunk.astype(BF16)

        def dilated(g):
            dil = B_PATTERNS[g][1]
            for kind in range(3):
                col0 = A_COLS + kind * B_PROJ_COLS + g * B_GROUP_COLS
                p = proj(col0, B_GROUP_COLS)
                if kind == 0:
                    chunks = rope(p, scale)
                elif kind == 1:
                    chunks = rope(p)
                else:
                    chunks = [p[:, c * LANES:(c + 1) * LANES] for c in range(B_GROUP_COLS // LANES)]
                o_ref = out_refs[g]
                sub_rows = slice(sub * t // dil, (sub + 1) * t // dil)
                col = lambda c: slice(kind * B_GROUP_COLS + c * LANES, kind * B_GROUP_COLS + (c + 1) * LANES)
                if dil == 1:
                    for c, chunk in enumerate(chunks):
                        o_ref[0, 0, sub_rows, col(c)] = chunk.astype(BF16)
                else:
                    for c, chunk in enumerate(chunks):
                        slab_ref[sub, kind, c] = chunk
                    for c in range(len(chunks)):
                        src = slab_ref.at[sub, kind, c]
                        if dil > DEINTERLEAVE_STRIDE:
                            assert dil == DEINTERLEAVE_STRIDE ** 2
                            for a in range(DEINTERLEAVE_STRIDE):
                                stage_ref[sub, kind, c, a] = src[pl.ds(a, t // DEINTERLEAVE_STRIDE,
                                                                       stride=DEINTERLEAVE_STRIDE), :]
                        for res in range(dil):
                            if dil > DEINTERLEAVE_STRIDE:
                                picked = stage_ref[sub, kind, c, res % DEINTERLEAVE_STRIDE,
                                                   pl.ds(res // DEINTERLEAVE_STRIDE, t // dil,
                                                         stride=DEINTERLEAVE_STRIDE), :]
                            else:
                                picked = src[pl.ds(res, t // dil, stride=dil), :]
                            o_ref[0, res, sub_rows, col(c)] = picked.astype(BF16)

        dilated(2)
        dilated(1)
        windowed()
        dilated(0)

    for sub in range(SUB_TILES):
        tile(sub)


def _in_proj(x, gain, w_in, cos, sin_signed):
    b, s, _ = x.shape
    t = SUB_TILES * ROW_TILE
    grid = (b, s // t)
    row_map = lambda bi, i: (bi, i, 0)
    out_shape = [
        jax.ShapeDtypeStruct((b, s, A_Q_COLS), BF16),
        jax.ShapeDtypeStruct((b, s, A_KVA_COLS), BF16),
    ]
    out_specs = [
        pl.BlockSpec((1, t, A_Q_COLS), row_map),
        pl.BlockSpec((1, t, A_KVA_COLS), row_map),
    ]
    for _, dil in B_PATTERNS:
        out_shape.append(jax.ShapeDtypeStruct((b, dil, s // dil, B_QKV_COLS), BF16))
        out_specs.append(pl.BlockSpec((1, dil, t // dil, B_QKV_COLS), lambda bi, i: (bi, 0, i, 0)))
    return pl.pallas_call(
        _in_proj_kernel,
        grid=grid,
        in_specs=[
            pl.BlockSpec((1, t, D_MODEL), row_map),
            _const_spec((1, D_MODEL)),
            _const_spec((D_MODEL, IN_COLS)),
            pl.BlockSpec((t, LANES), lambda bi, i: (i, 0)),
            pl.BlockSpec((t, LANES), lambda bi, i: (i, 0)),
        ],
        out_specs=out_specs,
        out_shape=out_shape,
        scratch_shapes=[
            pltpu.VMEM((SUB_TILES, 3, B_GROUP_COLS // LANES, ROW_TILE, LANES), F32),
            pltpu.VMEM((SUB_TILES, 3, B_GROUP_COLS // LANES, DEINTERLEAVE_STRIDE,
                        ROW_TILE // DEINTERLEAVE_STRIDE, LANES), F32),
            pltpu.VMEM((D_MODEL, IN_COLS), BF16),
        ],
        compiler_params=pltpu.CompilerParams(
            dimension_semantics=("arbitrary", "arbitrary"), vmem_limit_bytes=VMEM_LIMIT_BYTES),
        name="in_proj",
    )(x, gain, w_in, cos, sin_signed)


def _attn_b_kernel(bias_ref, q_ref, kv_ref, *rest, dil, n_cast):
    o_ref, lse_ref = rest[n_cast:n_cast + 2]
    _cast_chunks(rest[:n_cast], rest[n_cast + 2:2 * n_cast + 2])
    two_pass = dil > DEINTERLEAVE_STRIDE
    if two_pass:
        assert dil == DEINTERLEAVE_STRIDE ** 2
        stage_ref = rest[2 * n_cast + 2]
    sub_len = kv_ref.shape[2]
    sub_rows = q_ref.shape[2]
    blocks = sub_rows // B_Q_BLOCK
    step = pl.program_id(1)
    low_head = lax.broadcasted_iota(jnp.int32, (B_Q_BLOCK, LANES), 1) < HEAD_DIM
    low_key = lax.broadcasted_iota(jnp.int32, (B_KEY_WINDOW, LANES), 1) < HEAD_DIM

    def units(it, carry):
        work = []
        for uu in range(B_UNROLL):
            u = it * B_UNROLL + uu
            res = u // blocks
            r0 = (u % blocks) * B_Q_BLOCK
            q_pos = (step * sub_rows + r0) // B_HALF_WINDOW
            start_pos = jnp.clip(q_pos - 1, 0, (sub_len - B_KEY_WINDOW) // B_HALF_WINDOW)
            start = start_pos * B_HALF_WINDOW
            bias = bias_ref[q_pos - start_pos]
            q_blk = q_ref[0, res, pl.ds(r0, B_Q_BLOCK), 0:B_GROUP_COLS]
            k_win = kv_ref[0, res, pl.ds(start, B_KEY_WINDOW), B_GROUP_COLS:2 * B_GROUP_COLS]
            v_win = kv_ref[0, res, pl.ds(start, B_KEY_WINDOW), 2 * B_GROUP_COLS:3 * B_GROUP_COLS]
            if dil == 1:
                dst = (None, pl.ds(r0, B_Q_BLOCK))
            elif two_pass:
                dst = (res % DEINTERLEAVE_STRIDE,
                       pl.ds(r0 * DEINTERLEAVE_STRIDE + res // DEINTERLEAVE_STRIDE, B_Q_BLOCK,
                             stride=DEINTERLEAVE_STRIDE))
            else:
                dst = (None, pl.ds(r0 * dil + res, B_Q_BLOCK, stride=dil))
            rows = dst
            for pr in range(B_HEADS // 2):
                qp = q_blk[:, pr * LANES:(pr + 1) * LANES]
                zero = jnp.zeros_like(qp)
                lhs = jnp.concatenate([jnp.where(low_head, qp, zero), jnp.where(low_head, zero, qp)], axis=0)
                sc = _dot_nt(lhs, k_win[:, pr * LANES:(pr + 1) * LANES])
                work.append((rows, pr, bias, v_win[:, pr * LANES:(pr + 1) * LANES], sc))

        soft = []
        for rows, pr, bias, vp, sc in work:
            sc = sc + bias
            m = jnp.max(sc, axis=-1, keepdims=True)
            soft.append((rows, pr, vp, m, jnp.exp2(sc - m).astype(BF16)))

        for rows, pr, vp, m, p in soft:
            one = jnp.ones_like(vp)
            top = _dot(p[:B_Q_BLOCK], jnp.where(low_key, vp, one))
            bot = _dot(p[B_Q_BLOCK:], jnp.where(low_key, one, vp))
            denom = pltpu.roll(jnp.where(low_head, bot, top), HEAD_DIM, 1)
            o_val = jnp.where(low_head, top, bot) / denom
            lse_val = jnp.where(low_head, m[:B_Q_BLOCK], m[B_Q_BLOCK:]) + jnp.log2(denom)
            cls, where = rows
            if two_pass:
                stage_ref[0, pr, cls, where, :] = o_val
                stage_ref[1, pr, cls, where, :] = lse_val
            else:
                o_ref[0, pr, where, :] = o_val
                lse_ref[0, pr, where, :] = lse_val
        return carry

    lax.fori_loop(0, dil * blocks // B_UNROLL, units, 0)
    if two_pass:
        for which, out in enumerate((o_ref, lse_ref)):
            for pr in range(B_HEADS // 2):
                for a in range(DEINTERLEAVE_STRIDE):
                    out[0, pr, pl.ds(a, DEINTERLEAVE_STRIDE * sub_rows, stride=DEINTERLEAVE_STRIDE), :] = (
                        stage_ref[which, pr, a])


def _attn_b(bias, qkv, dil, cast_weights=()):
    b, _, sub_len, _ = qkv.shape
    s = sub_len * dil
    sub_rows = B_SPAN // dil
    nslab = B_GROUP_COLS // LANES
    out_sds = jax.ShapeDtypeStruct((b, nslab, s, LANES), F32)
    out_spec = pl.BlockSpec((1, nslab, B_SPAN, LANES), lambda bi, i: (bi, 0, i, 0))
    steps_per_seq = s // B_SPAN
    chunk_specs = _row_chunk_specs(cast_weights, steps_per_seq, b * steps_per_seq)
    return pl.pallas_call(
        functools.partial(_attn_b_kernel, dil=dil, n_cast=len(cast_weights)),
        grid=(b, steps_per_seq),
        in_specs=[
            _const_spec(bias.shape),
            pl.BlockSpec((1, dil, sub_rows, B_QKV_COLS), lambda bi, i: (bi, 0, i, 0)),
            pl.BlockSpec((1, dil, sub_len, B_QKV_COLS), lambda bi, i: (bi, 0, 0, 0)),
        ] + chunk_specs,
        out_specs=[out_spec, out_spec] + chunk_specs,
        out_shape=[out_sds, out_sds] + [jax.ShapeDtypeStruct(w.shape, BF16) for w in cast_weights],
        scratch_shapes=([pltpu.VMEM((2, nslab, DEINTERLEAVE_STRIDE, DEINTERLEAVE_STRIDE * sub_rows, LANES), F32)]
                        if dil > DEINTERLEAVE_STRIDE else []),
        compiler_params=pltpu.CompilerParams(
            dimension_semantics=("parallel", "parallel"), vmem_limit_bytes=VMEM_LIMIT_BYTES),
        name=f"attn_b_d{dil}",
    )(bias, qkv, qkv, *cast_weights)


def _window(main_ref, prev_ref, next_ref, lead, lo, hi, length, cols):
    parts = []
    if lo < 0:
        halo = prev_ref.shape[-2]
        parts.append(prev_ref[lead + (slice(halo + lo, halo), cols)])
        lo = 0
    parts.append(main_ref[lead + (slice(lo, min(hi, length)), cols)])
    if hi > length:
        parts.append(next_ref[lead + (slice(0, hi - length), cols)])
    return parts[0] if len(parts) == 1 else jnp.concatenate(parts, axis=0)


def _mix_attn_kernel(sink_ref, bias_a_ref, bias_b_ref, x_ref,
                     qa_ref, kva_ref, kva_p_ref, kva_n_ref,
                     b0_ref, b0_p_ref, b0_n_ref, b1_ref, b1_p_ref, b1_n_ref,
                     o2_ref, l2_ref,
                     g_pre_ref, w_gate_ref, b_gate_ref, w_a_ref, w_b_ref, w_out_ref, g_post_ref,
                     *rest, n_cast):
    out_ref = rest[n_cast]
    ya_ref, ob_ref = rest[2 * n_cast + 1:]
    _cast_chunks(rest[:n_cast], rest[n_cast + 1:2 * n_cast + 1])
    t = ROW_TILE
    step_rows = t
    i = pl.program_id(1)
    last = pl.num_programs(1) - 1
    low_head = lax.broadcasted_iota(jnp.int32, (A_Q_BLOCK, LANES), 1) < HEAD_DIM
    dil1 = B_PATTERNS[1][1]
    sub_rows = step_rows // dil1

    def edge_index(first, final):
        idx = 1
        if first:
            idx = jnp.where(i == 0, 0, idx)
        if final:
            idx = jnp.where(i == last, 2, idx)
        return idx

    def scores(grp_idx, part):
        work_a, work_b = [], []
        n = grp_idx
        lo, hi = (n - 1) * A_Q_BLOCK, (n + 2) * A_Q_BLOCK
        bias = bias_a_ref[edge_index(n == 0, hi > step_rows)]
        q_blk = qa_ref[0, n * A_Q_BLOCK:(n + 1) * A_Q_BLOCK, :]
        for j in range(A_KV_HEADS if part == 'a' else 0):
            kcols = slice(j * LANES, (j + 1) * LANES)
            vcols = slice(2 * A_KV_COLS + j * LANES, 2 * A_KV_COLS + (j + 1) * LANES)
            kd = _window(kva_ref, kva_p_ref, kva_n_ref, (0,), lo, hi, step_rows, kcols)
            vd = _window(kva_ref, kva_p_ref, kva_n_ref, (0,), lo, hi, step_rows, vcols)
            lhs = []
            for g in HEAD_ORDER:
                c0 = j * A_GROUP * HEAD_DIM + (g // 2) * LANES
                pair = q_blk[:, c0:c0 + LANES]
                keep = low_head if g % 2 == 0 else jnp.logical_not(low_head)
                lhs.append(jnp.where(keep, pair, jnp.zeros_like(pair)))
            sc = _dot_nt(jnp.concatenate(lhs, axis=0), kd)
            work_a.append((n, j, bias, vd, sc))
        units = [(0, (0, 0), grp_idx, step_rows), (1, (0, grp_idx), 0, sub_rows)]
        for grp, lead, n, length in (units if part == 'b' else []):
            m_ref, p_ref, n_ref = (b0_ref, b0_p_ref, b0_n_ref) if grp == 0 else (b1_ref, b1_p_ref, b1_n_ref)
            lo, hi = n * B_Q_BLOCK - B_HALF_WINDOW, (n + 1) * B_Q_BLOCK + B_HALF_WINDOW
            bias = bias_b_ref[edge_index(lo < 0, hi > length)]
            if grp == 0:
                rows = pl.ds(n * B_Q_BLOCK, B_Q_BLOCK)
            else:
                rows = pl.ds(n * B_Q_BLOCK * dil1 + lead[1], B_Q_BLOCK, stride=dil1)
            for pr in range(B_HEADS // 2):
                cols = [slice(kind * B_GROUP_COLS + pr * LANES, kind * B_GROUP_COLS + (pr + 1) * LANES)
                        for kind in range(3)]
                qp = m_ref[lead + (slice(n * B_Q_BLOCK, (n + 1) * B_Q_BLOCK), cols[0])]
                kp = _window(m_ref, p_ref, n_ref, lead, lo, hi, length, cols[1])
                vp = _window(m_ref, p_ref, n_ref, lead, lo, hi, length, cols[2])
                zero = jnp.zeros_like(qp)
                lhs = jnp.concatenate([jnp.where(low_head, qp, zero), jnp.where(low_head, zero, qp)], axis=0)
                work_b.append((grp, rows, pr, bias, vp, _dot_nt(lhs, kp)))
        return work_a, work_b

    def softmaxes(work):
        work_a, work_b = work
        soft_a, soft_b = [], []
        for n, j, bias, vd, sc in work_a:
            probs, denoms = {}, {}
            for slot, g in enumerate(HEAD_ORDER):
                sg = sc[slot * A_Q_BLOCK:(slot + 1) * A_Q_BLOCK]
                sg = jnp.concatenate([sg[:, :A_Q_BLOCK] + bias[:, :A_Q_BLOCK],
                                      sg[:, A_Q_BLOCK:2 * A_Q_BLOCK],
                                      sg[:, 2 * A_Q_BLOCK:] + bias[:, 2 * A_Q_BLOCK:]], axis=-1)
                sink = sink_ref[j * A_GROUP + g] * LOG2_E
                m = jnp.maximum(jnp.max(sg, axis=-1, keepdims=True), sink)
                p = jnp.exp2(sg - m)
                probs[g] = p.astype(BF16)
                denoms[g] = jnp.sum(p, axis=-1, keepdims=True) + jnp.exp2(sink - m)
            soft_a.append((n, j, vd, probs, denoms))
        for grp, rows, pr, bias, vp, sc in work_b:
            sc = sc + bias
            m = jnp.max(sc, axis=-1, keepdims=True)
            p = jnp.exp2(sc - m)
            soft_b.append((grp, rows, pr, vp, (m, jnp.sum(p, axis=-1, keepdims=True)), p.astype(BF16)))
        return soft_a, soft_b

    def values(soft):
        soft_a, soft_b = soft
        pairs = range(A_GROUP // 2)
        for n, j, vd, probs, denoms in soft_a:
            r = _dot(jnp.concatenate([probs[g] for g in range(A_GROUP)], axis=0), vd)
            for pr in pairs:
                even = r[(2 * pr) * A_Q_BLOCK:(2 * pr + 1) * A_Q_BLOCK] / denoms[2 * pr]
                odd = r[(2 * pr + 1) * A_Q_BLOCK:(2 * pr + 2) * A_Q_BLOCK] / denoms[2 * pr + 1]
                c0 = j * A_GROUP * HEAD_DIM + pr * LANES
                ya_ref[n * A_Q_BLOCK:(n + 1) * A_Q_BLOCK, c0:c0 + LANES] = (
                    jnp.where(low_head, even, odd)).astype(BF16)
        for grp, rows, pr, vp, (m, l), p in soft_b:
            o = _dot(p, vp) / l
            lse = m + jnp.log2(l)
            ob_ref[grp, 0, pr, rows, :] = jnp.where(low_head, o[:B_Q_BLOCK], o[B_Q_BLOCK:])
            ob_ref[grp, 1, pr, rows, :] = jnp.where(low_head, lse[:B_Q_BLOCK], lse[B_Q_BLOCK:])

    n_groups = t // A_Q_BLOCK
    assert step_rows == t and n_groups == dil1 == t // B_Q_BLOCK
    halves = [(g, part) for g in range(n_groups) for part in ('b', 'a')]
    gate_chunk = 2 * D_MODEL // len(halves)
    x = x_ref[0]
    h = _rms_norm(x, g_pre_ref[...]).astype(BF16)

    def gate_logits(j):
        cols = slice(j * gate_chunk, (j + 1) * gate_chunk)
        return _dot(h, w_gate_ref[:, cols]) + b_gate_ref[:, cols]

    z = [None] * len(halves)
    work = scores(*halves[0])
    z[0] = gate_logits(0)
    for k in range(len(halves)):
        soft = softmaxes(work)
        if k + 1 < len(halves):
            work = scores(*halves[k + 1])
        values(soft)
        if k + 1 < len(halves):
            z[k + 1] = gate_logits(k + 1)

    yb = []
    for c in range(B_GROUP_COLS // LANES):
        l0, l1, l2 = ob_ref[0, 1, c], ob_ref[1, 1, c], l2_ref[0, c]
        m = jnp.maximum(jnp.maximum(l0, l1), l2)
        e0, e1, e2 = jnp.exp2(l0 - m), jnp.exp2(l1 - m), jnp.exp2(l2 - m)
        num = e0 * ob_ref[0, 0, c] + e1 * ob_ref[1, 0, c] + e2 * o2_ref[0, c]
        yb.append((num / (e0 + e1 + e2)).astype(BF16))
    yb = jnp.concatenate(yb, axis=-1)
    gates = jax.nn.sigmoid(jnp.concatenate(z, axis=-1))
    merged = (gates[:, :D_MODEL] * _dot(ya_ref[...], w_a_ref[...])
              + gates[:, D_MODEL:] * _dot(yb, w_b_ref[...]))
    mix = _dot(merged.astype(BF16), w_out_ref[...])
    out_ref[0] = x + _rms_norm(mix, g_post_ref[...])


def _halo_specs(lead_blocks, rows, halo, total_rows, cols):
    nlead = len(lead_blocks)
    per = rows // halo
    nhalo = total_rows // halo
    zeros = (0,) * nlead
    main = pl.BlockSpec((1,) + lead_blocks + (rows, cols), lambda bi, i: (bi,) + zeros + (i, 0))
    prev = pl.BlockSpec((1,) + lead_blocks + (halo, cols),
                        lambda bi, i: (bi,) + zeros + (jnp.maximum(i * per - 1, 0), 0))
    nxt = pl.BlockSpec((1,) + lead_blocks + (halo, cols),
                       lambda bi, i: (bi,) + zeros + (jnp.minimum((i + 1) * per, nhalo - 1), 0))
    return [main, prev, nxt]


def _mix_attn(x, sink, bias_a, bias_b, qa, kva, qkv_b0, qkv_b1, ob2,
              g_pre, w_gate, b_gate, w_a, w_b, w_out, g_post, cast_weights=()):
    b, s, _ = x.shape
    t = ROW_TILE
    dil1 = B_PATTERNS[1][1]
    nslab = B_GROUP_COLS // LANES
    row_map = lambda bi, i: (bi, i, 0)
    slab_spec = pl.BlockSpec((1, nslab, t, LANES), lambda bi, i: (bi, 0, i, 0))
    chunk_specs = _row_chunk_specs(cast_weights, s // t, b * s // t)
    in_specs = ([pl.BlockSpec(memory_space=pltpu.SMEM), _const_spec(bias_a.shape), _const_spec(bias_b.shape),
                 pl.BlockSpec((1, t, D_MODEL), row_map),
                 pl.BlockSpec((1, t, A_Q_COLS), row_map)]
                + _halo_specs((), t, A_Q_BLOCK, s, A_KVA_COLS)
                + _halo_specs((1,), t, B_HALF_WINDOW, s, B_QKV_COLS)
                + _halo_specs((dil1,), t // dil1, B_HALF_WINDOW, s // dil1, B_QKV_COLS)
                + [slab_spec, slab_spec,
                   _const_spec((1, D_MODEL)),
                   _const_spec((D_MODEL, 2 * D_MODEL)),
                   _const_spec((1, 2 * D_MODEL)),
                   _const_spec((A_Q_COLS, D_MODEL)),
                   _const_spec((B_GROUP_COLS, D_MODEL)),
                   _const_spec((D_MODEL, D_MODEL)),
                   _const_spec((1, D_MODEL))])
    return pl.pallas_call(
        functools.partial(_mix_attn_kernel, n_cast=len(cast_weights)),
        grid=(b, s // t),
        in_specs=in_specs + chunk_specs,
        out_specs=[pl.BlockSpec((1, t, D_MODEL), row_map)] + chunk_specs,
        out_shape=[jax.ShapeDtypeStruct((b, s, D_MODEL), F32)]
        + [jax.ShapeDtypeStruct(w.shape, BF16) for w in cast_weights],
        scratch_shapes=[
            pltpu.VMEM((t, A_Q_COLS), BF16),
            pltpu.VMEM((2, 2, nslab, t, LANES), F32),
        ],
        compiler_params=pltpu.CompilerParams(
            dimension_semantics=("parallel", "parallel"), vmem_limit_bytes=VMEM_LIMIT_BYTES),
        name="mix_attn",
    )(sink, bias_a, bias_b, x, qa, kva, kva, kva, qkv_b0, qkv_b0, qkv_b0, qkv_b1, qkv_b1, qkv_b1,
      ob2[0], ob2[1], g_pre, w_gate, b_gate, w_a, w_b, w_out, g_post, *cast_weights)


def _gelu_tanh(x):
    c = -2.0 * math.sqrt(2.0 / math.pi) * LOG2_E
    return x / (1.0 + jnp.exp2(x * (c + (c * 0.044715) * (x * x))))


def _conv_ffn_kernel(x_ref, prev_ref, next_ref, g_pre_ref, w_up_ref, conv_w_ref, conv_b_ref,
                     w_down_ref, g_post_ref, out_ref, u_ref, acc_ref, perm_ref):
    t = ROW_TILE
    half = t // 2
    i = pl.program_id(1)
    last = pl.num_programs(1) - 1
    row = lax.broadcasted_iota(jnp.int32, (t + 2 * HALO, 1), 0)
    starts = [sum(FF_CHUNKS[:c]) for c in range(len(FF_CHUNKS))]

    def col_starts(c):
        return (starts[c], D_FF + starts[c])

    def project(h, c, slot):
        for part, col0 in enumerate(col_starts(c)):
            up = _dot(h, w_up_ref[:, col0:col0 + FF_CHUNKS[c]])
            for j in range(FF_CHUNKS[c] // LANES):
                u_ref[2 * slot + part, j] = up[:, j * LANES:(j + 1) * LANES]

    def conv(c, slot, part):
        buf, col0 = 2 * slot + part, col_starts(c)[part]
        pieces = []
        for j in range(FF_CHUNKS[c] // LANES):
            cols = slice(col0 + j * LANES, col0 + (j + 1) * LANES)
            w0, w1, w2 = conv_w_ref[0:1, cols], conv_w_ref[1:2, cols], conv_w_ref[2:3, cols]
            bias = conv_b_ref[:, cols]
            r = [u_ref[buf, j, pl.ds(HALO - 1 + k, half, stride=2), :] for k in range(4)]
            even = r[0] * w0 + r[1] * w1 + r[2] * w2 + bias
            odd = r[1] * w0 + r[2] * w1 + r[3] * w2 + bias
            pieces.append(jnp.concatenate([even, odd], axis=0))
        return jnp.concatenate(pieces, axis=-1)

    def tile(sub, carry):
        lo = pl.multiple_of(sub * t, t)
        x = x_ref[0, pl.ds(lo, t), :]
        first, final = sub == 0, sub == FFN_SUB_TILES - 1
        inner_before = x_ref[0, pl.ds(pl.multiple_of(jnp.maximum(lo - HALO, 0), HALO), HALO), :]
        inner_after = x_ref[0, pl.ds(pl.multiple_of(jnp.minimum(lo + t, (FFN_SUB_TILES - 1) * t), HALO), HALO), :]
        before = jnp.where(first, prev_ref[0], inner_before)
        after = jnp.where(final, next_ref[0], inner_after)
        hn = _rms_norm(jnp.concatenate([before, x, after], axis=0), g_pre_ref[...])
        hn = jnp.where(jnp.logical_or(row >= HALO, jnp.logical_or(i > 0, sub > 0)), hn, 0.0)
        hn = jnp.where(jnp.logical_or(row < HALO + t, jnp.logical_or(i < last, sub < FFN_SUB_TILES - 1)), hn, 0.0)
        h = hn.astype(BF16)

        n_chunks = len(FF_CHUNKS)
        for c in range(min(FF_LOOKAHEAD, n_chunks)):
            project(h, c, c % (FF_LOOKAHEAD + 1))
        for c in range(n_chunks):
            if c + FF_LOOKAHEAD < n_chunks:
                project(h, c + FF_LOOKAHEAD, (c + FF_LOOKAHEAD) % (FF_LOOKAHEAD + 1))
            slot = c % (FF_LOOKAHEAD + 1)
            act = (_gelu_tanh(conv(c, slot, 0)) * conv(c, slot, 1)).astype(BF16)
            part = _dot(act, w_down_ref[starts[c]:starts[c] + FF_CHUNKS[c], :])
            if c == 0:
                acc_ref[...] = part
            else:
                acc_ref[...] += part
        y = _rms_norm(acc_ref[...], g_post_ref[...])
        for j in range(D_MODEL // LANES):
            perm_ref[j, pl.ds(0, half, stride=2), :] = y[:half, j * LANES:(j + 1) * LANES]
            perm_ref[j, pl.ds(1, half, stride=2), :] = y[half:, j * LANES:(j + 1) * LANES]
        y_nat = jnp.concatenate([perm_ref[j] for j in range(D_MODEL // LANES)], axis=-1)
        out_ref[0, pl.ds(lo, t), :] = x + y_nat
        return carry

    lax.fori_loop(0, FFN_SUB_TILES, tile, 0)


def _conv_ffn(x, g_pre, w_up, conv_w, conv_b, w_down, g_post):
    b, s, _ = x.shape
    t = FFN_SUB_TILES * ROW_TILE
    tiles = s // t
    halo_blocks_per_tile = t // HALO
    n_halo_blocks = s // HALO
    row_map = lambda bi, i: (bi, i, 0)
    prev_map = lambda bi, i: (bi, jnp.maximum(i * halo_blocks_per_tile - 1, 0), 0)
    next_map = lambda bi, i: (bi, jnp.minimum((i + 1) * halo_blocks_per_tile, n_halo_blocks - 1), 0)
    return pl.pallas_call(
        _conv_ffn_kernel,
        grid=(b, tiles),
        in_specs=[
            pl.BlockSpec((1, t, D_MODEL), row_map),
            pl.BlockSpec((1, HALO, D_MODEL), prev_map),
            pl.BlockSpec((1, HALO, D_MODEL), next_map),
            _const_spec((1, D_MODEL)),
            _const_spec((D_MODEL, 2 * D_FF)),
            _const_spec((3, 2 * D_FF)),
            _const_spec((1, 2 * D_FF)),
            _const_spec((D_FF, D_MODEL)),
            _const_spec((1, D_MODEL)),
        ],
        out_specs=pl.BlockSpec((1, t, D_MODEL), row_map),
        out_shape=jax.ShapeDtypeStruct((b, s, D_MODEL), F32),
        scratch_shapes=[
            pltpu.VMEM((2 * (FF_LOOKAHEAD + 1), FF_CHUNK // LANES, ROW_TILE + 2 * HALO, LANES), F32),
            pltpu.VMEM((ROW_TILE, D_MODEL), F32),
            pltpu.VMEM((D_MODEL // LANES, ROW_TILE, LANES), F32),
        ],
        compiler_params=pltpu.CompilerParams(
            dimension_semantics=("parallel", "parallel"), vmem_limit_bytes=VMEM_LIMIT_BYTES),
        name="conv_ffn",
    )(x, x, x, g_pre, w_up, conv_w, conv_b, w_down, g_post)


def _rope_tables(seq_len):
    half = HEAD_DIM // 2
    inv = ROPE_THETA ** (-np.arange(half, dtype=np.float64) / half)
    ang = np.arange(seq_len, dtype=np.float64)[:, None] * inv[None, :]
    cos, sin = np.cos(ang), np.sin(ang)
    cos_head = np.concatenate([cos, cos], axis=-1)
    sin_head = np.concatenate([-sin, sin], axis=-1)
    reps = LANES // HEAD_DIM
    return (jnp.asarray(np.tile(cos_head, (1, reps)), dtype=F32),
            jnp.asarray(np.tile(sin_head, (1, reps)), dtype=F32))


def _band_bias(q_rows, stacked, keys, half_window, lead):
    i = (np.arange(stacked * q_rows) % q_rows)[None, :, None]
    j = np.arange(keys)[None, None, :]
    e = np.arange(3)[:, None, None]
    band = np.abs(j - lead - i) <= half_window
    inside = np.logical_and(np.logical_or(e != 0, j >= lead), np.logical_or(e != 2, j < lead + q_rows))
    return jnp.asarray(np.where(np.logical_and(band, inside), 0.0, NEG_INF), dtype=F32)


def _far_band_bias():
    off = np.arange(3)[:, None, None] * B_HALF_WINDOW
    i = (np.arange(2 * B_Q_BLOCK) % B_Q_BLOCK)[None, :, None]
    j = np.arange(B_KEY_WINDOW)[None, None, :]
    return jnp.asarray(np.where(np.abs(j - off - i) <= B_HALF_WINDOW, 0.0, NEG_INF), dtype=F32)


def kernel(x, norm_mix_pre, w_in, sink, w_branch_a, w_branch_b, w_gate, b_gate, w_out,
           norm_mix_post, norm_ffn_pre, w_up, conv_w, conv_b, w_down, norm_ffn_post):
    b, s, d = x.shape
    assert d == D_MODEL and s % B_SPAN == 0 and s % (SUB_TILES * ROW_TILE) == 0
    cos, sin_signed = _rope_tables(s)
    bias_a = _band_bias(A_Q_BLOCK, 1, A_KEY_WINDOW, A_HALF_WINDOW, A_Q_BLOCK)
    bias_b = _band_bias(B_Q_BLOCK, 2, B_KEY_WINDOW, B_HALF_WINDOW, B_HALF_WINDOW)
    bias_b_far = _far_band_bias()
    for layer in range(norm_mix_pre.shape[0]):
        proj = _in_proj(x, norm_mix_pre[layer][None], w_in[layer], cos, sin_signed)
        qa, kva, qkv_b0, qkv_b1, qkv_b2 = proj
        o2, l2, w_gate_h, w_a_h, w_b_h, w_out_h = _attn_b(
            bias_b_far, qkv_b2, B_PATTERNS[2][1],
            (w_gate[layer], w_branch_a[layer], w_branch_b[layer], w_out[layer]))
        ob2 = (o2, l2)
        x, w_up_h, w_down_h = _mix_attn(
                      x, sink[layer], bias_a, bias_b, qa, kva, qkv_b0, qkv_b1, ob2,
                      norm_mix_pre[layer][None],
                      w_gate_h, b_gate[layer][None], w_a_h, w_b_h, w_out_h, norm_mix_post[layer][None],
                      (w_up[layer], w_down[layer]))
        x = _conv_ffn(x, norm_ffn_pre[layer][None],
                      w_up_h, conv_w[layer], conv_b[layer][None], w_down_h,
                      norm_ffn_post[layer][None])
    return x
```

```python
import functools
import math

import jax
import jax.numpy as jnp
import numpy as np
from jax import lax
from jax.experimental import pallas as pl
from jax.experimental.pallas import tpu as pltpu

D_MODEL = 1024
HEAD_DIM = 64
A_Q_HEADS = 8
A_KV_HEADS = 2
A_GROUP = A_Q_HEADS // A_KV_HEADS
A_HALF_WINDOW = 128
B_PATTERNS = ((128, 1), (512, 4), (2048, 16))
B_N_GROUPS = len(B_PATTERNS)
B_HEADS = 4
HEAD_ORDER = (0, 2, 1, 3)
B_HALF_WINDOW = 64
ROPE_THETA = 10000.0
D_FF = 3 * D_MODEL
RMS_EPS = 1e-6
NEG_INF = -1e30
LOG2_E = math.log2(math.e)

A_Q_COLS = A_Q_HEADS * HEAD_DIM
A_KV_COLS = A_KV_HEADS * HEAD_DIM
A_COLS = A_Q_COLS + 2 * A_KV_COLS
B_GROUP_COLS = B_HEADS * HEAD_DIM
B_PROJ_COLS = B_N_GROUPS * B_GROUP_COLS
IN_COLS = A_COLS + 3 * B_PROJ_COLS
A_KVA_COLS = 4 * A_KV_COLS
B_QKV_COLS = 3 * B_GROUP_COLS

LANES = 128
VMEM_LIMIT_BYTES = 56 * 1024 * 1024

ROW_TILE = 512
SUB_TILES = 2
FFN_SUB_TILES = 2
A_Q_BLOCK = 128
A_KEY_WINDOW = 3 * A_Q_BLOCK
B_Q_BLOCK = 128
B_KEY_WINDOW = B_Q_BLOCK + 2 * B_HALF_WINDOW
B_SPAN = 2048
B_UNROLL = 8
FF_CHUNK = 1024
FF_CHUNKS = (FF_CHUNK,) * (D_FF // FF_CHUNK)
FF_LOOKAHEAD = 2
HALO = 8
DEINTERLEAVE_STRIDE = 4

BF16 = jnp.bfloat16
F32 = jnp.float32


def _dot(a, b):
    return jnp.dot(a, b, preferred_element_type=F32)


def _dot_nt(a, b):
    return lax.dot_general(a, b, (((1,), (1,)), ((), ())), preferred_element_type=F32)


def _rms_norm(x, gain):
    ms = jnp.mean(x * x, axis=-1, keepdims=True)
    return x * lax.rsqrt(ms + RMS_EPS) * gain


def _rope_chunk(p, cos, sin_signed, first_half):
    partner = jnp.where(first_half, pltpu.roll(p, 96, 1), pltpu.roll(p, 32, 1))
    return p * cos + partner * sin_signed


def _const_spec(shape):
    nd = len(shape)
    return pl.BlockSpec(shape, lambda *_: (0,) * nd, pipeline_mode=pl.Buffered(1))


def _row_chunk_specs(weights, steps_per_seq, n_steps):
    return [pl.BlockSpec((w.shape[0] // n_steps, w.shape[1]), lambda bi, i: (bi * steps_per_seq + i, 0))
            for w in weights]


def _cast_chunks(src_refs, dst_refs):
    for src_ref, dst_ref in zip(src_refs, dst_refs):
        dst_ref[...] = src_ref[...].astype(BF16)


def _in_proj_kernel(x_ref, gain_ref, w_f32_ref, cos_ref, sin_ref,
                    qa_ref, kva_ref, qkv0_ref, qkv1_ref, qkv2_ref, slab_ref, stage_ref, w_ref):
    t = ROW_TILE

    @pl.when(jnp.logical_and(pl.program_id(0) == 0, pl.program_id(1) == 0))
    def _():
        w_ref[...] = w_f32_ref[...].astype(BF16)

    lane = lax.broadcasted_iota(jnp.int32, (t, LANES), 1)
    first_half = (lane % HEAD_DIM) < (HEAD_DIM // 2)
    low_head = lane < HEAD_DIM
    scale = HEAD_DIM ** -0.5 * LOG2_E
    out_refs = (qkv0_ref, qkv1_ref, qkv2_ref)

    def tile(sub):
        rows = slice(sub * t, (sub + 1) * t)
        h = _rms_norm(x_ref[0, rows, :], gain_ref[...]).astype(BF16)
        cos = cos_ref[rows, :]
        sin_signed = sin_ref[rows, :]

        def proj(col0, ncols):
            return _dot(h, w_ref[:, col0:col0 + ncols])

        def rope(p, mult=None):
            chunks = []
            for c in range(p.shape[1] // LANES):
                r = _rope_chunk(p[:, c * LANES:(c + 1) * LANES], cos, sin_signed, first_half)
                chunks.append(r if mult is None else r * mult)
            return chunks

        def dup_heads(p):
            swapped = pltpu.roll(p, HEAD_DIM, 1)
            return [jnp.where(low_head, p, swapped), jnp.where(low_head, swapped, p)]

        def windowed():
            for c, chunk in enumerate(rope(proj(0, A_Q_COLS), scale)):
                qa_ref[0, rows, c * LANES:(c + 1) * LANES] = chunk.astype(BF16)
            kv = proj(A_Q_COLS, 2 * A_KV_COLS)
            ka = rope(kv[:, :A_KV_COLS])[0]
            for c, chunk in enumerate(dup_heads(ka) + dup_heads(kv[:, A_KV_COLS:])):
                kva_ref[0, rows, c * LANES:(c + 1) * LANES] = chunk.astype(BF16)

        def dilated(g):
            dil = B_PATTERNS[g][1]
            for kind in range(3):
                col0 = A_COLS + kind * B_PROJ_COLS + g * B_GROUP_COLS
                p = proj(col0, B_GROUP_COLS)
                if kind == 0:
                    chunks = rope(p, scale)
                elif kind == 1:
                    chunks = rope(p)
                else:
                    chunks = [p[:, c * LANES:(c + 1) * LANES] for c in range(B_GROUP_COLS // LANES)]
                o_ref = out_refs[g]
                sub_rows = slice(sub * t // dil, (sub + 1) * t // dil)
                col = lambda c: slice(kind * B_GROUP_COLS + c * LANES, kind * B_GROUP_COLS + (c + 1) * LANES)
                if dil == 1:
                    for c, chunk in enumerate(chunks):
                        o_ref[0, 0, sub_rows, col(c)] = chunk.astype(BF16)
                else:
                    for c, chunk in enumerate(chunks):
                        slab_ref[sub, kind, c] = chunk
                    for c in range(len(chunks)):
                        src = slab_ref.at[sub, kind, c]
                        if dil > DEINTERLEAVE_STRIDE:
                            assert dil == DEINTERLEAVE_STRIDE ** 2
                            for a in range(DEINTERLEAVE_STRIDE):
                                stage_ref[sub, kind, c, a] = src[pl.ds(a, t // DEINTERLEAVE_STRIDE,
                                                                       stride=DEINTERLEAVE_STRIDE), :]
                        for res in range(dil):
                            if dil > DEINTERLEAVE_STRIDE:
                                picked = stage_ref[sub, kind, c, res % DEINTERLEAVE_STRIDE,
                                                   pl.ds(res // DEINTERLEAVE_STRIDE, t // dil,
                                                         stride=DEINTERLEAVE_STRIDE), :]
                            else:
                                picked = src[pl.ds(res, t // dil, stride=dil), :]
                            o_ref[0, res, sub_rows, col(c)] = picked.astype(BF16)

        dilated(2)
        dilated(1)
        windowed()
        dilated(0)

    for sub in range(SUB_TILES):
        tile(sub)


def _in_proj(x, gain, w_in, cos, sin_signed):
    b, s, _ = x.shape
    t = SUB_TILES * ROW_TILE
    grid = (b, s // t)
    row_map = lambda bi, i: (bi, i, 0)
    out_shape = [
        jax.ShapeDtypeStruct((b, s, A_Q_COLS), BF16),
        jax.ShapeDtypeStruct((b, s, A_KVA_COLS), BF16),
    ]
    out_specs = [
        pl.BlockSpec((1, t, A_Q_COLS), row_map),
        pl.BlockSpec((1, t, A_KVA_COLS), row_map),
    ]
    for _, dil in B_PATTERNS:
        out_shape.append(jax.ShapeDtypeStruct((b, dil, s // dil, B_QKV_COLS), BF16))
        out_specs.append(pl.BlockSpec((1, dil, t // dil, B_QKV_COLS), lambda bi, i: (bi, 0, i, 0)))
    return pl.pallas_call(
        _in_proj_kernel,
        grid=grid,
        in_specs=[
            pl.BlockSpec((1, t, D_MODEL), row_map),
            _const_spec((1, D_MODEL)),
            _const_spec((D_MODEL, IN_COLS)),
            pl.BlockSpec((t, LANES), lambda bi, i: (i, 0)),
            pl.BlockSpec((t, LANES), lambda bi, i: (i, 0)),
        ],
        out_specs=out_specs,
        out_shape=out_shape,
        scratch_shapes=[
            pltpu.VMEM((SUB_TILES, 3, B_GROUP_COLS // LANES, ROW_TILE, LANES), F32),
            pltpu.VMEM((SUB_TILES, 3, B_GROUP_COLS // LANES, DEINTERLEAVE_STRIDE,
                        ROW_TILE // DEINTERLEAVE_STRIDE, LANES), F32),
            pltpu.VMEM((D_MODEL, IN_COLS), BF16),
        ],
        compiler_params=pltpu.CompilerParams(
            dimension_semantics=("arbitrary", "arbitrary"), vmem_limit_bytes=VMEM_LIMIT_BYTES),
        name="in_proj",
    )(x, gain, w_in, cos, sin_signed)


def _attn_b_kernel(bias_ref, q_ref, kv_ref, *rest, dil, n_cast):
    o_ref, lse_ref = rest[n_cast:n_cast + 2]
    _cast_chunks(rest[:n_cast], rest[n_cast + 2:2 * n_cast + 2])
    two_pass = dil > DEINTERLEAVE_STRIDE
    if two_pass:
        assert dil == DEINTERLEAVE_STRIDE ** 2
        stage_ref = rest[2 * n_cast + 2]
    sub_len = kv_ref.shape[2]
    sub_rows = q_ref.shape[2]
    blocks = sub_rows // B_Q_BLOCK
    step = pl.program_id(1)
    low_head = lax.broadcasted_iota(jnp.int32, (B_Q_BLOCK, LANES), 1) < HEAD_DIM
    low_key = lax.broadcasted_iota(jnp.int32, (B_KEY_WINDOW, LANES), 1) < HEAD_DIM

    def units(it, carry):
        work = []
        for uu in range(B_UNROLL):
            u = it * B_UNROLL + uu
            res = u // blocks
            r0 = (u % blocks) * B_Q_BLOCK
            q_pos = (step * sub_rows + r0) // B_HALF_WINDOW
            start_pos = jnp.clip(q_pos - 1, 0, (sub_len - B_KEY_WINDOW) // B_HALF_WINDOW)
            start = start_pos * B_HALF_WINDOW
            bias = bias_ref[q_pos - start_pos]
            q_blk = q_ref[0, res, pl.ds(r0, B_Q_BLOCK), 0:B_GROUP_COLS]
            k_win = kv_ref[0, res, pl.ds(start, B_KEY_WINDOW), B_GROUP_COLS:2 * B_GROUP_COLS]
            v_win = kv_ref[0, res, pl.ds(start, B_KEY_WINDOW), 2 * B_GROUP_COLS:3 * B_GROUP_COLS]
            if dil == 1:
                dst = (None, pl.ds(r0, B_Q_BLOCK))
            elif two_pass:
                dst = (res % DEINTERLEAVE_STRIDE,
                       pl.ds(r0 * DEINTERLEAVE_STRIDE + res // DEINTERLEAVE_STRIDE, B_Q_BLOCK,
                             stride=DEINTERLEAVE_STRIDE))
            else:
                dst = (None, pl.ds(r0 * dil + res, B_Q_BLOCK, stride=dil))
            rows = dst
            for pr in range(B_HEADS // 2):
                qp = q_blk[:, pr * LANES:(pr + 1) * LANES]
                zero = jnp.zeros_like(qp)
                lhs = jnp.concatenate([jnp.where(low_head, qp, zero), jnp.where(low_head, zero, qp)], axis=0)
                sc = _dot_nt(lhs, k_win[:, pr * LANES:(pr + 1) * LANES])
                work.append((rows, pr, bias, v_win[:, pr * LANES:(pr + 1) * LANES], sc))

        soft = []
        for rows, pr, bias, vp, sc in work:
            sc = sc + bias
            m = jnp.max(sc, axis=-1, keepdims=True)
            soft.append((rows, pr, vp, m, jnp.exp2(sc - m).astype(BF16)))

        for rows, pr, vp, m, p in soft:
            one = jnp.ones_like(vp)
            top = _dot(p[:B_Q_BLOCK], jnp.where(low_key, vp, one))
            bot = _dot(p[B_Q_BLOCK:], jnp.where(low_key, one, vp))
            denom = pltpu.roll(jnp.where(low_head, bot, top), HEAD_DIM, 1)
            o_val = jnp.where(low_head, top, bot) / denom
            lse_val = jnp.where(low_head, m[:B_Q_BLOCK], m[B_Q_BLOCK:]) + jnp.log2(denom)
            cls, where = rows
            if two_pass:
                stage_ref[0, pr, cls, where, :] = o_val
                stage_ref[1, pr, cls, where, :] = lse_val
            else:
                o_ref[0, pr, where, :] = o_val
                lse_ref[0, pr, where, :] = lse_val
        return carry

    lax.fori_loop(0, dil * blocks // B_UNROLL, units, 0)
    if two_pass:
        for which, out in enumerate((o_ref, lse_ref)):
            for pr in range(B_HEADS // 2):
                for a in range(DEINTERLEAVE_STRIDE):
                    out[0, pr, pl.ds(a, DEINTERLEAVE_STRIDE * sub_rows, stride=DEINTERLEAVE_STRIDE), :] = (
                        stage_ref[which, pr, a])


def _attn_b(bias, qkv, dil, cast_weights=()):
    b, _, sub_len, _ = qkv.shape
    s = sub_len * dil
    sub_rows = B_SPAN // dil
    nslab = B_GROUP_COLS // LANES
    out_sds = jax.ShapeDtypeStruct((b, nslab, s, LANES), F32)
    out_spec = pl.BlockSpec((1, nslab, B_SPAN, LANES), lambda bi, i: (bi, 0, i, 0))
    steps_per_seq = s // B_SPAN
    chunk_specs = _row_chunk_specs(cast_weights, steps_per_seq, b * steps_per_seq)
    return pl.pallas_call(
        functools.partial(_attn_b_kernel, dil=dil, n_cast=len(cast_weights)),
        grid=(b, steps_per_seq),
        in_specs=[
            _const_spec(bias.shape),
            pl.BlockSpec((1, dil, sub_rows, B_QKV_COLS), lambda bi, i: (bi, 0, i, 0)),
            pl.BlockSpec((1, dil, sub_len, B_QKV_COLS), lambda bi, i: (bi, 0, 0, 0)),
        ] + chunk_specs,
        out_specs=[out_spec, out_spec] + chunk_specs,
        out_shape=[out_sds, out_sds] + [jax.ShapeDtypeStruct(w.shape, BF16) for w in cast_weights],
        scratch_shapes=([pltpu.VMEM((2, nslab, DEINTERLEAVE_STRIDE, DEINTERLEAVE_STRIDE * sub_rows, LANES), F32)]
                        if dil > DEINTERLEAVE_STRIDE else []),
        compiler_params=pltpu.CompilerParams(
            dimension_semantics=("parallel", "parallel"), vmem_limit_bytes=VMEM_LIMIT_BYTES),
        name=f"attn_b_d{dil}",
    )(bias, qkv, qkv, *cast_weights)


def _window(main_ref, prev_ref, next_ref, lead, lo, hi, length, cols):
    parts = []
    if lo < 0:
        halo = prev_ref.shape[-2]
        parts.append(prev_ref[lead + (slice(halo + lo, halo), cols)])
        lo = 0
    parts.append(main_ref[lead + (slice(lo, min(hi, length)), cols)])
    if hi > length:
        parts.append(next_ref[lead + (slice(0, hi - length), cols)])
    return parts[0] if len(parts) == 1 else jnp.concatenate(parts, axis=0)


def _mix_attn_kernel(sink_ref, bias_a_ref, bias_b_ref, x_ref,
                     qa_ref, kva_ref, kva_p_ref, kva_n_ref,
                     b0_ref, b0_p_ref, b0_n_ref, b1_ref, b1_p_ref, b1_n_ref,
                     o2_ref, l2_ref,
                     g_pre_ref, w_gate_ref, b_gate_ref, w_a_ref, w_b_ref, w_out_ref, g_post_ref,
                     *rest, n_cast):
    out_ref = rest[n_cast]
    ya_ref, ob_ref = rest[2 * n_cast + 1:]
    _cast_chunks(rest[:n_cast], rest[n_cast + 1:2 * n_cast + 1])
    t = ROW_TILE
    step_rows = t
    i = pl.program_id(1)
    last = pl.num_programs(1) - 1
    low_head = lax.broadcasted_iota(jnp.int32, (A_Q_BLOCK, LANES), 1) < HEAD_DIM
    dil1 = B_PATTERNS[1][1]
    sub_rows = step_rows // dil1

    def edge_index(first, final):
        idx = 1
        if first:
            idx = jnp.where(i == 0, 0, idx)
        if final:
            idx = jnp.where(i == last, 2, idx)
        return idx

    def scores(grp_idx, part):
        work_a, work_b = [], []
        n = grp_idx
        lo, hi = (n - 1) * A_Q_BLOCK, (n + 2) * A_Q_BLOCK
        bias = bias_a_ref[edge_index(n == 0, hi > step_rows)]
        q_blk = qa_ref[0, n * A_Q_BLOCK:(n + 1) * A_Q_BLOCK, :]
        for j in range(A_KV_HEADS if part == 'a' else 0):
            kcols = slice(j * LANES, (j + 1) * LANES)
            vcols = slice(2 * A_KV_COLS + j * LANES, 2 * A_KV_COLS + (j + 1) * LANES)
            kd = _window(kva_ref, kva_p_ref, kva_n_ref, (0,), lo, hi, step_rows, kcols)
            vd = _window(kva_ref, kva_p_ref, kva_n_ref, (0,), lo, hi, step_rows, vcols)
            lhs = []
            for g in HEAD_ORDER:
                c0 = j * A_GROUP * HEAD_DIM + (g // 2) * LANES
                pair = q_blk[:, c0:c0 + LANES]
                keep = low_head if g % 2 == 0 else jnp.logical_not(low_head)
                lhs.append(jnp.where(keep, pair, jnp.zeros_like(pair)))
            sc = _dot_nt(jnp.concatenate(lhs, axis=0), kd)
            work_a.append((n, j, bias, vd, sc))
        units = [(0, (0, 0), grp_idx, step_rows), (1, (0, grp_idx), 0, sub_rows)]
        for grp, lead, n, length in (units if part == 'b' else []):
            m_ref, p_ref, n_ref = (b0_ref, b0_p_ref, b0_n_ref) if grp == 0 else (b1_ref, b1_p_ref, b1_n_ref)
            lo, hi = n * B_Q_BLOCK - B_HALF_WINDOW, (n + 1) * B_Q_BLOCK + B_HALF_WINDOW
            bias = bias_b_ref[edge_index(lo < 0, hi > length)]
            if grp == 0:
                rows = pl.ds(n * B_Q_BLOCK, B_Q_BLOCK)
            else:
                rows = pl.ds(n * B_Q_BLOCK * dil1 + lead[1], B_Q_BLOCK, stride=dil1)
            for pr in range(B_HEADS // 2):
                cols = [slice(kind * B_GROUP_COLS + pr * LANES, kind * B_GROUP_COLS + (pr + 1) * LANES)
                        for kind in range(3)]
                qp = m_ref[lead + (slice(n * B_Q_BLOCK, (n + 1) * B_Q_BLOCK), cols[0])]
                kp = _window(m_ref, p_ref, n_ref, lead, lo, hi, length, cols[1])
                vp = _window(m_ref, p_ref, n_ref, lead, lo, hi, length, cols[2])
                zero = jnp.zeros_like(qp)
                lhs = jnp.concatenate([jnp.where(low_head, qp, zero), jnp.where(low_head, zero, qp)], axis=0)
                work_b.append((grp, rows, pr, bias, vp, _dot_nt(lhs, kp)))
        return work_a, work_b

    def softmaxes(work):
        work_a, work_b = work
        soft_a, soft_b = [], []
        for n, j, bias, vd, sc in work_a:
            probs, denoms = {}, {}
            for slot, g in enumerate(HEAD_ORDER):
                sg = sc[slot * A_Q_BLOCK:(slot + 1) * A_Q_BLOCK]
                sg = jnp.concatenate([sg[:, :A_Q_BLOCK] + bias[:, :A_Q_BLOCK],
                                      sg[:, A_Q_BLOCK:2 * A_Q_BLOCK],
                                      sg[:, 2 * A_Q_BLOCK:] + bias[:, 2 * A_Q_BLOCK:]], axis=-1)
                sink = sink_ref[j * A_GROUP + g] * LOG2_E
                m = jnp.maximum(jnp.max(sg, axis=-1, keepdims=True), sink)
                p = jnp.exp2(sg - m)
                probs[g] = p.astype(BF16)
                denoms[g] = jnp.sum(p, axis=-1, keepdims=True) + jnp.exp2(sink - m)
            soft_a.append((n, j, vd, probs, denoms))
        for grp, rows, pr, bias, vp, sc in work_b:
            sc = sc + bias
            m = jnp.max(sc, axis=-1, keepdims=True)
            p = jnp.exp2(sc - m)
            soft_b.append((grp, rows, pr, vp, (m, jnp.sum(p, axis=-1, keepdims=True)), p.astype(BF16)))
        return soft_a, soft_b

    def values(soft):
        soft_a, soft_b = soft
        pairs = range(A_GROUP // 2)
        for n, j, vd, probs, denoms in soft_a:
            r = _dot(jnp.concatenate([probs[g] for g in range(A_GROUP)], axis=0), vd)
            for pr in pairs:
                even = r[(2 * pr) * A_Q_BLOCK:(2 * pr + 1) * A_Q_BLOCK] / denoms[2 * pr]
                odd = r[(2 * pr + 1) * A_Q_BLOCK:(2 * pr + 2) * A_Q_BLOCK] / denoms[2 * pr + 1]
                c0 = j * A_GROUP * HEAD_DIM + pr * LANES
                ya_ref[n * A_Q_BLOCK:(n + 1) * A_Q_BLOCK, c0:c0 + LANES] = (
                    jnp.where(low_head, even, odd)).astype(BF16)
        for grp, rows, pr, vp, (m, l), p in soft_b:
            o = _dot(p, vp) / l
            lse = m + jnp.log2(l)
            ob_ref[grp, 0, pr, rows, :] = jnp.where(low_head, o[:B_Q_BLOCK], o[B_Q_BLOCK:])
            ob_ref[grp, 1, pr, rows, :] = jnp.where(low_head, lse[:B_Q_BLOCK], lse[B_Q_BLOCK:])

    n_groups = t // A_Q_BLOCK
    assert step_rows == t and n_groups == dil1 == t // B_Q_BLOCK
    halves = [(g, part) for g in range(n_groups) for part in ('b', 'a')]
    gate_chunk = 2 * D_MODEL // len(halves)
    x = x_ref[0]
    h = _rms_norm(x, g_pre_ref[...]).astype(BF16)

    def gate_logits(j):
        cols = slice(j * gate_chunk, (j + 1) * gate_chunk)
        return _dot(h, w_gate_ref[:, cols]) + b_gate_ref[:, cols]

    z = [None] * len(halves)
    work = scores(*halves[0])
    z[0] = gate_logits(0)
    for k in range(len(halves)):
        soft = softmaxes(work)
        if k + 1 < len(halves):
            work = scores(*halves[k + 1])
            z[k + 1] = gate_logits(k + 1)
        values(soft)

    yb = []
    for c in range(B_GROUP_COLS // LANES):
        l0, l1, l2 = ob_ref[0, 1, c], ob_ref[1, 1, c], l2_ref[0, c]
        m = jnp.maximum(jnp.maximum(l0, l1), l2)
        e0, e1, e2 = jnp.exp2(l0 - m), jnp.exp2(l1 - m), jnp.exp2(l2 - m)
        num = e0 * ob_ref[0, 0, c] + e1 * ob_ref[1, 0, c] + e2 * o2_ref[0, c]
        yb.append((num / (e0 + e1 + e2)).astype(BF16))
    yb = jnp.concatenate(yb, axis=-1)
    gates = jax.nn.sigmoid(jnp.concatenate(z, axis=-1))
    merged = (gates[:, :D_MODEL] * _dot(ya_ref[...], w_a_ref[...])
              + gates[:, D_MODEL:] * _dot(yb, w_b_ref[...]))
    mix = _dot(merged.astype(BF16), w_out_ref[...])
    out_ref[0] = x + _rms_norm(mix, g_post_ref[...])


def _halo_specs(lead_blocks, rows, halo, total_rows, cols):
    nlead = len(lead_blocks)
    per = rows // halo
    nhalo = total_rows // halo
    zeros = (0,) * nlead
    main = pl.BlockSpec((1,) + lead_blocks + (rows, cols), lambda bi, i: (bi,) + zeros + (i, 0))
    prev = pl.BlockSpec((1,) + lead_blocks + (halo, cols),
                        lambda bi, i: (bi,) + zeros + (jnp.maximum(i * per - 1, 0), 0))
    nxt = pl.BlockSpec((1,) + lead_blocks + (halo, cols),
                       lambda bi, i: (bi,) + zeros + (jnp.minimum((i + 1) * per, nhalo - 1), 0))
    return [main, prev, nxt]


def _mix_attn(x, sink, bias_a, bias_b, qa, kva, qkv_b0, qkv_b1, ob2,
              g_pre, w_gate, b_gate, w_a, w_b, w_out, g_post, cast_weights=()):
    b, s, _ = x.shape
    t = ROW_TILE
    dil1 = B_PATTERNS[1][1]
    nslab = B_GROUP_COLS // LANES
    row_map = lambda bi, i: (bi, i, 0)
    slab_spec = pl.BlockSpec((1, nslab, t, LANES), lambda bi, i: (bi, 0, i, 0))
    chunk_specs = _row_chunk_specs(cast_weights, s // t, b * s // t)
    in_specs = ([pl.BlockSpec(memory_space=pltpu.SMEM), _const_spec(bias_a.shape), _const_spec(bias_b.shape),
                 pl.BlockSpec((1, t, D_MODEL), row_map),
                 pl.BlockSpec((1, t, A_Q_COLS), row_map)]
                + _halo_specs((), t, A_Q_BLOCK, s, A_KVA_COLS)
                + _halo_specs((1,), t, B_HALF_WINDOW, s, B_QKV_COLS)
                + _halo_specs((dil1,), t // dil1, B_HALF_WINDOW, s // dil1, B_QKV_COLS)
                + [slab_spec, slab_spec,
                   _const_spec((1, D_MODEL)),
                   _const_spec((D_MODEL, 2 * D_MODEL)),
                   _const_spec((1, 2 * D_MODEL)),
                   _const_spec((A_Q_COLS, D_MODEL)),
                   _const_spec((B_GROUP_COLS, D_MODEL)),
                   _const_spec((D_MODEL, D_MODEL)),
                   _const_spec((1, D_MODEL))])
    return pl.pallas_call(
        functools.partial(_mix_attn_kernel, n_cast=len(cast_weights)),
        grid=(b, s // t),
        in_specs=in_specs + chunk_specs,
        out_specs=[pl.BlockSpec((1, t, D_MODEL), row_map)] + chunk_specs,
        out_shape=[jax.ShapeDtypeStruct((b, s, D_MODEL), F32)]
        + [jax.ShapeDtypeStruct(w.shape, BF16) for w in cast_weights],
        scratch_shapes=[
            pltpu.VMEM((t, A_Q_COLS), BF16),
            pltpu.VMEM((2, 2, nslab, t, LANES), F32),
        ],
        compiler_params=pltpu.CompilerParams(
            dimension_semantics=("parallel", "parallel"), vmem_limit_bytes=VMEM_LIMIT_BYTES),
        name="mix_attn",
    )(sink, bias_a, bias_b, x, qa, kva, kva, kva, qkv_b0, qkv_b0, qkv_b0, qkv_b1, qkv_b1, qkv_b1,
      ob2[0], ob2[1], g_pre, w_gate, b_gate, w_a, w_b, w_out, g_post, *cast_weights)


def _gelu_tanh(x):
    c = -2.0 * math.sqrt(2.0 / math.pi) * LOG2_E
    return x / (1.0 + jnp.exp2(x * (c + (c * 0.044715) * (x * x))))


def _conv_ffn_kernel(x_ref, prev_ref, next_ref, g_pre_ref, w_up_ref, conv_w_ref, conv_b_ref,
                     w_down_ref, g_post_ref, out_ref, u_ref, acc_ref, perm_ref):
    t = ROW_TILE
    half = t // 2
    i = pl.program_id(1)
    last = pl.num_programs(1) - 1
    row = lax.broadcasted_iota(jnp.int32, (t + 2 * HALO, 1), 0)
    starts = [sum(FF_CHUNKS[:c]) for c in range(len(FF_CHUNKS))]

    def col_starts(c):
        return (starts[c], D_FF + starts[c])

    def project(h, c, slot):
        for part, col0 in enumerate(col_starts(c)):
            up = _dot(h, w_up_ref[:, col0:col0 + FF_CHUNKS[c]])
            for j in range(FF_CHUNKS[c] // LANES):
                u_ref[2 * slot + part, j] = up[:, j * LANES:(j + 1) * LANES]

    def conv(c, slot, part):
        buf, col0 = 2 * slot + part, col_starts(c)[part]
        pieces = []
        for j in range(FF_CHUNKS[c] // LANES):
            cols = slice(col0 + j * LANES, col0 + (j + 1) * LANES)
            w0, w1, w2 = conv_w_ref[0:1, cols], conv_w_ref[1:2, cols], conv_w_ref[2:3, cols]
            bias = conv_b_ref[:, cols]
            r = [u_ref[buf, j, pl.ds(HALO - 1 + k, half, stride=2), :] for k in range(4)]
            even = r[0] * w0 + r[1] * w1 + r[2] * w2 + bias
            odd = r[1] * w0 + r[2] * w1 + r[3] * w2 + bias
            pieces.append(jnp.concatenate([even, odd], axis=0))
        return jnp.concatenate(pieces, axis=-1)

    def tile(sub, carry):
        lo = pl.multiple_of(sub * t, t)
        x = x_ref[0, pl.ds(lo, t), :]
        first, final = sub == 0, sub == FFN_SUB_TILES - 1
        inner_before = x_ref[0, pl.ds(pl.multiple_of(jnp.maximum(lo - HALO, 0), HALO), HALO), :]
        inner_after = x_ref[0, pl.ds(pl.multiple_of(jnp.minimum(lo + t, (FFN_SUB_TILES - 1) * t), HALO), HALO), :]
        before = jnp.where(first, prev_ref[0], inner_before)
        after = jnp.where(final, next_ref[0], inner_after)
        hn = _rms_norm(jnp.concatenate([before, x, after], axis=0), g_pre_ref[...])
        hn = jnp.where(jnp.logical_or(row >= HALO, jnp.logical_or(i > 0, sub > 0)), hn, 0.0)
        hn = jnp.where(jnp.logical_or(row < HALO + t, jnp.logical_or(i < last, sub < FFN_SUB_TILES - 1)), hn, 0.0)
        h = hn.astype(BF16)

        n_chunks = len(FF_CHUNKS)
        for c in range(min(FF_LOOKAHEAD, n_chunks)):
            project(h, c, c % (FF_LOOKAHEAD + 1))
        for c in range(n_chunks):
            if c + FF_LOOKAHEAD < n_chunks:
                project(h, c + FF_LOOKAHEAD, (c + FF_LOOKAHEAD) % (FF_LOOKAHEAD + 1))
            slot = c % (FF_LOOKAHEAD + 1)
            act = (_gelu_tanh(conv(c, slot, 0)) * conv(c, slot, 1)).astype(BF16)
            part = _dot(act, w_down_ref[starts[c]:starts[c] + FF_CHUNKS[c], :])
            if c == 0:
                acc_ref[...] = part
            else:
                acc_ref[...] += part
        y = _rms_norm(acc_ref[...], g_post_ref[...])
        for j in range(D_MODEL // LANES):
            perm_ref[j, pl.ds(0, half, stride=2), :] = y[:half, j * LANES:(j + 1) * LANES]
            perm_ref[j, pl.ds(1, half, stride=2), :] = y[half:, j * LANES:(j + 1) * LANES]
        y_nat = jnp.concatenate([perm_ref[j] for j in range(D_MODEL // LANES)], axis=-1)
        out_ref[0, pl.ds(lo, t), :] = x + y_nat
        return carry

    lax.fori_loop(0, FFN_SUB_TILES, tile, 0)


def _conv_ffn(x, g_pre, w_up, conv_w, conv_b, w_down, g_post):
    b, s, _ = x.shape
    t = FFN_SUB_TILES * ROW_TILE
    tiles = s // t
    halo_blocks_per_tile = t // HALO
    n_halo_blocks = s // HALO
    row_map = lambda bi, i: (bi, i, 0)
    prev_map = lambda bi, i: (bi, jnp.maximum(i * halo_blocks_per_tile - 1, 0), 0)
    next_map = lambda bi, i: (bi, jnp.minimum((i + 1) * halo_blocks_per_tile, n_halo_blocks - 1), 0)
    return pl.pallas_call(
        _conv_ffn_kernel,
        grid=(b, tiles),
        in_specs=[
            pl.BlockSpec((1, t, D_MODEL), row_map),
            pl.BlockSpec((1, HALO, D_MODEL), prev_map),
            pl.BlockSpec((1, HALO, D_MODEL), next_map),
            _const_spec((1, D_MODEL)),
            _const_spec((D_MODEL, 2 * D_FF)),
            _const_spec((3, 2 * D_FF)),
            _const_spec((1, 2 * D_FF)),
            _const_spec((D_FF, D_MODEL)),
            _const_spec((1, D_MODEL)),
        ],
        out_specs=pl.BlockSpec((1, t, D_MODEL), row_map),
        out_shape=jax.ShapeDtypeStruct((b, s, D_MODEL), F32),
        scratch_shapes=[
            pltpu.VMEM((2 * (FF_LOOKAHEAD + 1), FF_CHUNK // LANES, ROW_TILE + 2 * HALO, LANES), F32),
            pltpu.VMEM((ROW_TILE, D_MODEL), F32),
            pltpu.VMEM((D_MODEL // LANES, ROW_TILE, LANES), F32),
        ],
        compiler_params=pltpu.CompilerParams(
            dimension_semantics=("parallel", "parallel"), vmem_limit_bytes=VMEM_LIMIT_BYTES),
        name="conv_ffn",
    )(x, x, x, g_pre, w_up, conv_w, conv_b, w_down, g_post)


def _rope_tables(seq_len):
    half = HEAD_DIM // 2
    inv = ROPE_THETA ** (-np.arange(half, dtype=np.float64) / half)
    ang = np.arange(seq_len, dtype=np.float64)[:, None] * inv[None, :]
    cos, sin = np.cos(ang), np.sin(ang)
    cos_head = np.concatenate([cos, cos], axis=-1)
    sin_head = np.concatenate([-sin, sin], axis=-1)
    reps = LANES // HEAD_DIM
    return (jnp.asarray(np.tile(cos_head, (1, reps)), dtype=F32),
            jnp.asarray(np.tile(sin_head, (1, reps)), dtype=F32))


def _band_bias(q_rows, stacked, keys, half_window, lead):
    i = (np.arange(stacked * q_rows) % q_rows)[None, :, None]
    j = np.arange(keys)[None, None, :]
    e = np.arange(3)[:, None, None]
    band = np.abs(j - lead - i) <= half_window
    inside = np.logical_and(np.logical_or(e != 0, j >= lead), np.logical_or(e != 2, j < lead + q_rows))
    return jnp.asarray(np.where(np.logical_and(band, inside), 0.0, NEG_INF), dtype=F32)


def _far_band_bias():
    off = np.arange(3)[:, None, None] * B_HALF_WINDOW
    i = (np.arange(2 * B_Q_BLOCK) % B_Q_BLOCK)[None, :, None]
    j = np.arange(B_KEY_WINDOW)[None, None, :]
    return jnp.asarray(np.where(np.abs(j - off - i) <= B_HALF_WINDOW, 0.0, NEG_INF), dtype=F32)


def kernel(x, norm_mix_pre, w_in, sink, w_branch_a, w_branch_b, w_gate, b_gate, w_out,
           norm_mix_post, norm_ffn_pre, w_up, conv_w, conv_b, w_down, norm_ffn_post):
    b, s, d = x.shape
    assert d == D_MODEL and s % B_SPAN == 0 and s % (SUB_TILES * ROW_TILE) == 0
    cos, sin_signed = _rope_tables(s)
    bias_a = _band_bias(A_Q_BLOCK, 1, A_KEY_WINDOW, A_HALF_WINDOW, A_Q_BLOCK)
    bias_b = _band_bias(B_Q_BLOCK, 2, B_KEY_WINDOW, B_HALF_WINDOW, B_HALF_WINDOW)
    bias_b_far = _far_band_bias()
    for layer in range(norm_mix_pre.shape[0]):
        proj = _in_proj(x, norm_mix_pre[layer][None], w_in[layer], cos, sin_signed)
        qa, kva, qkv_b0, qkv_b1, qkv_b2 = proj
        o2, l2, w_gate_h, w_a_h, w_b_h, w_out_h = _attn_b(
            bias_b_far, qkv_b2, B_PATTERNS[2][1],
            (w_gate[layer], w_branch_a[layer], w_branch_b[layer], w_out[layer]))
        ob2 = (o2, l2)
        x, w_up_h, w_down_h = _mix_attn(
                      x, sink[layer], bias_a, bias_b, qa, kva, qkv_b0, qkv_b1, ob2,
                      norm_mix_pre[layer][None],
                      w_gate_h, b_gate[layer][None], w_a_h, w_b_h, w_out_h, norm_mix_post[layer][None],
                      (w_up[layer], w_down[layer]))
        x = _conv_ffn(x, norm_ffn_pre[layer][None],
                      w_up_h, conv_w[layer], conv_b[layer][None], w_down_h,
                      norm_ffn_post[layer][None])
    return x
```
